```python
import math
import jax
import jax.numpy as jnp
from jax import lax
import numpy as np

D_MODEL = 1024
BATCH = 8
SEQ = 2048
DEPTH = 2
DEC_BATCH = 32
DEC_SEQ = 1
PAST_LEN = 8192
PAGE_SIZE = 128

HEAD_DIM = 64
ROT_DIM = HEAD_DIM // 4
ROPE_THETA = 500000.0
EPS = 1e-6
QBLK = 128
H_DIFF = D_MODEL // (4 * HEAD_DIM)
DIFF_LAMBDA_INIT = 0.2
H_SB = D_MODEL // (2 * HEAD_DIM)
DIFF_W = H_DIFF * 2 * HEAD_DIM
SB_W = H_SB * HEAD_DIM
SPLIT0 = [DIFF_W, 2 * DIFF_W, 3 * DIFF_W, 3 * DIFF_W + SB_W, 3 * DIFF_W + 2 * SB_W]
W_IN0 = 3 * DIFF_W + 3 * SB_W
W_OUT0 = DIFF_W + SB_W
H_NSA = D_MODEL // HEAD_DIM
H_NSA_KV = H_NSA // 4
Q_PER_KV = H_NSA // H_NSA_KV
L_CMP = 32
CMP_STRIDE = 16
CMP_HID = 4 * HEAD_DIM
L_SEL = 64
N_SEL = 16
WINDOW = 512
NSA_QBLK = 64
FORCE_BONUS = 1.0e4
NSA_Q_W = H_NSA * HEAD_DIM
NSA_KV_W = H_NSA_KV * HEAD_DIM
SPLIT1 = [NSA_Q_W, NSA_Q_W + NSA_KV_W, NSA_Q_W + 2 * NSA_KV_W, NSA_Q_W + 3 * NSA_KV_W,
          NSA_Q_W + 4 * NSA_KV_W, NSA_Q_W + 5 * NSA_KV_W, NSA_Q_W + 6 * NSA_KV_W]
W_IN1 = NSA_Q_W + 6 * NSA_KV_W + 3 * H_NSA
D_FF = ((8 * D_MODEL // 3 + 127) // 128) * 128
N_EXPERTS = 8
TOP_K = 2
D_FF_EXPERT = 7 * D_MODEL // 2
POOL_EXTRA = 4

kernel_name = 'hybrid_diff_sb_nsa_moe_decode_step'


def rms_norm(x, g):
    xf = x.astype(jnp.float32)
    xf = xf * lax.rsqrt(jnp.mean(xf * xf, axis=-1, keepdims=True) + EPS)
    return (xf * g.astype(jnp.float32)).astype(x.dtype)


def partial_rope(x, pos):
    half = ROT_DIM // 2
    inv = ROPE_THETA ** (-jnp.arange(half, dtype=jnp.float32) * 2.0 / ROT_DIM)
    ang = pos.astype(jnp.float32)[:, None] * inv[None, :]
    shp = (ang.shape[0],) + (1,) * (x.ndim - 3) + (half,)
    cos = jnp.cos(ang).reshape(shp)
    sin = jnp.sin(ang).reshape(shp)
    xr = x[..., :ROT_DIM].astype(jnp.float32)
    x1, x2 = xr[..., :half], xr[..., half:]
    rot = jnp.concatenate([x1 * cos - x2 * sin, x2 * cos + x1 * sin], axis=-1).astype(x.dtype)
    return jnp.concatenate([rot, x[..., ROT_DIM:]], axis=-1)


def masked_softmax(s, mask):
    s = jnp.where(mask, s, -jnp.inf)
    m = jnp.max(s, axis=-1, keepdims=True)
    m = jnp.where(jnp.isfinite(m), m, 0.0)
    e = jnp.where(mask, jnp.exp(s - m), 0.0)
    return e / jnp.maximum(jnp.sum(e, axis=-1, keepdims=True), 1e-30)


def sweep_query_blocks(fn, q_arrays, qpos, blk):
    B, S = q_arrays[0].shape[:2]
    nb = S // blk
    split = lambda a: a.reshape((B, nb, blk) + a.shape[2:]).swapaxes(0, 1)
    out = lax.map(lambda args: fn(*args), tuple(split(a) for a in q_arrays) + (qpos.reshape(nb, blk),))
    return out.swapaxes(0, 1).reshape((B, S) + out.shape[3:])


def gather_pages(pool, page_table):
    g = pool[page_table]
    return g.reshape((g.shape[0], g.shape[1] * g.shape[2]) + g.shape[3:])


def diff_lambda(lam_p):
    lp = lam_p.astype(jnp.float32)
    return jnp.exp(jnp.sum(lp[0] * lp[1])) - jnp.exp(jnp.sum(lp[2] * lp[3])) + DIFF_LAMBDA_INIT


def diff_attn_core(q, k, v, mask, lam, subln_g):
    s = jnp.einsum('bqhmd,bkhmd->bhmqk', q.astype(jnp.float32), k.astype(jnp.float32)) * HEAD_DIM ** -0.5
    p = masked_softmax(s, mask[None, None, None])
    p = p[:, :, 0] - lam * p[:, :, 1]
    o = jnp.einsum('bhqk,bkhe->bqhe', p, v.astype(jnp.float32))
    o = rms_norm(o, subln_g) * (1.0 - DIFF_LAMBDA_INIT)
    return o.reshape(o.shape[0], o.shape[1], -1).astype(q.dtype)


def stick_breaking_core(q, k, v, mask):
    z = jnp.einsum('bqhd,bkhd->bhqk', q.astype(jnp.float32), k.astype(jnp.float32)) * HEAD_DIM ** -0.5
    log_1mb = jnp.where(mask, jax.nn.log_sigmoid(-z), 0.0)
    after = lax.cumsum(log_1mb, axis=3, reverse=True) - log_1mb
    w = jnp.where(mask, jnp.exp(jax.nn.log_sigmoid(z) + after), 0.0)
    o = jnp.einsum('bhqk,bkhd->bqhd', w, v.astype(jnp.float32))
    return o.reshape(o.shape[0], o.shape[1], -1).astype(q.dtype)


def project_even(x, pos, norm_g, w_in, q_g, k_g):
    B, S = x.shape[:2]
    h = rms_norm(x, norm_g)
    dq, dk, dv, sq, sk, sv = jnp.split(h @ w_in, SPLIT0, axis=-1)
    dq = partial_rope(rms_norm(dq.reshape(B, S, H_DIFF, 2, HEAD_DIM), q_g), pos)
    dk = partial_rope(rms_norm(dk.reshape(B, S, H_DIFF, 2, HEAD_DIM), k_g), pos)
    diff_rows = jnp.stack([dk.reshape(B, S, H_DIFF, 2 * HEAD_DIM), dv.reshape(B, S, H_DIFF, 2 * HEAD_DIM)], axis=2)
    sb_rows = jnp.stack([sk.reshape(B, S, H_SB, HEAD_DIM), sv.reshape(B, S, H_SB, HEAD_DIM)], axis=2)
    return dq, sq.reshape(B, S, H_SB, HEAD_DIM), diff_rows, sb_rows


def even_core(dq, sq, qpos, diff_rows, sb_rows, kpos, lam, subln_g):
    B, Sk = diff_rows.shape[:2]
    dk = diff_rows[:, :, 0].reshape(B, Sk, H_DIFF, 2, HEAD_DIM)
    od = diff_attn_core(dq, dk, diff_rows[:, :, 1], kpos[None, :] <= qpos[:, None], lam, subln_g)
    osb = stick_breaking_core(sq, sb_rows[:, :, 0], sb_rows[:, :, 1], kpos[None, :] < qpos[:, None])
    return jnp.concatenate([od, osb], axis=-1)


def even_mixer_prompt(x, pos, norm_g, w_in, q_g, k_g, lam_p, subln_g, w_out):
    dq, sq, diff_rows, sb_rows = project_even(x, pos, norm_g, w_in, q_g, k_g)
    lam = diff_lambda(lam_p)
    blk = lambda dq_b, sq_b, qp: even_core(dq_b, sq_b, qp, diff_rows, sb_rows, pos, lam, subln_g)
    o = sweep_query_blocks(blk, (dq, sq), pos, QBLK)
    return x + o @ w_out, diff_rows, sb_rows


def even_mixer_sample(x, pos, cache_diff, cache_sb, page_table, norm_g, w_in, q_g, k_g, lam_p, subln_g, w_out):
    dq, sq, diff_rows, sb_rows = project_even(x, pos, norm_g, w_in, q_g, k_g)
    lam = diff_lambda(lam_p)
    all_diff = jnp.concatenate([gather_pages(cache_diff, page_table), diff_rows], axis=1)
    all_sb = jnp.concatenate([gather_pages(cache_sb, page_table), sb_rows], axis=1)
    kpos = jnp.arange(all_diff.shape[1], dtype=jnp.int32)
    o = even_core(dq, sq, pos, all_diff, all_sb, kpos, lam, subln_g)
    return x + o @ w_out, diff_rows, sb_rows


def compress_rows(rows, cmp_pos, w1, w2):
    B, L = rows.shape[:2]
    n_cmp = (L - L_CMP) // CMP_STRIDE + 1
    idx = (np.arange(n_cmp)[:, None] * CMP_STRIDE + np.arange(L_CMP)[None, :]).astype(np.int32)
    blk = rows[:, idx] + cmp_pos[None, None, :, :, None, :]
    flat = jnp.moveaxis(blk, 2, 4).reshape(B, n_cmp, 2, rows.shape[3], L_CMP * HEAD_DIM)
    hid = jax.nn.gelu(jnp.einsum('bnigf,ifh->bnigh', flat, w1))
    out = jnp.einsum('bnigh,ihd->bnigd', hid, w2)
    return out[:, :, 0], out[:, :, 1], idx[:, -1]


def overlap_matrix(n_cmp, n_sel):
    start = np.arange(n_cmp)[:, None] * CMP_STRIDE
    j = np.arange(n_sel)[None, :]
    return ((start < (j + 1) * L_SEL) & (start + L_CMP > j * L_SEL)).astype(np.float32)


def fetch_from_rows(rows, tok_pos):
    B, L, _, G, _ = rows.shape
    b = jnp.arange(B)[:, None, None, None, None]
    g = jnp.arange(G)[None, :, None, None, None]
    p = jnp.clip(tok_pos, 0, L - 1)
    return rows[b, p, 0, g], rows[b, p, 1, g]


def fetch_from_pages(pool, page_table, new_rows, tok_pos):
    B, n_pages = page_table.shape
    page = pool.shape[1]
    past_len = n_pages * page
    G = pool.shape[3]
    b = jnp.arange(B)[:, None, None, None, None]
    g = jnp.arange(G)[None, :, None, None, None]
    pp = jnp.clip(tok_pos, 0, past_len - 1)
    phys = page_table[b, pp // page]
    off = pp % page
    pn = jnp.clip(tok_pos - past_len, 0, new_rows.shape[1] - 1)
    is_past = (tok_pos < past_len)[..., None]
    k = jnp.where(is_past, pool[phys, off, 0, g], new_rows[b, pn, 0, g])
    v = jnp.where(is_past, pool[phys, off, 1, g], new_rows[b, pn, 1, g])
    return k, v


def nsa_core(q, gate, qpos, kc, vc, c_end, n_sel, fetch_sel, kw, vw, wpos):
    B, Sq = q.shape[:2]
    scale = HEAD_DIM ** -0.5
    qg = q.astype(jnp.float32).reshape(B, Sq, H_NSA_KV, Q_PER_KV, HEAD_DIM)
    s_c = jnp.einsum('bqgrd,bngd->bgrqn', qg, kc.astype(jnp.float32)) * scale
    p_c = masked_softmax(s_c, jnp.asarray(c_end)[None, :] <= qpos[:, None])
    o_c = jnp.einsum('bgrqn,bngd->bqgrd', p_c, vc.astype(jnp.float32))
    imp = jnp.einsum('bgrqn,nj->bgqj', p_c, jnp.asarray(overlap_matrix(kc.shape[1], n_sel)))
    blk = jnp.arange(n_sel, dtype=jnp.int32)
    cur = qpos // L_SEL
    valid = blk[None, :] * L_SEL <= qpos[:, None]
    forced = (blk[None, :] == 0) | (blk[None, :] == cur[:, None]) | (blk[None, :] == cur[:, None] - 1)
    score = jnp.where(valid, imp + jnp.where(forced, FORCE_BONUS, 0.0), -jnp.inf)
    top_val, top_idx = lax.top_k(score, min(N_SEL, n_sel))
    tok_pos = top_idx[..., None] * L_SEL + jnp.arange(L_SEL, dtype=jnp.int32)
    tok_ok = jnp.isfinite(top_val)[..., None] & (tok_pos <= qpos[None, None, :, None, None])
    ks, vs = fetch_sel(tok_pos)
    n_tok = tok_pos.shape[3] * L_SEL
    ks = ks.reshape(B, H_NSA_KV, Sq, n_tok, HEAD_DIM).astype(jnp.float32)
    vs = vs.reshape(B, H_NSA_KV, Sq, n_tok, HEAD_DIM).astype(jnp.float32)
    s_s = jnp.einsum('bqgrd,bgqkd->bgrqk', qg, ks) * scale
    p_s = masked_softmax(s_s, tok_ok.reshape(B, H_NSA_KV, 1, Sq, n_tok))
    o_s = jnp.einsum('bgrqk,bgqkd->bqgrd', p_s, vs)
    d = qpos[:, None] - wpos[None, :]
    wmask = (d >= 0) & (d < WINDOW) & (wpos[None, :] >= 0)
    s_w = jnp.einsum('bqgrd,bkgd->bgrqk', qg, kw.astype(jnp.float32)) * scale
    p_w = masked_softmax(s_w, wmask)
    o_w = jnp.einsum('bgrqk,bkgd->bqgrd', p_w, vw.astype(jnp.float32))
    g = gate.reshape(B, Sq, H_NSA_KV, Q_PER_KV, 3)
    o = o_c * g[..., 0:1] + o_s * g[..., 1:2] + o_w * g[..., 2:3]
    return o.reshape(B, Sq, H_NSA * HEAD_DIM).astype(q.dtype)


def project_odd(x, pos, norm_g, w_in, q_g, k_g):
    B, S = x.shape[:2]
    h = rms_norm(x, norm_g)
    q, kc, vc, ks, vs, kw, vw, gl = jnp.split(h @ w_in, SPLIT1, axis=-1)
    q = partial_rope(rms_norm(q.reshape(B, S, H_NSA, HEAD_DIM), q_g), pos)

    def rows(k, v, i):
        k = partial_rope(rms_norm(k.reshape(B, S, H_NSA_KV, HEAD_DIM), k_g[i]), pos)
        return jnp.stack([k, v.reshape(B, S, H_NSA_KV, HEAD_DIM)], axis=2)

    gate = jax.nn.sigmoid(gl.astype(jnp.float32)).reshape(B, S, H_NSA, 3)
    return q, gate, rows(kc, vc, 0), rows(ks, vs, 1), rows(kw, vw, 2)


def odd_mixer_prompt(x, pos, norm_g, w_in, q_g, k_g, cmp_pos, cmp_w1, cmp_w2, w_out):
    S = x.shape[1]
    q, gate, rows_c, rows_s, rows_w = project_odd(x, pos, norm_g, w_in, q_g, k_g)
    kc, vc, c_end = compress_rows(rows_c, cmp_pos, cmp_w1, cmp_w2)
    n_sel = -(-S // L_SEL)
    rows_w_pad = jnp.pad(rows_w, ((0, 0), (WINDOW, 0), (0, 0), (0, 0), (0, 0)))

    def blk(q_b, g_b, qp):
        nq = q_b.shape[1]
        win = lax.dynamic_slice_in_dim(rows_w_pad, qp[0], WINDOW + nq, axis=1)
        wpos = qp[0] - WINDOW + jnp.arange(WINDOW + nq, dtype=jnp.int32)
        return nsa_core(q_b, g_b, qp, kc, vc, c_end, n_sel, lambda tp: fetch_from_rows(rows_s, tp),
                        win[:, :, 0], win[:, :, 1], wpos)

    o = sweep_query_blocks(blk, (q, gate), pos, NSA_QBLK)
    return x + o @ w_out, rows_c, rows_s, rows_w[:, S - min(WINDOW, S):]


def odd_mixer_sample(x, pos, cache_cmp, cache_sel, win_state, page_table,
                     norm_g, w_in, q_g, k_g, cmp_pos, cmp_w1, cmp_w2, w_out):
    q, gate, rows_c, rows_s, rows_w = project_odd(x, pos, norm_g, w_in, q_g, k_g)
    past_len = page_table.shape[1] * cache_cmp.shape[1]
    all_c = jnp.concatenate([gather_pages(cache_cmp, page_table), rows_c], axis=1)
    kc, vc, c_end = compress_rows(all_c, cmp_pos, cmp_w1, cmp_w2)
    n_sel = -(-all_c.shape[1] // L_SEL)
    win = jnp.concatenate([win_state, rows_w], axis=1)
    w_buf = win_state.shape[1]
    wpos = past_len - w_buf + jnp.arange(win.shape[1], dtype=jnp.int32)
    o = nsa_core(q, gate, pos, kc, vc, c_end, n_sel, lambda tp: fetch_from_pages(cache_sel, page_table, rows_s, tp),
                 win[:, :, 0], win[:, :, 1], wpos)
    return x + o @ w_out, rows_c, rows_s, win[:, win.shape[1] - w_buf:]


def dense_swiglu(x, norm_g, w_gate_up, w_down):
    h = rms_norm(x, norm_g)
    g, u = jnp.split(h @ w_gate_up, 2, axis=-1)
    return x + (jax.nn.silu(g) * u) @ w_down


def moe_swiglu(x, norm_g, w_router, w_gate_up, w_down):
    h = rms_norm(x, norm_g)
    logits = jnp.einsum('bsd,de->bse', h.astype(jnp.float32), w_router.astype(jnp.float32))
    top_val, top_idx = lax.top_k(logits, TOP_K)
    w = jax.nn.softmax(top_val, axis=-1)
    combine = jnp.sum(jax.nn.one_hot(top_idx, N_EXPERTS, dtype=jnp.float32) * w[..., None], axis=-2)
    y = jnp.zeros(x.shape, jnp.float32)
    for e in range(N_EXPERTS):
        g, u = jnp.split(h @ w_gate_up[e], 2, axis=-1)
        y = y + combine[..., e:e + 1] * ((jax.nn.silu(g) * u) @ w_down[e]).astype(jnp.float32)
    return x + y.astype(x.dtype)


def setup_inputs(seed: int = 0) -> dict:
    key = jax.random.key(seed)
    k = jax.random.split(key, 32)
    nrm = lambda i, shape, scale: jax.random.normal(k[i], shape, jnp.float32) * scale
    gain = lambda i, shape: 1.0 + nrm(i, shape, 0.02)
    n_pages = PAST_LEN // PAGE_SIZE
    n_used = DEC_BATCH * n_pages
    n_pool = n_used + n_used // POOL_EXTRA
    page_table = jax.random.permutation(k[7], n_pool)[:n_used].reshape(DEC_BATCH, n_pages).astype(jnp.int32)
    w_buf = min(WINDOW, PAST_LEN)
    return {
        'x_prompt': nrm(0, (BATCH, SEQ, D_MODEL), 1.0),
        'x_sample': nrm(1, (DEC_BATCH, DEC_SEQ, D_MODEL), 1.0),
        'cache_l0_diff': nrm(2, (n_pool, PAGE_SIZE, 2, H_DIFF, 2 * HEAD_DIM), 1.0),
        'cache_l0_sb': nrm(3, (n_pool, PAGE_SIZE, 2, H_SB, HEAD_DIM), 1.0),
        'cache_l1_cmp': nrm(4, (n_pool, PAGE_SIZE, 2, H_NSA_KV, HEAD_DIM), 1.0),
        'cache_l1_sel': nrm(5, (n_pool, PAGE_SIZE, 2, H_NSA_KV, HEAD_DIM), 1.0),
        'state_l1_win': nrm(6, (DEC_BATCH, w_buf, 2, H_NSA_KV, HEAD_DIM), 1.0),
        'page_table': page_table,
        'l0_norm_mix': gain(8, (D_MODEL,)),
        'l0_w_in': nrm(9, (D_MODEL, W_IN0), D_MODEL ** -0.5),
        'l0_diff_q_norm': gain(10, (HEAD_DIM,)),
        'l0_diff_k_norm': gain(11, (HEAD_DIM,)),
        'l0_diff_lambda': nrm(12, (4, HEAD_DIM), 0.1),
        'l0_diff_subln': gain(13, (H_DIFF, 2 * HEAD_DIM)),
        'l0_w_out': nrm(14, (W_OUT0, D_MODEL), W_OUT0 ** -0.5),
        'l0_norm_ffn': gain(15, (D_MODEL,)),
        'l0_ffn_w_gate_up': nrm(16, (D_MODEL, 2 * D_FF), D_MODEL ** -0.5),
        'l0_ffn_w_down': nrm(17, (D_FF, D_MODEL), D_FF ** -0.5),
        'l1_norm_mix': gain(18, (D_MODEL,)),
        'l1_w_in': nrm(19, (D_MODEL, W_IN1), D_MODEL ** -0.5),
        'l1_q_norm': gain(20, (HEAD_DIM,)),
        'l1_k_norm': gain(21, (3, HEAD_DIM)),
        'l1_cmp_pos': nrm(22, (L_CMP, 2, HEAD_DIM), 0.1),
        'l1_cmp_w1': nrm(23, (2, L_CMP * HEAD_DIM, CMP_HID), (L_CMP * HEAD_DIM) ** -0.5),
        'l1_cmp_w2': nrm(24, (2, CMP_HID, HEAD_DIM), CMP_HID ** -0.5),
        'l1_w_out': nrm(25, (NSA_Q_W, D_MODEL), NSA_Q_W ** -0.5),
        'l1_norm_ffn': gain(26, (D_MODEL,)),
        'l1_router': nrm(27, (D_MODEL, N_EXPERTS), D_MODEL ** -0.5),
        'l1_moe_w_gate_up': nrm(28, (N_EXPERTS, D_MODEL, 2 * D_FF_EXPERT), D_MODEL ** -0.5),
        'l1_moe_w_down': nrm(29, (N_EXPERTS, D_FF_EXPERT, D_MODEL), D_FF_EXPERT ** -0.5),
    }


def reference(x_prompt, x_sample, cache_l0_diff, cache_l0_sb, cache_l1_cmp, cache_l1_sel, state_l1_win, page_table,
              l0_norm_mix, l0_w_in, l0_diff_q_norm, l0_diff_k_norm, l0_diff_lambda, l0_diff_subln, l0_w_out,
              l0_norm_ffn, l0_ffn_w_gate_up, l0_ffn_w_down,
              l1_norm_mix, l1_w_in, l1_q_norm, l1_k_norm, l1_cmp_pos, l1_cmp_w1, l1_cmp_w2, l1_w_out,
              l1_norm_ffn, l1_router, l1_moe_w_gate_up, l1_moe_w_down):
    past_len = page_table.shape[1] * cache_l0_diff.shape[1]
    pos_p = jnp.arange(x_prompt.shape[1], dtype=jnp.int32)
    pos_s = past_len + jnp.arange(x_sample.shape[1], dtype=jnp.int32)
    xp, xs = x_prompt, x_sample
    for layer in range(DEPTH):
        if layer % 2 == 0:
            wts0 = (l0_norm_mix, l0_w_in, l0_diff_q_norm, l0_diff_k_norm, l0_diff_lambda, l0_diff_subln, l0_w_out)
            xp, diff_p, sb_p = even_mixer_prompt(xp, pos_p, *wts0)
            xs, diff_s, sb_s = even_mixer_sample(xs, pos_s, cache_l0_diff, cache_l0_sb, page_table, *wts0)
            xp = dense_swiglu(xp, l0_norm_ffn, l0_ffn_w_gate_up, l0_ffn_w_down)
            xs = dense_swiglu(xs, l0_norm_ffn, l0_ffn_w_gate_up, l0_ffn_w_down)
        else:
            wts1 = (l1_norm_mix, l1_w_in, l1_q_norm, l1_k_norm, l1_cmp_pos, l1_cmp_w1, l1_cmp_w2, l1_w_out)
            xp, cmp_p, sel_p, win_p = odd_mixer_prompt(xp, pos_p, *wts1)
            xs, cmp_s, sel_s, win_s = odd_mixer_sample(xs, pos_s, cache_l1_cmp, cache_l1_sel, state_l1_win, page_table, *wts1)
            xp = moe_swiglu(xp, l1_norm_ffn, l1_router, l1_moe_w_gate_up, l1_moe_w_down)
            xs = moe_swiglu(xs, l1_norm_ffn, l1_router, l1_moe_w_gate_up, l1_moe_w_down)
    return (xp, xs, diff_p, sb_p, cmp_p, sel_p, win_p, diff_s, sb_s, cmp_s, sel_s, win_s)
```

```python
import functools

import numpy as np
import jax
import jax.numpy as jnp
from jax import lax
from jax.experimental import pallas as pl
from jax.experimental.pallas import tpu as pltpu

F32 = jnp.float32
BF16 = jnp.bfloat16

HEAD_DIM = 64
ROT_DIM = HEAD_DIM // 4
ROPE_THETA = 500000.0
EPS = 1e-6
DIFF_LAMBDA_INIT = 0.2
L_CMP = 32
CMP_STRIDE = 16
L_SEL = 64
N_SEL = 16
WINDOW = 512
FORCE_BONUS = 1.0e4
N_EXPERTS = 8
LANES = 128
VMEM_LIMIT = 56 * 1024 * 1024
NEG_INF = float("-inf")


def _cparams(sem):
    return pltpu.CompilerParams(dimension_semantics=sem, vmem_limit_bytes=VMEM_LIMIT)


def _dot(a, b):
    return jnp.dot(a, b, preferred_element_type=F32)


def _dot_nt(a, b):
    return lax.dot_general(a, b, (((1,), (1,)), ((), ())), preferred_element_type=F32)


def _split2_dot(x, m):
    hi = x.astype(BF16)
    lo = (x - hi.astype(F32)).astype(BF16)
    return _dot(hi, m) + _dot(lo, m)


def _split3_dot(x, m):
    hi = x.astype(BF16)
    r1 = x - hi.astype(F32)
    mid = r1.astype(BF16)
    lo = (r1 - mid.astype(F32)).astype(BF16)
    return _dot(hi, m) + _dot(mid, m) + _dot(lo, m)


def _iota(shape, dim):
    return lax.broadcasted_iota(jnp.int32, shape, dim)


def _group_ones(n, group):
    r = _iota((n, n), 0) // group
    c = _iota((n, n), 1) // group
    return jnp.where(r == c, 1.0, 0.0).astype(BF16)


def _rms(x, g):
    return x * lax.rsqrt(jnp.mean(x * x, axis=-1, keepdims=True) + EPS) * g


def _head_norm_rope(xc, g, ones64, c, sa, sb):
    ms = _split2_dot(xc * xc, ones64) * (1.0 / HEAD_DIM)
    xn = xc * lax.rsqrt(ms + EPS) * g
    return xn * c + pltpu.roll(xn, LANES - ROT_DIM // 2, 1) * sa + pltpu.roll(xn, ROT_DIM // 2, 1) * sb


def _rope_tables(pos):
    half = ROT_DIM // 2
    inv = ROPE_THETA ** (-jnp.arange(half, dtype=F32) * 2.0 / ROT_DIM)
    ang = pos.astype(F32)[:, None] * inv[None, :]
    cos, sin = jnp.cos(ang), jnp.sin(ang)
    n = pos.shape[0]
    one = jnp.ones((n, HEAD_DIM - ROT_DIM), F32)
    zero_h = jnp.zeros((n, half), F32)
    zero_r = jnp.zeros((n, HEAD_DIM - ROT_DIM), F32)
    c = jnp.concatenate([cos, cos, one], axis=1)
    sa = jnp.concatenate([-sin, zero_h, zero_r], axis=1)
    sb = jnp.concatenate([zero_h, sin, zero_r], axis=1)
    tile = lambda t: jnp.concatenate([t, t], axis=1)
    return tile(c), tile(sa), tile(sb)


def _proj0_kernel(x_ref, ng_ref, w_ref, qg_ref, kg_ref, c_ref, sa_ref, sb_ref,
                  dq_ref, drows_ref, sq_ref, srows_ref, dkv_ref, skv_ref):
    h = _rms(x_ref[...], ng_ref[...]).astype(BF16)
    ones64 = _group_ones(LANES, HEAD_DIM)
    c, sa, sb = c_ref[...], sa_ref[...], sb_ref[...]
    scale = HEAD_DIM ** -0.5
    y = _dot(h, w_ref[:, 0:512])
    for j in range(4):
        sl = slice(j * LANES, (j + 1) * LANES)
        dq_ref[:, sl] = (_head_norm_rope(y[:, sl], qg_ref[...], ones64, c, sa, sb) * scale).astype(BF16)
    y = _dot(h, w_ref[:, 512:1024])
    for j in range(4):
        sl = slice(j * LANES, (j + 1) * LANES)
        k = _head_norm_rope(y[:, sl], kg_ref[...], ones64, c, sa, sb)
        drows_ref[:, sl] = k
        dkv_ref[:, sl] = k.astype(BF16)
    y = _dot(h, w_ref[:, 1024:1536])
    drows_ref[:, 512:1024] = y
    dkv_ref[:, 512:1024] = y.astype(BF16)
    y = _dot(h, w_ref[:, 1536:2048])
    sq_ref[...] = (y * scale).astype(BF16)
    y = _dot(h, w_ref[:, 2048:3072])
    srows_ref[...] = y
    skv_ref[...] = y.astype(BF16)


def _proj0(x, norm_g, w_in, q_g, k_g, tabs, tm):
    t, d = x.shape
    tab_blocks = tabs[0].shape[0] // tm
    row = lambda i: (i, 0)
    fixed = lambda i: (0, 0)
    tab = lambda i: (i % tab_blocks, 0)
    g2 = lambda g: jnp.tile(g.reshape(1, HEAD_DIM), (1, LANES // HEAD_DIM))
    outs = [jax.ShapeDtypeStruct((t, 512), BF16), jax.ShapeDtypeStruct((t, 1024), F32),
            jax.ShapeDtypeStruct((t, 512), BF16), jax.ShapeDtypeStruct((t, 1024), F32),
            jax.ShapeDtypeStruct((t, 1024), BF16), jax.ShapeDtypeStruct((t, 1024), BF16)]
    return pl.pallas_call(
        _proj0_kernel,
        grid=(t // tm,),
        in_specs=[pl.BlockSpec((tm, d), row), pl.BlockSpec((1, d), fixed),
                  pl.BlockSpec(w_in.shape, fixed),
                  pl.BlockSpec((1, LANES), fixed), pl.BlockSpec((1, LANES), fixed),
                  pl.BlockSpec((tm, LANES), tab), pl.BlockSpec((tm, LANES), tab), pl.BlockSpec((tm, LANES), tab)],
        out_specs=[pl.BlockSpec((tm, o.shape[1]), row) for o in outs],
        out_shape=outs,
        compiler_params=_cparams(("arbitrary",)),
        name="proj0",
    )(x, norm_g.reshape(1, d), w_in.astype(BF16), g2(q_g), g2(k_g), *tabs)


def _lane_half_masks(shape):
    lane = _iota(shape, 1)
    return lane < HEAD_DIM, lane >= HEAD_DIM


def _diff_lambda(lp):
    a = jnp.sum(lp[0:1] * lp[1:2], axis=-1, keepdims=True)
    b = jnp.sum(lp[2:3] * lp[3:4], axis=-1, keepdims=True)
    return jnp.exp(a) - jnp.exp(b) + DIFF_LAMBDA_INIT


def _attn0_kernel(lam_ref, subg_ref, dq_ref, sq_ref, dkv_ref, skv_ref, o_ref, *, tq):
    qi = pl.program_id(1)
    q0 = qi * tq
    lam = _diff_lambda(lam_ref[...])
    lo_mask, hi_mask = _lane_half_masks((tq, LANES))
    row = _iota((tq, tq), 0)
    col = _iota((tq, tq), 1)
    zero_bf = jnp.zeros((tq, LANES), BF16)

    for h in range(4):
        sl = slice(h * LANES, (h + 1) * LANES)
        vsl = slice(512 + h * LANES, 512 + (h + 1) * LANES)
        q = dq_ref[0, :, sl]
        qm = (jnp.where(lo_mask, q, zero_bf), jnp.where(hi_mask, q, zero_bf))

        def step(kb, carry, diag):
            k = dkv_ref[0, pl.ds(pl.multiple_of(kb * tq, tq), tq), sl]
            v = dkv_ref[0, pl.ds(pl.multiple_of(kb * tq, tq), tq), vsl]
            out = []
            for m in range(2):
                m_run, l_run, acc = carry[m]
                s = _dot_nt(qm[m], k)
                if diag:
                    s = jnp.where(col <= row, s, NEG_INF)
                m_new = jnp.maximum(m_run, jnp.max(s, axis=-1, keepdims=True))
                p = jnp.exp(s - m_new)
                alpha = jnp.exp(m_run - m_new)
                l_new = alpha * l_run + jnp.sum(p, axis=-1, keepdims=True)
                acc = alpha * acc + _dot(p.astype(BF16), v)
                out.append((m_new, l_new, acc))
            return tuple(out)

        init = tuple((jnp.full((tq, 1), NEG_INF, F32), jnp.zeros((tq, 1), F32), jnp.zeros((tq, LANES), F32))
                     for _ in range(2))
        carry = step(qi, init, True)
        carry = lax.fori_loop(0, qi, lambda kb, c: step(kb, c, False), carry)
        (_, l0, a0), (_, l1, a1) = carry
        o = a0 / l0 - lam * (a1 / l1)
        o_ref[0, :, sl] = _rms(o, subg_ref[:, sl]) * (1.0 - DIFF_LAMBDA_INIT)

    upper = jnp.where(row > col, 1.0, 0.0).astype(BF16)
    for p2 in range(4):
        sl = slice(p2 * LANES, (p2 + 1) * LANES)
        vsl = slice(512 + p2 * LANES, 512 + (p2 + 1) * LANES)
        q = sq_ref[0, :, sl]
        qm = (jnp.where(lo_mask, q, zero_bf), jnp.where(hi_mask, q, zero_bf))

        def sb_step(kb, carry, diag):
            k = skv_ref[0, pl.ds(pl.multiple_of(kb * tq, tq), tq), sl]
            v = skv_ref[0, pl.ds(pl.multiple_of(kb * tq, tq), tq), vsl]
            out = []
            for m in range(2):
                tail, acc = carry[m]
                z = _dot_nt(qm[m], k)
                t = jnp.log1p(jnp.exp(-jnp.abs(z)))
                lsp = -(jnp.maximum(-z, 0.0) + t)
                l1m = -(jnp.maximum(z, 0.0) + t)
                if diag:
                    l1m = jnp.where(col < row, l1m, 0.0)
                after = _split2_dot(l1m, upper) + tail
                w = jnp.exp(lsp + after)
                if diag:
                    w = jnp.where(col < row, w, 0.0)
                acc = acc + _dot(w.astype(BF16), v)
                tail = tail + jnp.sum(l1m, axis=-1, keepdims=True)
                out.append((tail, acc))
            return tuple(out)

        init = tuple((jnp.zeros((tq, 1), F32), jnp.zeros((tq, LANES), F32)) for _ in range(2))
        carry = sb_step(qi, init, True)
        carry = lax.fori_loop(0, qi, lambda i, c: sb_step(qi - 1 - i, c, False), carry)
        (_, a0), (_, a1) = carry
        o_ref[0, :, 512 + p2 * LANES:512 + (p2 + 1) * LANES] = jnp.where(lo_mask, a0, a1)


def _attn0_prompt(lam_p, subln_g, dq, sq, dkv, skv, b, s, tq):
    r3 = lambda a: a.reshape(b, s, a.shape[-1])
    qspec = pl.BlockSpec((1, tq, 512), lambda bi, qi: (bi, qi, 0))
    kvspec = pl.BlockSpec((1, s, 1024), lambda bi, qi: (bi, 0, 0))
    fixed = lambda bi, qi: (0, 0)
    out = pl.pallas_call(
        functools.partial(_attn0_kernel, tq=tq),
        grid=(b, s // tq),
        in_specs=[pl.BlockSpec((4, HEAD_DIM), fixed), pl.BlockSpec((1, 512), fixed), qspec, qspec, kvspec, kvspec],
        out_specs=pl.BlockSpec((1, tq, 1024), lambda bi, qi: (bi, qi, 0)),
        out_shape=jax.ShapeDtypeStruct((b, s, 1024), F32),
        compiler_params=_cparams(("arbitrary", "arbitrary")),
        name="attn0_prompt",
    )(lam_p, subln_g.reshape(1, 512), r3(dq), r3(sq), r3(dkv), r3(skv))
    return out.reshape(b * s, 1024)


def _outproj_ffn_kernel(x_ref, o_ref, wo_ref, ng_ref, wg_ref, wu_ref, wd_ref, y_ref, x1_ref, h_ref, acc_ref):
    j = pl.program_id(1)

    @pl.when(j == 0)
    def _():
        x1 = x_ref[...] + _dot(o_ref[...].astype(BF16), wo_ref[...])
        x1_ref[...] = x1
        h_ref[...] = _rms(x1, ng_ref[...]).astype(BF16)
        acc_ref[...] = jnp.zeros_like(acc_ref)

    h = h_ref[...]
    g = _dot(h, wg_ref[...])
    u = _dot(h, wu_ref[...])
    a = (g * jax.nn.sigmoid(g) * u).astype(BF16)
    acc_ref[...] += _dot(a, wd_ref[...])

    @pl.when(j == pl.num_programs(1) - 1)
    def _():
        y_ref[...] = x1_ref[...] + acc_ref[...]


def _outproj_ffn(x, o, w_out, norm_g, w_gate_up, w_down, tm, tf):
    t, d = x.shape
    ff = w_down.shape[0]
    nj = ff // tf
    row = lambda i, j: (i, 0)
    fixed = lambda i, j: (0, 0)
    wgu = w_gate_up.astype(BF16)
    return pl.pallas_call(
        _outproj_ffn_kernel,
        grid=(t // tm, nj),
        in_specs=[pl.BlockSpec((tm, d), row), pl.BlockSpec((tm, o.shape[1]), row),
                  pl.BlockSpec(w_out.shape, fixed), pl.BlockSpec((1, d), fixed),
                  pl.BlockSpec((d, tf), lambda i, j: (0, j)), pl.BlockSpec((d, tf), lambda i, j: (0, nj + j)),
                  pl.BlockSpec((tf, d), lambda i, j: (j, 0))],
        out_specs=pl.BlockSpec((tm, d), row),
        out_shape=jax.ShapeDtypeStruct((t, d), F32),
        scratch_shapes=[pltpu.VMEM((tm, d), F32), pltpu.VMEM((tm, d), BF16), pltpu.VMEM((tm, d), F32)],
        compiler_params=_cparams(("arbitrary", "arbitrary")),
        name="outproj_ffn",
    )(x, o, w_out.astype(BF16), norm_g.reshape(1, d), wgu, wgu, w_down.astype(BF16))


def _rows_select(x, group, nrows):
    w = x.shape[-1]
    keep = _iota((nrows, w), 1) // group == _iota((nrows, w), 0)
    return jnp.where(keep, jnp.broadcast_to(x.astype(F32), (nrows, w)), 0.0).astype(x.dtype)


def _attn0_decode_kernel(pt_ref, lam_ref, subg_ref, dq_ref, sq_ref, dnew_ref, *rest, pages_per_step):
    pp = pages_per_step
    dpages = rest[:pp]
    spages = rest[pp:2 * pp]
    o_ref = rest[2 * pp]
    md_ref, ld_ref, accd_ref, tail_ref, accs_ref = rest[2 * pp + 1:]
    j = pl.program_id(1)
    qd = _rows_select(dq_ref[0], HEAD_DIM, 8)
    qs = _rows_select(sq_ref[0], HEAD_DIM, 8)

    @pl.when(j == 0)
    def _():
        new = dnew_ref[0]
        s_new = jnp.sum(qd.astype(F32) * new[:, 0:512].astype(F32), axis=-1, keepdims=True)
        md_ref[...] = s_new
        ld_ref[...] = jnp.ones_like(ld_ref)
        accd_ref[...] = jnp.broadcast_to(new[:, 512:1024].astype(F32), accd_ref.shape)
        tail_ref[...] = jnp.zeros_like(tail_ref)
        accs_ref[...] = jnp.zeros_like(accs_ref)

    page = dpages[0].shape[1]
    upper = jnp.where(_iota((page, page), 0) > _iota((page, page), 1), 1.0, 0.0).astype(BF16)
    for i in range(pp):
        kv = dpages[i][0]
        s = _dot_nt(qd, kv[:, 0:512].astype(BF16))
        m_run = md_ref[...]
        m_new = jnp.maximum(m_run, jnp.max(s, axis=-1, keepdims=True))
        p = jnp.exp(s - m_new)
        alpha = jnp.exp(m_run - m_new)
        ld_ref[...] = alpha * ld_ref[...] + jnp.sum(p, axis=-1, keepdims=True)
        accd_ref[...] = alpha * accd_ref[...] + _dot(p.astype(BF16), kv[:, 512:1024].astype(BF16))
        md_ref[...] = m_new

        kv = spages[i][0]
        z = _dot_nt(qs, kv[:, 0:512].astype(BF16))
        t = jnp.log1p(jnp.exp(-jnp.abs(z)))
        lsp = -(jnp.maximum(-z, 0.0) + t)
        l1m = -(jnp.maximum(z, 0.0) + t)
        tail = tail_ref[...]
        w = jnp.exp(lsp + _split3_dot(l1m, upper) + tail)
        accs_ref[...] += _dot(w.astype(BF16), kv[:, 512:1024].astype(BF16))
        tail_ref[...] = tail + jnp.sum(l1m, axis=-1, keepdims=True)

    @pl.when(j == pl.num_programs(1) - 1)
    def _():
        lam = _diff_lambda(lam_ref[...])
        on = accd_ref[...] / ld_ref[...]
        r = _iota(on.shape, 0)
        head = _iota(on.shape, 1) // LANES
        o0 = jnp.sum(jnp.where(r == 2 * head, on, 0.0), axis=0, keepdims=True)
        o1 = jnp.sum(jnp.where(r == 2 * head + 1, on, 0.0), axis=0, keepdims=True)
        o = o0 - lam * o1
        ms = _split2_dot(o * o, _group_ones(512, LANES)) * (1.0 / LANES)
        o_ref[0, :, 0:512] = o * lax.rsqrt(ms + EPS) * subg_ref[...] * (1.0 - DIFF_LAMBDA_INIT)
        acs = accs_ref[...]
        o_ref[0, :, 512:1024] = jnp.sum(jnp.where(r == _iota(on.shape, 1) // HEAD_DIM, acs, 0.0), axis=0,
                                        keepdims=True)


def _attn0_decode(page_table, lam_p, subln_g, dq, sq, dkv, cache_diff, cache_sb, pages_per_step):
    nb, n_pages = page_table.shape
    n_pool, page = cache_diff.shape[:2]
    pp = pages_per_step
    cd = cache_diff.reshape(n_pool, page, 1024)
    cs = cache_sb.reshape(n_pool, page, 1024)
    fixed = lambda b, j, pt: (0, 0)
    per_b = lambda b, j, pt: (b, 0, 0)

    def page_spec(i):
        return pl.BlockSpec((1, page, 1024), lambda b, j, pt: (pt[b, n_pages - 1 - (j * pp + i)], 0, 0))

    grid_spec = pltpu.PrefetchScalarGridSpec(
        num_scalar_prefetch=1,
        grid=(nb, n_pages // pp),
        in_specs=[pl.BlockSpec((4, HEAD_DIM), fixed), pl.BlockSpec((1, 512), fixed),
                  pl.BlockSpec((1, 1, 512), per_b), pl.BlockSpec((1, 1, 512), per_b),
                  pl.BlockSpec((1, 1, 1024), per_b)]
                 + [page_spec(i) for i in range(pp)] + [page_spec(i) for i in range(pp)],
        out_specs=pl.BlockSpec((1, 1, 1024), per_b),
        scratch_shapes=[pltpu.VMEM((8, 1), F32), pltpu.VMEM((8, 1), F32), pltpu.VMEM((8, 512), F32),
                        pltpu.VMEM((8, 1), F32), pltpu.VMEM((8, 512), F32)],
    )
    out = pl.pallas_call(
        functools.partial(_attn0_decode_kernel, pages_per_step=pp),
        grid_spec=grid_spec,
        out_shape=jax.ShapeDtypeStruct((nb, 1, 1024), F32),
        compiler_params=_cparams(("arbitrary", "arbitrary")),
        name="attn0_decode",
    )(page_table, lam_p, subln_g.reshape(1, 512), dq.reshape(nb, 1, 512), sq.reshape(nb, 1, 512),
      dkv.reshape(nb, 1, 1024), *([cd] * pp), *([cs] * pp))
    return out.reshape(nb, 1024)


def _proj1_kernel(x_ref, ng_ref, w_ref, qg_ref, kg_ref, c_ref, sa_ref, sb_ref,
                  q_ref, cmp_ref, sel_ref, win_ref, gate_ref, selkv_ref, winkv_ref):
    h = _rms(x_ref[...], ng_ref[...]).astype(BF16)
    ones64 = _group_ones(LANES, HEAD_DIM)
    c, sa, sb = c_ref[...], sa_ref[...], sb_ref[...]
    scale = HEAD_DIM ** -0.5
    for half in range(2):
        y = _dot(h, w_ref[:, half * 512:(half + 1) * 512])
        for j in range(4):
            sl = slice(j * LANES, (j + 1) * LANES)
            q = _head_norm_rope(y[:, sl], qg_ref[...], ones64, c, sa, sb) * scale
            q_ref[:, half * 512 + j * LANES:half * 512 + (j + 1) * LANES] = q.astype(BF16)
    for i, (rows_ref, bf_ref) in enumerate(((cmp_ref, None), (sel_ref, selkv_ref), (win_ref, winkv_ref))):
        y = _dot(h, w_ref[:, 1024 + i * 512:1024 + (i + 1) * 512])
        for j in range(2):
            sl = slice(j * LANES, (j + 1) * LANES)
            k = _head_norm_rope(y[:, sl], kg_ref[i:i + 1, :], ones64, c, sa, sb)
            rows_ref[:, sl] = k
            if bf_ref is not None:
                bf_ref[:, sl] = k.astype(BF16)
        rows_ref[:, 256:512] = y[:, 256:512]
        if bf_ref is not None:
            bf_ref[:, 256:512] = y[:, 256:512].astype(BF16)
    gate_ref[...] = jax.nn.sigmoid(_dot(h, w_ref[:, 2560:2688]))


def _proj1(x, norm_g, w_in, q_g, k_g, tabs, tm):
    t, d = x.shape
    tab_blocks = tabs[0].shape[0] // tm
    row = lambda i: (i, 0)
    fixed = lambda i: (0, 0)
    tab = lambda i: (i % tab_blocks, 0)
    w = jnp.pad(w_in, ((0, 0), (0, 2688 - w_in.shape[1]))).astype(BF16)
    rep = LANES // HEAD_DIM
    outs = [jax.ShapeDtypeStruct((t, 1024), BF16), jax.ShapeDtypeStruct((t, 512), F32),
            jax.ShapeDtypeStruct((t, 512), F32), jax.ShapeDtypeStruct((t, 512), F32),
            jax.ShapeDtypeStruct((t, LANES), F32), jax.ShapeDtypeStruct((t, 512), BF16),
            jax.ShapeDtypeStruct((t, 512), BF16)]
    return pl.pallas_call(
        _proj1_kernel,
        grid=(t // tm,),
        in_specs=[pl.BlockSpec((tm, d), row), pl.BlockSpec((1, d), fixed), pl.BlockSpec(w.shape, fixed),
                  pl.BlockSpec((1, LANES), fixed), pl.BlockSpec((3, LANES), fixed),
                  pl.BlockSpec((tm, LANES), tab), pl.BlockSpec((tm, LANES), tab), pl.BlockSpec((tm, LANES), tab)],
        out_specs=[pl.BlockSpec((tm, o.shape[1]), row) for o in outs],
        out_shape=outs,
        compiler_params=_cparams(("arbitrary",)),
        name="proj1",
    )(x, norm_g.reshape(1, d), w, jnp.tile(q_g.reshape(1, HEAD_DIM), (1, rep)), jnp.tile(k_g, (1, rep)), *tabs)


def _gelu_tanh(x):
    return 0.5 * x * (1.0 + jnp.tanh(0.7978845608028654 * (x + 0.044715 * x * x * x)))


def _compress_kernel(*refs, n_in, n_prefetch):
    refs = refs[n_prefetch:]
    row_refs = refs[:n_in]
    posv_ref, w1_ref, w2_ref, out_ref, carry_ref = refs[n_in:]
    t = pl.program_id(1)
    m_rows = sum(r.shape[1] for r in row_refs)
    first = _iota((m_rows, 1), 0) == 0
    for i in range(2):
        for p in range(2):
            cols = []
            for l in range(CMP_STRIDE):
                c0 = l * 512 + i * 256 + p * LANES
                pieces = [r[0, :, c0:c0 + LANES] for r in row_refs]
                cols.append(pieces[0] if n_in == 1 else jnp.concatenate(pieces, axis=0))
            xcat = jnp.concatenate(cols, axis=1)
            a = _dot((xcat + posv_ref[i, 0]).astype(BF16), w1_ref[i, 0])
            b = _dot((xcat + posv_ref[i, 1]).astype(BF16), w1_ref[i, 1])
            prev = jnp.where(t == 0, jnp.zeros((1, 512), F32), carry_ref[2 * i + p, 0:1, :])
            a_prev = jnp.where(first, prev, pltpu.roll(a, 1, 0))
            carry_ref[2 * i + p, 0:1, :] = a[m_rows - 1:m_rows, :]
            hid = _gelu_tanh(a_prev + b)
            out = _dot(hid.astype(BF16), w2_ref[i])
            out = jnp.where(first & (t == 0), 0.0, out)
            out_ref[0, :, i * 256 + p * LANES:i * 256 + (p + 1) * LANES] = out


def _compress_weights(cmp_pos, w1, w2):
    w1r = w1.reshape(2, 2, CMP_STRIDE, HEAD_DIM, w1.shape[-1])
    hid = w1.shape[-1]
    z = jnp.zeros_like(w1r)
    top = jnp.concatenate([w1r, z], axis=-1)
    bot = jnp.concatenate([z, w1r], axis=-1)
    w1bd = jnp.concatenate([top, bot], axis=3).reshape(2, 2, CMP_STRIDE * LANES, 2 * hid).astype(BF16)
    z2 = jnp.zeros_like(w2)
    w2bd = jnp.concatenate([jnp.concatenate([w2, z2], axis=-1), jnp.concatenate([z2, w2], axis=-1)],
                           axis=1).astype(BF16)
    pv = cmp_pos.reshape(2, CMP_STRIDE, 2, HEAD_DIM)
    pv = jnp.transpose(pv, (2, 0, 1, 3))
    posv = jnp.concatenate([pv, pv], axis=-1).reshape(2, 2, 1, CMP_STRIDE * LANES)
    return posv, w1bd, w2bd


def _compress_call(row_arrays, row_specs, grid, out_map, nb, n_chunks, rows_per_step, weights, prefetch=None):
    posv, w1bd, w2bd = weights
    n_in = len(row_arrays)
    nidx = 2 + (1 if prefetch is not None else 0)
    fix = lambda nd: (lambda *a: (0,) * nd)
    in_specs = list(row_specs) + [pl.BlockSpec(posv.shape, fix(4)), pl.BlockSpec(w1bd.shape, fix(4)),
                                  pl.BlockSpec(w2bd.shape, fix(3))]
    out_spec = pl.BlockSpec((1, rows_per_step, 512), out_map)
    scratch = [pltpu.VMEM((4, 8, 512), F32)]
    kern = functools.partial(_compress_kernel, n_in=n_in, n_prefetch=0 if prefetch is None else 1)
    out_shape = jax.ShapeDtypeStruct((nb, n_chunks, 512), F32)
    if prefetch is None:
        return pl.pallas_call(kern, grid=grid, in_specs=in_specs, out_specs=out_spec, out_shape=out_shape,
                              scratch_shapes=scratch, compiler_params=_cparams(("arbitrary", "arbitrary")),
                              name="compress_prompt")(*row_arrays, posv, w1bd, w2bd)
    gs = pltpu.PrefetchScalarGridSpec(num_scalar_prefetch=1, grid=grid, in_specs=in_specs, out_specs=out_spec,
                                      scratch_shapes=scratch)
    return pl.pallas_call(kern, grid_spec=gs, out_shape=out_shape,
                          compiler_params=_cparams(("arbitrary", "arbitrary")),
                          name="compress_decode")(prefetch, *row_arrays, posv, w1bd, w2bd)


def _compress_prompt(cmp_rows, b, s, weights):
    n_chunks = s // CMP_STRIDE
    y = cmp_rows.reshape(b, n_chunks, CMP_STRIDE * 512)
    spec = pl.BlockSpec((1, n_chunks, CMP_STRIDE * 512), lambda bi, t: (bi, 0, 0))
    return _compress_call([y], [spec], (b, 1), lambda bi, t: (bi, 0, 0), b, n_chunks, n_chunks, weights)


def _compress_decode(cache_cmp, page_table, weights, pages_per_step):
    nb, n_pages = page_table.shape
    n_pool, page = cache_cmp.shape[:2]
    cpp = page // CMP_STRIDE
    pp = pages_per_step
    y = cache_cmp.reshape(n_pool, cpp, CMP_STRIDE * 512)
    specs = [pl.BlockSpec((1, cpp, CMP_STRIDE * 512), functools.partial(
        lambda bi, t, pt, i: (pt[bi, t * pp + i], 0, 0), i=i)) for i in range(pp)]
    return _compress_call([y] * pp, specs, (nb, n_pages // pp), lambda bi, t, pt: (bi, t, 0), nb, n_pages * cpp,
                          pp * cpp, weights, prefetch=page_table)


def _overlap_table(n_entries, n_sel, width):
    start = (np.arange(n_entries)[:, None] - 1) * CMP_STRIDE
    j = np.arange(width)[None, :]
    ov = (start < (j + 1) * L_SEL) & (start + L_CMP > j * L_SEL) & (np.arange(n_entries)[:, None] >= 1) & (j < n_sel)
    return jnp.asarray(ov.astype(np.float32), dtype=BF16)


def _masked_softmax(s, mask):
    s = jnp.where(mask, s, NEG_INF)
    m = jnp.max(s, axis=-1, keepdims=True)
    m = jnp.where(m == NEG_INF, 0.0, m)
    e = jnp.where(mask, jnp.exp(s - m), 0.0)
    return e / jnp.maximum(jnp.sum(e, axis=-1, keepdims=True), 1e-30)


def _select_blocks(imp, qpos, n_sel):
    blk = _iota(imp.shape, 1)
    cur = qpos // L_SEL
    valid = blk * L_SEL <= qpos
    forced = (blk == 0) | (blk == cur) | (blk == cur - 1)
    score = jnp.where(valid, imp + jnp.where(forced, FORCE_BONUS, 0.0), NEG_INF)
    rank = jnp.zeros(imp.shape, F32)
    for k in range(n_sel):
        sk = score[:, k:k + 1]
        ahead = (sk > score) | ((sk == score) & (blk > k))
        rank = rank + jnp.where(ahead, 1.0, 0.0)
    return jnp.where(valid & (rank < N_SEL), 1.0, 0.0)


def _online_update(state, s, mask, v):
    m_run, l_run, acc = state
    s = jnp.where(mask, s, NEG_INF)
    m_new = jnp.maximum(m_run, jnp.max(s, axis=-1, keepdims=True))
    m_safe = jnp.where(m_new == NEG_INF, 0.0, m_new)
    p = jnp.where(mask, jnp.exp(s - m_safe), 0.0)
    alpha = jnp.exp(m_run - m_safe)
    return (m_new, alpha * l_run + jnp.sum(p, axis=-1, keepdims=True), alpha * acc + _dot(p.astype(BF16), v))


def _online_init(rows, width):
    return (jnp.full((rows, 1), NEG_INF, F32), jnp.zeros((rows, 1), F32), jnp.zeros((rows, width), F32))


def _online_finish(state):
    _, l_run, acc = state
    return acc / jnp.maximum(l_run, 1e-30)


def _nsa_prompt_kernel(q_ref, gate_ref, kcvc_ref, selkv_ref, winkv_ref, ovl_ref, expand_ref, o_ref, *, tq, n_sel):
    qi = pl.program_id(1)
    q0 = qi * tq
    nc = kcvc_ref.shape[1]
    lo_mask, hi_mask = _lane_half_masks((tq, LANES))
    qpos1 = q0 + _iota((tq, 1), 0)
    qpos4 = jnp.concatenate([qpos1] * 4, axis=0)
    col4 = _iota((4 * tq, tq), 1)
    gate = gate_ref[0]
    lane_g = _iota((tq, LANES), 1)
    tile4 = lambda a: jnp.concatenate([a] * 4, axis=0)

    for g in range(4):
        p, gh = g // 2, g % 2
        keep = hi_mask if gh else lo_mask
        ksl = slice(p * LANES, (p + 1) * LANES)
        vsl = slice(256 + p * LANES, 256 + (p + 1) * LANES)
        qs = []
        for r in range(4):
            h = g * 4 + r
            qh = q_ref[0, :, (h // 2) * LANES:(h // 2 + 1) * LANES].astype(F32)
            if h % 2 != gh:
                qh = pltpu.roll(qh, HEAD_DIM, 1)
            qs.append(jnp.where(keep, qh, 0.0).astype(BF16))
        qst = jnp.concatenate(qs, axis=0)

        kc = kcvc_ref[0, :, ksl].astype(BF16)
        vc = kcvc_ref[0, :, vsl].astype(BF16)
        ment = _iota((4 * tq, nc), 1)
        cmask = (ment >= 1) & (ment * CMP_STRIDE + (L_CMP - CMP_STRIDE - 1) <= qpos4)
        p_c = _masked_softmax(_dot_nt(qst, kc), cmask)
        p_cb = p_c.astype(BF16)
        o_c = _dot(p_cb, vc)
        p_cr = p_cb.astype(F32)
        psum = p_cr[0:tq] + p_cr[tq:2 * tq] + p_cr[2 * tq:3 * tq] + p_cr[3 * tq:4 * tq]
        sel = _select_blocks(_split2_dot(psum, ovl_ref[...]), qpos1, n_sel).astype(BF16)

        def sel_step(kb, state, diag):
            rows = pl.ds(pl.multiple_of(kb * tq, tq), tq)
            chosen = tile4(_dot(sel, expand_ref[kb])) > 0.5
            if diag:
                chosen = chosen & (q0 + col4 <= qpos4)
            return _online_update(state, _dot_nt(qst, selkv_ref[0, rows, ksl]), chosen, selkv_ref[0, rows, vsl])

        st = lax.fori_loop(0, qi, lambda kb, c: sel_step(kb, c, False), _online_init(4 * tq, LANES))
        o_s = _online_finish(sel_step(qi, st, True))

        def win_step(kb, state):
            rows = pl.ds(pl.multiple_of(kb * tq, tq), tq)
            dist = qpos4 - (kb * tq + col4)
            inside = (dist >= 0) & (dist < WINDOW)
            return _online_update(state, _dot_nt(qst, winkv_ref[0, rows, ksl]), inside, winkv_ref[0, rows, vsl])

        kb_lo = jnp.maximum(qi - (WINDOW + tq - 1) // tq, 0)
        o_w = _online_finish(lax.fori_loop(kb_lo, qi + 1, win_step, _online_init(4 * tq, LANES)))

        placed = []
        for r in range(4):
            h = g * 4 + r
            rs = slice(r * tq, (r + 1) * tq)
            gsel = lambda c: jnp.sum(jnp.where(lane_g == c, gate, 0.0), axis=-1, keepdims=True)
            oh = o_c[rs] * gsel(3 * h) + o_s[rs] * gsel(3 * h + 1) + o_w[rs] * gsel(3 * h + 2)
            if h % 2 != gh:
                oh = pltpu.roll(oh, HEAD_DIM, 1)
            placed.append(oh)
        for c in range(2):
            o_ref[0, :, (g * 2 + c) * LANES:(g * 2 + c + 1) * LANES] = jnp.where(lo_mask, placed[2 * c], placed[2 * c + 1])


def _nsa_prompt(q, gate, kcvc, selkv, winkv, b, s, tq):
    n_sel = -(-s // L_SEL)
    nc = kcvc.shape[1]
    ovl = _overlap_table(nc, n_sel, LANES)
    kpos = np.arange(s).reshape(s // tq, 1, tq)
    expand = jnp.asarray((kpos // L_SEL == np.arange(LANES).reshape(1, LANES, 1)).astype(np.float32), dtype=BF16)
    r3 = lambda a: a.reshape(b, s, a.shape[-1])
    tile = lambda w: pl.BlockSpec((1, tq, w), lambda bi, qi: (bi, qi, 0))
    full = lambda n, w: pl.BlockSpec((1, n, w), lambda bi, qi: (bi, 0, 0))
    out = pl.pallas_call(
        functools.partial(_nsa_prompt_kernel, tq=tq, n_sel=n_sel),
        grid=(b, s // tq),
        in_specs=[tile(1024), tile(LANES), full(nc, 512), full(s, 512), full(s, 512),
                  pl.BlockSpec(ovl.shape, lambda bi, qi: (0, 0)), pl.BlockSpec(expand.shape, lambda bi, qi: (0, 0, 0))],
        out_specs=tile(1024),
        out_shape=jax.ShapeDtypeStruct((b, s, 1024), F32),
        compiler_params=_cparams(("arbitrary", "arbitrary")),
        name="nsa_prompt",
    )(r3(q), r3(gate), kcvc, r3(selkv), r3(winkv), ovl, expand)
    return out.reshape(b * s, 1024)


def _nsa_decode_kernel(pt_ref, q_ref, gate_ref, kcvc_ref, ovl_ref, selnew_ref, winnew_ref, winnewf_ref, state_ref,
                       *rest, pages_per_step, n_sel, past_len):
    pp = pages_per_step
    pages = rest[:pp]
    o_ref, winout_ref = rest[pp:pp + 2]
    qbig_ref, sel_ref, oc_ref, m_ref, l_ref, acc_ref = rest[pp + 2:]
    j = pl.program_id(1)
    page = pages[0].shape[1]
    head_of_lane = _iota((16, 256), 1) // HEAD_DIM
    row16 = _iota((16, 256), 0)
    own = head_of_lane == row16 // 4
    spread = jnp.where(_iota((16, 8), 0) // 4 == _iota((16, 8), 1), 1.0, 0.0).astype(BF16)

    @pl.when(j == 0)
    def _():
        tile_lanes = jnp.where(_iota((HEAD_DIM, 256), 0) == _iota((HEAD_DIM, 256), 1) % HEAD_DIM, 1.0, 0.0)
        qb = _dot(q_ref[0], tile_lanes.astype(BF16))
        qbig = jnp.where(own, qb, 0.0).astype(BF16)
        qbig_ref[...] = qbig
        nc = kcvc_ref.shape[1]
        ment = _iota((16, nc), 1)
        cmask = (ment >= 1) & (ment * CMP_STRIDE + (L_CMP - CMP_STRIDE - 1) <= past_len)
        p_c = _masked_softmax(_dot_nt(qbig, kcvc_ref[0, :, 0:256].astype(BF16)), cmask)
        oc_ref[...] = _dot(p_c.astype(BF16), kcvc_ref[0, :, 256:512].astype(BF16))
        gather = jnp.where(_iota((8, 16), 1) // 4 == _iota((8, 16), 0), 1.0, 0.0).astype(BF16)
        psum = _dot(gather, p_c.astype(BF16))
        imp = _split2_dot(psum, ovl_ref[...])
        sel_ref[...] = _select_blocks(imp, jnp.full((8, 1), past_len, jnp.int32), n_sel)
        m_ref[...] = jnp.full(m_ref.shape, NEG_INF, F32)
        l_ref[...] = jnp.zeros_like(l_ref)
        acc_ref[...] = jnp.zeros_like(acc_ref)

    qbig = qbig_ref[...]
    sel = sel_ref[...].astype(BF16)
    blocks_per_page = page // L_SEL
    for i in range(pp):
        pg = j * pp + i
        kv = pages[i][0]
        want = jnp.where(_iota((256, page), 0) == pg * blocks_per_page + _iota((256, page), 1) // L_SEL, 1.0, 0.0)
        chosen = _dot(spread, _dot(sel, want.astype(BF16)).astype(BF16)) > 0.5
        st = _online_update((m_ref[...], l_ref[...], acc_ref[...]), _dot_nt(qbig, kv[:, 0:256].astype(BF16)),
                            chosen, kv[:, 256:512].astype(BF16))
        m_ref[...], l_ref[...], acc_ref[...] = st

    @pl.when(j == pl.num_programs(1) - 1)
    def _():
        qf = qbig.astype(F32)
        knew = selnew_ref[0]
        s_new = jnp.sum(qf * knew[:, 0:256].astype(F32), axis=-1, keepdims=True)
        last = jnp.where(_iota((256, LANES), 0) == past_len // L_SEL, 1.0, 0.0).astype(BF16)
        has_new = _dot(spread, _dot(sel, last).astype(BF16))[:, 0:1] > 0.5
        m_run, l_run, acc = m_ref[...], l_ref[...], acc_ref[...]
        m_new = jnp.maximum(m_run, jnp.where(has_new, s_new, NEG_INF))
        m_safe = jnp.where(m_new == NEG_INF, 0.0, m_new)
        p_new = jnp.where(has_new, jnp.exp(s_new - m_safe), 0.0)
        alpha = jnp.exp(m_run - m_safe)
        l_s = alpha * l_run + p_new
        acc_s = alpha * acc + p_new.astype(BF16).astype(F32) * knew[:, 256:512].astype(F32)
        o_s = acc_s / jnp.maximum(l_s, 1e-30)

        state = state_ref[0]
        w_buf = state.shape[0]
        s_w = _dot_nt(qbig, state[:, 0:256].astype(BF16))
        wpos = past_len - w_buf + _iota(s_w.shape, 1)
        wmask = (past_len - wpos < WINDOW) & (wpos >= 0)
        wnew = winnew_ref[0]
        s_wn = jnp.sum(qf * wnew[:, 0:256].astype(F32), axis=-1, keepdims=True)
        mw = jnp.maximum(jnp.max(jnp.where(wmask, s_w, NEG_INF), axis=-1, keepdims=True), s_wn)
        e = jnp.where(wmask, jnp.exp(s_w - mw), 0.0)
        en = jnp.exp(s_wn - mw)
        o_w = (_dot(e.astype(BF16), state[:, 256:512].astype(BF16))
               + en.astype(BF16).astype(F32) * wnew[:, 256:512].astype(F32)) / (jnp.sum(e, axis=-1, keepdims=True) + en)
        rolled = pltpu.roll(state, w_buf - 1, 0)
        winout_ref[0] = jnp.where(_iota(state.shape, 0) == w_buf - 1, winnewf_ref[0], rolled)

        gate = jnp.broadcast_to(gate_ref[0], (16, LANES))
        lane = _iota((16, LANES), 1)
        hrow = _iota((16, LANES), 0)
        gsel = lambda br: jnp.sum(jnp.where(lane == 3 * hrow + br, gate, 0.0), axis=-1, keepdims=True)
        o = oc_ref[...] * gsel(0) + o_s * gsel(1) + o_w * gsel(2)
        fold = jnp.where(_iota((256, HEAD_DIM), 0) % HEAD_DIM == _iota((256, HEAD_DIM), 1), 1.0, 0.0).astype(BF16)
        o_ref[0] = _split3_dot(jnp.where(own, o, 0.0), fold)


def _nsa_decode(page_table, q, gate, kcvc, selkv_new, winkv_new, win_new, state, cache_sel, pages_per_step):
    nb, n_pages = page_table.shape
    n_pool, page = cache_sel.shape[:2]
    past_len = n_pages * page
    n_sel = -(-(past_len + 1) // L_SEL)
    assert n_sel <= 256
    pp = pages_per_step
    nc = kcvc.shape[1]
    w_buf = state.shape[1]
    ovl = _overlap_table(nc, n_sel, 256)
    cs = cache_sel.reshape(n_pool, page, 512)
    per_b = lambda b, j, pt: (b, 0, 0)
    specs = [pl.BlockSpec((1, page, 512), functools.partial(lambda b, j, pt, i: (pt[b, j * pp + i], 0, 0), i=i))
             for i in range(pp)]
    grid_spec = pltpu.PrefetchScalarGridSpec(
        num_scalar_prefetch=1,
        grid=(nb, n_pages // pp),
        in_specs=[pl.BlockSpec((1, 16, HEAD_DIM), per_b), pl.BlockSpec((1, 1, LANES), per_b),
                  pl.BlockSpec((1, nc, 512), per_b), pl.BlockSpec(ovl.shape, lambda b, j, pt: (0, 0)),
                  pl.BlockSpec((1, 1, 512), per_b), pl.BlockSpec((1, 1, 512), per_b), pl.BlockSpec((1, 1, 512), per_b),
                  pl.BlockSpec((1, w_buf, 512), per_b)] + specs,
        out_specs=[pl.BlockSpec((1, 16, HEAD_DIM), per_b), pl.BlockSpec((1, w_buf, 512), per_b)],
        scratch_shapes=[pltpu.VMEM((16, 256), BF16), pltpu.VMEM((8, 256), F32), pltpu.VMEM((16, 256), F32),
                        pltpu.VMEM((16, 1), F32), pltpu.VMEM((16, 1), F32), pltpu.VMEM((16, 256), F32)],
    )
    o, win_out = pl.pallas_call(
        functools.partial(_nsa_decode_kernel, pages_per_step=pp, n_sel=n_sel, past_len=past_len),
        grid_spec=grid_spec,
        out_shape=[jax.ShapeDtypeStruct((nb, 16, HEAD_DIM), F32), jax.ShapeDtypeStruct((nb, w_buf, 512), F32)],
        compiler_params=_cparams(("arbitrary", "arbitrary")),
        name="nsa_decode",
    )(page_table, q.reshape(nb, 16, HEAD_DIM), gate.reshape(nb, 1, LANES), kcvc, ovl, selkv_new.reshape(nb, 1, 512),
      winkv_new.reshape(nb, 1, 512), win_new.reshape(nb, 1, 512), state.reshape(nb, w_buf, 512), *([cs] * pp))
    return o.reshape(nb, 1024), win_out


def _outproj_router_kernel(x_ref, o_ref, wo_ref, ng_ref, wr_ref, x1_ref, h_ref, comb_ref):
    x1 = x_ref[...] + _dot(o_ref[...].astype(BF16), wo_ref[...])
    x1_ref[...] = x1
    hf = _rms(x1, ng_ref[...])
    h = hf.astype(BF16)
    h_ref[...] = h
    logits = _dot(h, wr_ref[...])
    lane = _iota(logits.shape, 1)
    logits = jnp.where(lane < N_EXPERTS, logits, NEG_INF)
    m1 = jnp.max(logits, axis=-1, keepdims=True)
    i1 = jnp.min(jnp.where(logits == m1, lane, LANES), axis=-1, keepdims=True)
    rest = jnp.where(lane == i1, NEG_INF, logits)
    m2 = jnp.max(rest, axis=-1, keepdims=True)
    i2 = jnp.min(jnp.where(rest == m2, lane, LANES), axis=-1, keepdims=True)
    e2 = jnp.exp(m2 - m1)
    comb_ref[...] = jnp.where(lane == i1, 1.0 / (1.0 + e2), 0.0) + jnp.where(lane == i2, e2 / (1.0 + e2), 0.0)


def _outproj_router(x, o, w_out, norm_g, w_router, tm):
    t, d = x.shape
    row = lambda i: (i, 0)
    fixed = lambda i: (0, 0)
    wr = jnp.pad(w_router, ((0, 0), (0, LANES - w_router.shape[1]))).astype(BF16)
    outs = [jax.ShapeDtypeStruct((t, d), F32), jax.ShapeDtypeStruct((t, d), BF16), jax.ShapeDtypeStruct((t, LANES), F32)]
    return pl.pallas_call(
        _outproj_router_kernel,
        grid=(t // tm,),
        in_specs=[pl.BlockSpec((tm, d), row), pl.BlockSpec((tm, o.shape[1]), row), pl.BlockSpec(w_out.shape, fixed),
                  pl.BlockSpec((1, d), fixed), pl.BlockSpec(wr.shape, fixed)],
        out_specs=[pl.BlockSpec((tm, a.shape[1]), row) for a in outs],
        out_shape=outs,
        compiler_params=_cparams(("arbitrary",)),
        name="outproj_router",
    )(x, o, w_out.astype(BF16), norm_g.reshape(1, d), wr)


def _moe_kernel(x1_ref, h_ref, comb_ref, wg_ref, wu_ref, wd_ref, y_ref, acc_ref):
    e = pl.program_id(1)
    j = pl.program_id(2)

    @pl.when((e == 0) & (j == 0))
    def _():
        acc_ref[...] = jnp.zeros_like(acc_ref)

    h = h_ref[...]
    g = _dot(h, wg_ref[0])
    u = _dot(h, wu_ref[0])
    a = (g * jax.nn.sigmoid(g) * u).astype(BF16)
    comb = comb_ref[...]
    w_e = jnp.sum(jnp.where(_iota(comb.shape, 1) == e, comb, 0.0), axis=-1, keepdims=True)
    acc_ref[...] += w_e * _dot(a, wd_ref[0])

    @pl.when((e == pl.num_programs(1) - 1) & (j == pl.num_programs(2) - 1))
    def _():
        y_ref[...] = x1_ref[...] + acc_ref[...]


def _moe(x1, h, comb, w_gate_up, w_down, tm, tf):
    t, d = x1.shape
    n_e, ff = w_down.shape[:2]
    nj = ff // tf
    row = lambda i, e, j: (i, 0)
    wgu = w_gate_up.astype(BF16)
    return pl.pallas_call(
        _moe_kernel,
        grid=(t // tm, n_e, nj),
        in_specs=[pl.BlockSpec((tm, d), row), pl.BlockSpec((tm, d), row), pl.BlockSpec((tm, LANES), row),
                  pl.BlockSpec((1, d, tf), lambda i, e, j: (e, 0, j)),
                  pl.BlockSpec((1, d, tf), lambda i, e, j: (e, 0, nj + j)),
                  pl.BlockSpec((1, tf, d), lambda i, e, j: (e, j, 0))],
        out_specs=pl.BlockSpec((tm, d), row),
        out_shape=jax.ShapeDtypeStruct((t, d), F32),
        scratch_shapes=[pltpu.VMEM((tm, d), F32)],
        compiler_params=_cparams(("arbitrary", "arbitrary", "arbitrary")),
        name="moe",
    )(x1, h, comb, wgu, wgu, w_down.astype(BF16))


def kernel(x_prompt, x_sample, cache_l0_diff, cache_l0_sb, cache_l1_cmp, cache_l1_sel, state_l1_win, page_table, l0_norm_mix, l0_w_in, l0_diff_q_norm, l0_diff_k_norm, l0_diff_lambda, l0_diff_subln, l0_w_out, l0_norm_ffn, l0_ffn_w_gate_up, l0_ffn_w_down, l1_norm_mix, l1_w_in, l1_q_norm, l1_k_norm, l1_cmp_pos, l1_cmp_w1, l1_cmp_w2, l1_w_out, l1_norm_ffn, l1_router, l1_moe_w_gate_up, l1_moe_w_down):
    b, s, d = x_prompt.shape
    nb = x_sample.shape[0]
    n_pages, page = page_table.shape[1], cache_l0_diff.shape[1]
    past_len = n_pages * page
    tm = min(512, s)
    tq = min(256, s)
    dec_pages = min(8, n_pages)
    cmp_pages = min(32, n_pages)
    ff_tile = l0_ffn_w_down.shape[0] // 2
    moe_tile = l1_moe_w_down.shape[1] // 7
    tabs_p = _rope_tables(jnp.arange(s, dtype=jnp.int32))
    tabs_s = _rope_tables(jnp.full((nb,), past_len, jnp.int32))
    xp = x_prompt.reshape(b * s, d)
    xs = x_sample.reshape(nb, d)

    dq, diff_p, sq, sb_p, dkv, skv = _proj0(xp, l0_norm_mix, l0_w_in, l0_diff_q_norm, l0_diff_k_norm, tabs_p, tm)
    o = _attn0_prompt(l0_diff_lambda, l0_diff_subln, dq, sq, dkv, skv, b, s, tq)
    xp = _outproj_ffn(xp, o, l0_w_out, l0_norm_ffn, l0_ffn_w_gate_up, l0_ffn_w_down, tm, ff_tile)
    dq, diff_s, sq, sb_s, dkv, skv = _proj0(xs, l0_norm_mix, l0_w_in, l0_diff_q_norm, l0_diff_k_norm, tabs_s, nb)
    o = _attn0_decode(page_table, l0_diff_lambda, l0_diff_subln, dq, sq, dkv, cache_l0_diff, cache_l0_sb, dec_pages)
    xs = _outproj_ffn(xs, o, l0_w_out, l0_norm_ffn, l0_ffn_w_gate_up, l0_ffn_w_down, nb, ff_tile)

    cw = _compress_weights(l1_cmp_pos, l1_cmp_w1, l1_cmp_w2)
    q, cmp_p, sel_p, win_p, gate, selkv, winkv = _proj1(xp, l1_norm_mix, l1_w_in, l1_q_norm, l1_k_norm, tabs_p, tm)
    kcvc = _compress_prompt(cmp_p, b, s, cw)
    o = _nsa_prompt(q, gate, kcvc, selkv, winkv, b, s, tq)
    x1, h, comb = _outproj_router(xp, o, l1_w_out, l1_norm_ffn, l1_router, tm)
    xp = _moe(x1, h, comb, l1_moe_w_gate_up, l1_moe_w_down, min(1024, b * s), moe_tile)

    q, cmp_s, sel_s, win_s_new, gate, selkv, winkv = _proj1(xs, l1_norm_mix, l1_w_in, l1_q_norm, l1_k_norm, tabs_s, nb)
    kcvc = _compress_decode(cache_l1_cmp, page_table, cw, cmp_pages)
    o, win_s = _nsa_decode(page_table, q, gate, kcvc, selkv, winkv, win_s_new, state_l1_win, cache_l1_sel, dec_pages)
    x1, h, comb = _outproj_router(xs, o, l1_w_out, l1_norm_ffn, l1_router, nb)
    xs = _moe(x1, h, comb, l1_moe_w_gate_up, l1_moe_w_down, nb, moe_tile)

    w_keep = min(WINDOW, s)
    hk = cmp_p.shape[-1] // (2 * HEAD_DIM)
    rows5 = lambda a, n: a.reshape(n, -1, 2, a.shape[-1] // (2 * HEAD_DIM), HEAD_DIM)
    return (xp.reshape(b, s, d), xs.reshape(nb, 1, d),
            diff_p.reshape(b, s, 2, 4, 2 * HEAD_DIM), sb_p.reshape(b, s, 2, 8, HEAD_DIM),
            rows5(cmp_p, b), rows5(sel_p, b), rows5(win_p, b)[:, s - w_keep:],
            diff_s.reshape(nb, 1, 2, 4, 2 * HEAD_DIM), sb_s.reshape(nb, 1, 2, 8, HEAD_DIM),
            rows5(cmp_s, nb), rows5(sel_s, nb), rows5(win_s, nb))
```

```python
import functools

import numpy as np
import jax
import jax.numpy as jnp
from jax import lax
from jax.experimental import pallas as pl
from jax.experimental.pallas import tpu as pltpu

F32 = jnp.float32
BF16 = jnp.bfloat16

HEAD_DIM = 64
ROT_DIM = HEAD_DIM // 4
ROPE_THETA = 500000.0
EPS = 1e-6
DIFF_LAMBDA_INIT = 0.2
L_CMP = 32
CMP_STRIDE = 16
L_SEL = 64
N_SEL = 16
WINDOW = 512
FORCE_BONUS = 1.0e4
N_EXPERTS = 8
LANES = 128
VMEM_LIMIT = 56 * 1024 * 1024
NEG_INF = float("-inf")


def _cparams(sem):
    return pltpu.CompilerParams(dimension_semantics=sem, vmem_limit_bytes=VMEM_LIMIT)


def _dot(a, b):
    return jnp.dot(a, b, preferred_element_type=F32)


def _dot_nt(a, b):
    return lax.dot_general(a, b, (((1,), (1,)), ((), ())), preferred_element_type=F32)


def _split2_dot(x, m):
    hi = x.astype(BF16)
    lo = (x - hi.astype(F32)).astype(BF16)
    return _dot(hi, m) + _dot(lo, m)


def _split3_dot(x, m):
    hi = x.astype(BF16)
    r1 = x - hi.astype(F32)
    mid = r1.astype(BF16)
    lo = (r1 - mid.astype(F32)).astype(BF16)
    return _dot(hi, m) + _dot(mid, m) + _dot(lo, m)


def _iota(shape, dim):
    return lax.broadcasted_iota(jnp.int32, shape, dim)


def _group_ones(n, group):
    r = _iota((n, n), 0) // group
    c = _iota((n, n), 1) // group
    return jnp.where(r == c, 1.0, 0.0).astype(BF16)


def _rms(x, g):
    return x * lax.rsqrt(jnp.mean(x * x, axis=-1, keepdims=True) + EPS) * g


def _head_norm_rope(xc, g, ones64, c, sa, sb):
    ms = _split2_dot(xc * xc, ones64) * (1.0 / HEAD_DIM)
    xn = xc * lax.rsqrt(ms + EPS) * g
    return xn * c + pltpu.roll(xn, LANES - ROT_DIM // 2, 1) * sa + pltpu.roll(xn, ROT_DIM // 2, 1) * sb


def _rope_tables(pos):
    half = ROT_DIM // 2
    inv = ROPE_THETA ** (-jnp.arange(half, dtype=F32) * 2.0 / ROT_DIM)
    ang = pos.astype(F32)[:, None] * inv[None, :]
    cos, sin = jnp.cos(ang), jnp.sin(ang)
    n = pos.shape[0]
    one = jnp.ones((n, HEAD_DIM - ROT_DIM), F32)
    zero_h = jnp.zeros((n, half), F32)
    zero_r = jnp.zeros((n, HEAD_DIM - ROT_DIM), F32)
    c = jnp.concatenate([cos, cos, one], axis=1)
    sa = jnp.concatenate([-sin, zero_h, zero_r], axis=1)
    sb = jnp.concatenate([zero_h, sin, zero_r], axis=1)
    tile = lambda t: jnp.concatenate([t, t], axis=1)
    return tile(c), tile(sa), tile(sb)


def _proj0_kernel(x_ref, ng_ref, w_ref, qg_ref, kg_ref, c_ref, sa_ref, sb_ref,
                  dq_ref, drows_ref, sq_ref, srows_ref, dkv_ref, skv_ref):
    h = _rms(x_ref[...], ng_ref[...]).astype(BF16)
    ones64 = _group_ones(LANES, HEAD_DIM)
    c, sa, sb = c_ref[...], sa_ref[...], sb_ref[...]
    scale = HEAD_DIM ** -0.5
    y = _dot(h, w_ref[:, 0:512])
    for j in range(4):
        sl = slice(j * LANES, (j + 1) * LANES)
        dq_ref[:, sl] = (_head_norm_rope(y[:, sl], qg_ref[...], ones64, c, sa, sb) * scale).astype(BF16)
    y = _dot(h, w_ref[:, 512:1024])
    for j in range(4):
        sl = slice(j * LANES, (j + 1) * LANES)
        k = _head_norm_rope(y[:, sl], kg_ref[...], ones64, c, sa, sb)
        drows_ref[:, sl] = k
        dkv_ref[:, sl] = k.astype(BF16)
    y = _dot(h, w_ref[:, 1024:1536])
    drows_ref[:, 512:1024] = y
    dkv_ref[:, 512:1024] = y.astype(BF16)
    y = _dot(h, w_ref[:, 1536:2048])
    sq_ref[...] = (y * scale).astype(BF16)
    y = _dot(h, w_ref[:, 2048:3072])
    srows_ref[...] = y
    skv_ref[...] = y.astype(BF16)


def _proj0(x, norm_g, w_in, q_g, k_g, tabs, tm):
    t, d = x.shape
    tab_blocks = tabs[0].shape[0] // tm
    row = lambda i: (i, 0)
    fixed = lambda i: (0, 0)
    tab = lambda i: (i % tab_blocks, 0)
    g2 = lambda g: jnp.tile(g.reshape(1, HEAD_DIM), (1, LANES // HEAD_DIM))
    outs = [jax.ShapeDtypeStruct((t, 512), BF16), jax.ShapeDtypeStruct((t, 1024), F32),
            jax.ShapeDtypeStruct((t, 512), BF16), jax.ShapeDtypeStruct((t, 1024), F32),
            jax.ShapeDtypeStruct((t, 1024), BF16), jax.ShapeDtypeStruct((t, 1024), BF16)]
    return pl.pallas_call(
        _proj0_kernel,
        grid=(t // tm,),
        in_specs=[pl.BlockSpec((tm, d), row), pl.BlockSpec((1, d), fixed),
                  pl.BlockSpec(w_in.shape, fixed),
                  pl.BlockSpec((1, LANES), fixed), pl.BlockSpec((1, LANES), fixed),
                  pl.BlockSpec((tm, LANES), tab), pl.BlockSpec((tm, LANES), tab), pl.BlockSpec((tm, LANES), tab)],
        out_specs=[pl.BlockSpec((tm, o.shape[1]), row) for o in outs],
        out_shape=outs,
        compiler_params=_cparams(("arbitrary",)),
        name="proj0",
    )(x, norm_g.reshape(1, d), w_in.astype(BF16), g2(q_g), g2(k_g), *tabs)


def _lane_half_masks(shape):
    lane = _iota(shape, 1)
    return lane < HEAD_DIM, lane >= HEAD_DIM


def _diff_lambda(lp):
    a = jnp.sum(lp[0:1] * lp[1:2], axis=-1, keepdims=True)
    b = jnp.sum(lp[2:3] * lp[3:4], axis=-1, keepdims=True)
    return jnp.exp(a) - jnp.exp(b) + DIFF_LAMBDA_INIT


def _attn0_kernel(lam_ref, subg_ref, dq_ref, sq_ref, dkv_ref, skv_ref, o_ref, *, tq):
    qi = pl.program_id(1)
    q0 = qi * tq
    lam = _diff_lambda(lam_ref[...])
    lo_mask, hi_mask = _lane_half_masks((tq, LANES))
    row = _iota((tq, tq), 0)
    col = _iota((tq, tq), 1)
    zero_bf = jnp.zeros((tq, LANES), BF16)

    for h in range(4):
        sl = slice(h * LANES, (h + 1) * LANES)
        vsl = slice(512 + h * LANES, 512 + (h + 1) * LANES)
        q = dq_ref[0, :, sl]
        qm = (jnp.where(lo_mask, q, zero_bf), jnp.where(hi_mask, q, zero_bf))

        def step(kb, carry, diag):
            k = dkv_ref[0, pl.ds(pl.multiple_of(kb * tq, tq), tq), sl]
            v = dkv_ref[0, pl.ds(pl.multiple_of(kb * tq, tq), tq), vsl]
            out = []
            for m in range(2):
                m_run, l_run, acc = carry[m]
                s = _dot_nt(qm[m], k)
                if diag:
                    s = jnp.where(col <= row, s, NEG_INF)
                m_new = jnp.maximum(m_run, jnp.max(s, axis=-1, keepdims=True))
                p = jnp.exp(s - m_new)
                alpha = jnp.exp(m_run - m_new)
                l_new = alpha * l_run + jnp.sum(p, axis=-1, keepdims=True)
                acc = alpha * acc + _dot(p.astype(BF16), v)
                out.append((m_new, l_new, acc))
            return tuple(out)

        init = tuple((jnp.full((tq, 1), NEG_INF, F32), jnp.zeros((tq, 1), F32), jnp.zeros((tq, LANES), F32))
                     for _ in range(2))
        carry = step(qi, init, True)
        carry = lax.fori_loop(0, qi, lambda kb, c: step(kb, c, False), carry)
        (_, l0, a0), (_, l1, a1) = carry
        o = a0 / l0 - lam * (a1 / l1)
        o_ref[0, :, sl] = _rms(o, subg_ref[:, sl]) * (1.0 - DIFF_LAMBDA_INIT)

    upper = jnp.where(row > col, 1.0, 0.0).astype(BF16)
    for p2 in range(4):
        sl = slice(p2 * LANES, (p2 + 1) * LANES)
        vsl = slice(512 + p2 * LANES, 512 + (p2 + 1) * LANES)
        q = sq_ref[0, :, sl]
        qm = (jnp.where(lo_mask, q, zero_bf), jnp.where(hi_mask, q, zero_bf))

        def sb_step(kb, carry, diag):
            k = skv_ref[0, pl.ds(pl.multiple_of(kb * tq, tq), tq), sl]
            v = skv_ref[0, pl.ds(pl.multiple_of(kb * tq, tq), tq), vsl]
            out = []
            for m in range(2):
                tail, acc = carry[m]
                z = _dot_nt(qm[m], k)
                t = jnp.log1p(jnp.exp(-jnp.abs(z)))
                lsp = -(jnp.maximum(-z, 0.0) + t)
                l1m = -(jnp.maximum(z, 0.0) + t)
                if diag:
                    l1m = jnp.where(col < row, l1m, 0.0)
                after = _split2_dot(l1m, upper) + tail
                w = jnp.exp(lsp + after)
                if diag:
                    w = jnp.where(col < row, w, 0.0)
                acc = acc + _dot(w.astype(BF16), v)
                tail = tail + jnp.sum(l1m, axis=-1, keepdims=True)
                out.append((tail, acc))
            return tuple(out)

        init = tuple((jnp.zeros((tq, 1), F32), jnp.zeros((tq, LANES), F32)) for _ in range(2))
        carry = sb_step(qi, init, True)
        carry = lax.fori_loop(0, qi, lambda i, c: sb_step(qi - 1 - i, c, False), carry)
        (_, a0), (_, a1) = carry
        o_ref[0, :, 512 + p2 * LANES:512 + (p2 + 1) * LANES] = jnp.where(lo_mask, a0, a1)


def _attn0_prompt(lam_p, subln_g, dq, sq, dkv, skv, b, s, tq):
    r3 = lambda a: a.reshape(b, s, a.shape[-1])
    qspec = pl.BlockSpec((1, tq, 512), lambda bi, qi: (bi, qi, 0))
    kvspec = pl.BlockSpec((1, s, 1024), lambda bi, qi: (bi, 0, 0))
    fixed = lambda bi, qi: (0, 0)
    out = pl.pallas_call(
        functools.partial(_attn0_kernel, tq=tq),
        grid=(b, s // tq),
        in_specs=[pl.BlockSpec((4, HEAD_DIM), fixed), pl.BlockSpec((1, 512), fixed), qspec, qspec, kvspec, kvspec],
        out_specs=pl.BlockSpec((1, tq, 1024), lambda bi, qi: (bi, qi, 0)),
        out_shape=jax.ShapeDtypeStruct((b, s, 1024), F32),
        compiler_params=_cparams(("arbitrary", "arbitrary")),
        name="attn0_prompt",
    )(lam_p, subln_g.reshape(1, 512), r3(dq), r3(sq), r3(dkv), r3(skv))
    return out.reshape(b * s, 1024)


def _outproj_ffn_kernel(x_ref, o_ref, wo_ref, ng_ref, wg_ref, wu_ref, wd_ref, y_ref, x1_ref, h_ref, acc_ref):
    j = pl.program_id(1)

    @pl.when(j == 0)
    def _():
        x1 = x_ref[...] + _dot(o_ref[...].astype(BF16), wo_ref[...])
        x1_ref[...] = x1
        h_ref[...] = _rms(x1, ng_ref[...]).astype(BF16)
        acc_ref[...] = jnp.zeros_like(acc_ref)

    h = h_ref[...]
    g = _dot(h, wg_ref[...])
    u = _dot(h, wu_ref[...])
    a = (g * jax.nn.sigmoid(g) * u).astype(BF16)
    acc_ref[...] += _dot(a, wd_ref[...])

    @pl.when(j == pl.num_programs(1) - 1)
    def _():
        y_ref[...] = x1_ref[...] + acc_ref[...]


def _outproj_ffn(x, o, w_out, norm_g, w_gate_up, w_down, tm, tf):
    t, d = x.shape
    ff = w_down.shape[0]
    nj = ff // tf
    row = lambda i, j: (i, 0)
    fixed = lambda i, j: (0, 0)
    wgu = w_gate_up.astype(BF16)
    return pl.pallas_call(
        _outproj_ffn_kernel,
        grid=(t // tm, nj),
        in_specs=[pl.BlockSpec((tm, d), row), pl.BlockSpec((tm, o.shape[1]), row),
                  pl.BlockSpec(w_out.shape, fixed), pl.BlockSpec((1, d), fixed),
                  pl.BlockSpec((d, tf), lambda i, j: (0, j)), pl.BlockSpec((d, tf), lambda i, j: (0, nj + j)),
                  pl.BlockSpec((tf, d), lambda i, j: (j, 0))],
        out_specs=pl.BlockSpec((tm, d), row),
        out_shape=jax.ShapeDtypeStruct((t, d), F32),
        scratch_shapes=[pltpu.VMEM((tm, d), F32), pltpu.VMEM((tm, d), BF16), pltpu.VMEM((tm, d), F32)],
        compiler_params=_cparams(("arbitrary", "arbitrary")),
        name="outproj_ffn",
    )(x, o, w_out.astype(BF16), norm_g.reshape(1, d), wgu, wgu, w_down.astype(BF16))


def _rows_select(x, group, nrows):
    w = x.shape[-1]
    keep = _iota((nrows, w), 1) // group == _iota((nrows, w), 0)
    return jnp.where(keep, jnp.broadcast_to(x.astype(F32), (nrows, w)), 0.0).astype(x.dtype)


def _fold_chunks(x):
    return x[:, 0:LANES] + x[:, LANES:2 * LANES] + x[:, 2 * LANES:3 * LANES] + x[:, 3 * LANES:4 * LANES]


def _attn0_decode_kernel(pt_ref, lam_ref, subg_ref, dq_ref, sq_ref, dnew_ref, *rest, pages_per_step):
    pp = pages_per_step
    dpages = rest[:pp]
    spages = rest[pp:2 * pp]
    od_ref, os_ref = rest[2 * pp:2 * pp + 2]
    s_ref, md_ref, ld_ref, accd_ref, tail_ref, accs_ref = rest[2 * pp + 2:]
    sweep = pl.program_id(1)
    j = pl.program_id(2)
    n_steps = pl.num_programs(2)
    qd = _rows_select(dq_ref[0], HEAD_DIM, 8)
    qd128 = _fold_chunks(qd.astype(F32)).astype(BF16)
    nrow = dpages[0].shape[1]
    page = nrow // 8
    new = dnew_ref[0]
    s_new = jnp.sum(qd.astype(F32) * new[:, 0:512].astype(F32), axis=-1, keepdims=True)
    lam = _diff_lambda(lam_ref[...])
    even_row = _iota((8, 1), 0) % 2 == 0

    def combine(pn):
        return jnp.where(even_row, pn - lam * pltpu.roll(pn, 7, 0), 0.0)

    @pl.when((sweep == 0) & (j == 0))
    def _():
        md_ref[...] = s_new
        tail_ref[...] = jnp.zeros_like(tail_ref)
        accs_ref[...] = jnp.zeros_like(accs_ref)

    @pl.when(sweep == 0)
    def _():
        qs = _rows_select(sq_ref[0], HEAD_DIM, 8)
        upper = jnp.where(_iota((page, page), 0) > _iota((page, page), 1), 1.0, 0.0).astype(BF16)
        is_key_row = _iota((8, nrow), 1) % 8 == _iota((8, nrow), 0) // 2
        for i in range(pp):
            s = jnp.where(is_key_row, _dot_nt(qd128, dpages[i][0].astype(BF16)), NEG_INF)
            s_ref[j * pp + i] = s
            md_ref[...] = jnp.maximum(md_ref[...], jnp.max(s, axis=-1, keepdims=True))

            z = _dot(qs, spages[i][0, 0].astype(BF16))
            t = jnp.log1p(jnp.exp(-jnp.abs(z)))
            lsp = -(jnp.maximum(-z, 0.0) + t)
            l1m = -(jnp.maximum(z, 0.0) + t)
            tail = tail_ref[...]
            w = jnp.exp(lsp + _split3_dot(l1m, upper) + tail)
            accs_ref[...] += _dot_nt(w.astype(BF16), spages[i][0, 1].astype(BF16))
            tail_ref[...] = tail + jnp.sum(l1m, axis=-1, keepdims=True)

    @pl.when((sweep == 1) & (j == 0))
    def _():
        m = md_ref[...]
        total = lax.fori_loop(0, s_ref.shape[0],
                              lambda g, t: t + jnp.sum(jnp.exp(s_ref[g] - m), axis=-1, keepdims=True),
                              jnp.exp(s_new - m))
        ld_ref[...] = jnp.maximum(total, 1e-30)
        own_head = _iota((8, 512), 1) // LANES == _iota((8, 512), 0) // 2
        v_new = _fold_chunks(jnp.where(own_head, jnp.broadcast_to(new[:, 512:1024].astype(F32), (8, 512)), 0.0))
        p_new = combine(jnp.exp(s_new - m) / ld_ref[...])
        accd_ref[...] = p_new.astype(BF16).astype(F32) * v_new

    @pl.when(sweep == 1)
    def _():
        for i in range(pp):
            pc = combine(jnp.exp(s_ref[j * pp + i] - md_ref[...]) / ld_ref[...])
            p_on_v = pltpu.roll(pc, 4, 1).astype(BF16)
            accd_ref[...] += _dot(p_on_v, dpages[i][0].astype(BF16))

    @pl.when((sweep == 1) & (j == n_steps - 1))
    def _():
        od_ref[0] = _rms(accd_ref[...], subg_ref[...]) * (1.0 - DIFF_LAMBDA_INIT)
        acs = accs_ref[...]
        os_ref[0] = jnp.sum(jnp.where(_iota(acs.shape, 0) == _iota(acs.shape, 1) // HEAD_DIM, acs, 0.0), axis=0,
                            keepdims=True)


def _attn0_decode(page_table, lam_p, subln_g, dq, sq, dkv, cache_diff, cache_sb, pages_per_step):
    nb, n_pages = page_table.shape
    n_pool, page = cache_diff.shape[:2]
    pp = pages_per_step
    cd = cache_diff.reshape(n_pool, page * 8, LANES)
    cs = jnp.transpose(cache_sb, (0, 2, 3, 4, 1)).reshape(n_pool, 2, 512, page)
    n_steps = n_pages // pp
    fixed = lambda b, sw, j, pt: (0, 0)
    per_b = lambda b, sw, j, pt: (b, 0, 0)
    newest_first = lambda i: (lambda b, j, pt: pt[b, n_pages - 1 - (j * pp + i)])

    def dspec(i):
        pick = newest_first(i)
        return pl.BlockSpec((1, page * 8, LANES), lambda b, sw, j, pt: (pick(b, j, pt), 0, 0))

    def sspec(i):
        pick = newest_first(i)
        return pl.BlockSpec((1, 2, 512, page),
                            lambda b, sw, j, pt: (pick(b, jnp.where(sw == 0, j, n_steps - 1), pt), 0, 0, 0))

    grid_spec = pltpu.PrefetchScalarGridSpec(
        num_scalar_prefetch=1,
        grid=(nb, 2, n_steps),
        in_specs=[pl.BlockSpec((4, HEAD_DIM), fixed), pl.BlockSpec((8, LANES), fixed),
                  pl.BlockSpec((1, 1, 512), per_b), pl.BlockSpec((1, 1, 512), per_b),
                  pl.BlockSpec((1, 1, 1024), per_b)]
                 + [dspec(i) for i in range(pp)] + [sspec(i) for i in range(pp)],
        out_specs=[pl.BlockSpec((1, 8, LANES), per_b), pl.BlockSpec((1, 1, 512), per_b)],
        scratch_shapes=[pltpu.VMEM((n_pages, 8, page * 8), F32),
                        pltpu.VMEM((8, 1), F32), pltpu.VMEM((8, 1), F32), pltpu.VMEM((8, LANES), F32),
                        pltpu.VMEM((8, 1), F32), pltpu.VMEM((8, 512), F32)],
    )
    od, osb = pl.pallas_call(
        functools.partial(_attn0_decode_kernel, pages_per_step=pp),
        grid_spec=grid_spec,
        out_shape=[jax.ShapeDtypeStruct((nb, 8, LANES), F32), jax.ShapeDtypeStruct((nb, 1, 512), F32)],
        compiler_params=_cparams(("arbitrary", "arbitrary", "arbitrary")),
        name="attn0_decode",
    )(page_table, lam_p, jnp.repeat(subln_g, 2, axis=0), dq.reshape(nb, 1, 512), sq.reshape(nb, 1, 512),
      dkv.reshape(nb, 1, 1024), *([cd] * pp), *([cs] * pp))
    return jnp.concatenate([od[:, 0::2, :].reshape(nb, 512), osb.reshape(nb, 512)], axis=-1)


def _proj1_kernel(x_ref, ng_ref, w_ref, qg_ref, kg_ref, c_ref, sa_ref, sb_ref,
                  q_ref, cmp_ref, sel_ref, win_ref, gate_ref, selkv_ref, winkv_ref):
    h = _rms(x_ref[...], ng_ref[...]).astype(BF16)
    ones64 = _group_ones(LANES, HEAD_DIM)
    c, sa, sb = c_ref[...], sa_ref[...], sb_ref[...]
    scale = HEAD_DIM ** -0.5
    for half in range(2):
        y = _dot(h, w_ref[:, half * 512:(half + 1) * 512])
        for j in range(4):
            sl = slice(j * LANES, (j + 1) * LANES)
            q = _head_norm_rope(y[:, sl], qg_ref[...], ones64, c, sa, sb) * scale
            q_ref[:, half * 512 + j * LANES:half * 512 + (j + 1) * LANES] = q.astype(BF16)
    for i, (rows_ref, bf_ref) in enumerate(((cmp_ref, None), (sel_ref, selkv_ref), (win_ref, winkv_ref))):
        y = _dot(h, w_ref[:, 1024 + i * 512:1024 + (i + 1) * 512])
        for j in range(2):
            sl = slice(j * LANES, (j + 1) * LANES)
            k = _head_norm_rope(y[:, sl], kg_ref[i:i + 1, :], ones64, c, sa, sb)
            rows_ref[:, sl] = k
            if bf_ref is not None:
                bf_ref[:, sl] = k.astype(BF16)
        rows_ref[:, 256:512] = y[:, 256:512]
        if bf_ref is not None:
            bf_ref[:, 256:512] = y[:, 256:512].astype(BF16)
    gate_ref[...] = jax.nn.sigmoid(_dot(h, w_ref[:, 2560:2688]))


def _proj1(x, norm_g, w_in, q_g, k_g, tabs, tm):
    t, d = x.shape
    tab_blocks = tabs[0].shape[0] // tm
    row = lambda i: (i, 0)
    fixed = lambda i: (0, 0)
    tab = lambda i: (i % tab_blocks, 0)
    w = jnp.pad(w_in, ((0, 0), (0, 2688 - w_in.shape[1]))).astype(BF16)
    rep = LANES // HEAD_DIM
    outs = [jax.ShapeDtypeStruct((t, 1024), BF16), jax.ShapeDtypeStruct((t, 512), F32),
            jax.ShapeDtypeStruct((t, 512), F32), jax.ShapeDtypeStruct((t, 512), F32),
            jax.ShapeDtypeStruct((t, LANES), F32), jax.ShapeDtypeStruct((t, 512), BF16),
            jax.ShapeDtypeStruct((t, 512), BF16)]
    return pl.pallas_call(
        _proj1_kernel,
        grid=(t // tm,),
        in_specs=[pl.BlockSpec((tm, d), row), pl.BlockSpec((1, d), fixed), pl.BlockSpec(w.shape, fixed),
                  pl.BlockSpec((1, LANES), fixed), pl.BlockSpec((3, LANES), fixed),
                  pl.BlockSpec((tm, LANES), tab), pl.BlockSpec((tm, LANES), tab), pl.BlockSpec((tm, LANES), tab)],
        out_specs=[pl.BlockSpec((tm, o.shape[1]), row) for o in outs],
        out_shape=outs,
        compiler_params=_cparams(("arbitrary",)),
        name="proj1",
    )(x, norm_g.reshape(1, d), w, jnp.tile(q_g.reshape(1, HEAD_DIM), (1, rep)), jnp.tile(k_g, (1, rep)), *tabs)


def _gelu_tanh(x):
    return 0.5 * x * (1.0 + jnp.tanh(0.7978845608028654 * (x + 0.044715 * x * x * x)))


def _compress_kernel(*refs, n_in, n_prefetch):
    refs = refs[n_prefetch:]
    row_refs = refs[:n_in]
    posv_ref, w1_ref, w2_ref, out_ref, carry_ref = refs[n_in:]
    t = pl.program_id(1)
    paged = len(row_refs[0].shape) == 5
    chunks = row_refs[0].shape[3] // CMP_STRIDE if paged else row_refs[0].shape[1]
    m_rows = chunks * n_in
    first = _iota((m_rows, 1), 0) == 0
    for i in range(2):
        for p in range(2):
            cols = []
            for l in range(CMP_STRIDE):
                c0 = l * 512 + i * 256 + p * LANES
                if paged:
                    pieces = [r[0, i, p, pl.ds(l, chunks, stride=CMP_STRIDE), :] for r in row_refs]
                else:
                    pieces = [r[0, :, c0:c0 + LANES] for r in row_refs]
                cols.append(pieces[0] if n_in == 1 else jnp.concatenate(pieces, axis=0))
            xcat = jnp.concatenate(cols, axis=1)
            a = _dot((xcat + posv_ref[i, 0]).astype(BF16), w1_ref[i, 0])
            b = _dot((xcat + posv_ref[i, 1]).astype(BF16), w1_ref[i, 1])
            prev = jnp.where(t == 0, jnp.zeros((1, 512), F32), carry_ref[2 * i + p, 0:1, :])
            a_prev = jnp.where(first, prev, pltpu.roll(a, 1, 0))
            carry_ref[2 * i + p, 0:1, :] = a[m_rows - 1:m_rows, :]
            hid = _gelu_tanh(a_prev + b)
            out = _dot(hid.astype(BF16), w2_ref[i])
            out = jnp.where(first & (t == 0), 0.0, out)
            out_ref[0, :, i * 256 + p * LANES:i * 256 + (p + 1) * LANES] = out


def _compress_weights(cmp_pos, w1, w2):
    w1r = w1.reshape(2, 2, CMP_STRIDE, HEAD_DIM, w1.shape[-1])
    hid = w1.shape[-1]
    z = jnp.zeros_like(w1r)
    top = jnp.concatenate([w1r, z], axis=-1)
    bot = jnp.concatenate([z, w1r], axis=-1)
    w1bd = jnp.concatenate([top, bot], axis=3).reshape(2, 2, CMP_STRIDE * LANES, 2 * hid).astype(BF16)
    z2 = jnp.zeros_like(w2)
    w2bd = jnp.concatenate([jnp.concatenate([w2, z2], axis=-1), jnp.concatenate([z2, w2], axis=-1)],
                           axis=1).astype(BF16)
    pv = cmp_pos.reshape(2, CMP_STRIDE, 2, HEAD_DIM)
    pv = jnp.transpose(pv, (2, 0, 1, 3))
    posv = jnp.concatenate([pv, pv], axis=-1).reshape(2, 2, 1, CMP_STRIDE * LANES)
    return posv, w1bd, w2bd


def _compress_call(row_arrays, row_specs, grid, out_map, nb, n_chunks, rows_per_step, weights, prefetch=None):
    posv, w1bd, w2bd = weights
    n_in = len(row_arrays)
    nidx = 2 + (1 if prefetch is not None else 0)
    fix = lambda nd: (lambda *a: (0,) * nd)
    in_specs = list(row_specs) + [pl.BlockSpec(posv.shape, fix(4)), pl.BlockSpec(w1bd.shape, fix(4)),
                                  pl.BlockSpec(w2bd.shape, fix(3))]
    out_spec = pl.BlockSpec((1, rows_per_step, 512), out_map)
    scratch = [pltpu.VMEM((4, 8, 512), F32)]
    kern = functools.partial(_compress_kernel, n_in=n_in, n_prefetch=0 if prefetch is None else 1)
    out_shape = jax.ShapeDtypeStruct((nb, n_chunks, 512), F32)
    if prefetch is None:
        return pl.pallas_call(kern, grid=grid, in_specs=in_specs, out_specs=out_spec, out_shape=out_shape,
                              scratch_shapes=scratch, compiler_params=_cparams(("arbitrary", "arbitrary")),
                              name="compress_prompt")(*row_arrays, posv, w1bd, w2bd)
    gs = pltpu.PrefetchScalarGridSpec(num_scalar_prefetch=1, grid=grid, in_specs=in_specs, out_specs=out_spec,
                                      scratch_shapes=scratch)
    return pl.pallas_call(kern, grid_spec=gs, out_shape=out_shape,
                          compiler_params=_cparams(("arbitrary", "arbitrary")),
                          name="compress_decode")(prefetch, *row_arrays, posv, w1bd, w2bd)


def _compress_prompt(cmp_rows, b, s, weights):
    n_chunks = s // CMP_STRIDE
    y = cmp_rows.reshape(b, n_chunks, CMP_STRIDE * 512)
    spec = pl.BlockSpec((1, n_chunks, CMP_STRIDE * 512), lambda bi, t: (bi, 0, 0))
    return _compress_call([y], [spec], (b, 1), lambda bi, t: (bi, 0, 0), b, n_chunks, n_chunks, weights)


def _compress_decode(cache_cmp, page_table, weights, pages_per_step):
    nb, n_pages = page_table.shape
    n_pool, page = cache_cmp.shape[:2]
    cpp = page // CMP_STRIDE
    pp = pages_per_step
    y = jnp.swapaxes(jnp.transpose(cache_cmp, (0, 2, 3, 4, 1)).reshape(n_pool, 2, 2, LANES, page), -1, -2)
    specs = [pl.BlockSpec((1, 2, 2, page, LANES), functools.partial(
        lambda bi, t, pt, i: (pt[bi, t * pp + i], 0, 0, 0, 0), i=i)) for i in range(pp)]
    return _compress_call([y] * pp, specs, (nb, n_pages // pp), lambda bi, t, pt: (bi, t, 0), nb, n_pages * cpp,
                          pp * cpp, weights, prefetch=page_table)


def _overlap_table(n_entries, n_sel, width):
    start = (np.arange(n_entries)[:, None] - 1) * CMP_STRIDE
    j = np.arange(width)[None, :]
    ov = (start < (j + 1) * L_SEL) & (start + L_CMP > j * L_SEL) & (np.arange(n_entries)[:, None] >= 1) & (j < n_sel)
    return jnp.asarray(ov.astype(np.float32), dtype=BF16)


def _masked_softmax(s, mask):
    s = jnp.where(mask, s, NEG_INF)
    m = jnp.max(s, axis=-1, keepdims=True)
    m = jnp.where(m == NEG_INF, 0.0, m)
    e = jnp.where(mask, jnp.exp(s - m), 0.0)
    return e / jnp.maximum(jnp.sum(e, axis=-1, keepdims=True), 1e-30)


def _select_blocks(imp, qpos, n_sel):
    blk = _iota(imp.shape, 1)
    cur = qpos // L_SEL
    valid = blk * L_SEL <= qpos
    forced = (blk == 0) | (blk == cur) | (blk == cur - 1)
    score = jnp.where(valid, imp + jnp.where(forced, FORCE_BONUS, 0.0), NEG_INF)
    rank = jnp.zeros(imp.shape, F32)
    for k in range(n_sel):
        sk = score[:, k:k + 1]
        ahead = (sk > score) | ((sk == score) & (blk > k))
        rank = rank + jnp.where(ahead, 1.0, 0.0)
    return jnp.where(valid & (rank < N_SEL), 1.0, 0.0)


def _online_update(state, s, mask, v, v_keys_on_lanes=False):
    m_run, l_run, acc = state
    s = jnp.where(mask, s, NEG_INF)
    m_new = jnp.maximum(m_run, jnp.max(s, axis=-1, keepdims=True))
    m_safe = jnp.where(m_new == NEG_INF, 0.0, m_new)
    p = jnp.where(mask, jnp.exp(s - m_safe), 0.0)
    alpha = jnp.exp(m_run - m_safe)
    pv = _dot_nt(p.astype(BF16), v) if v_keys_on_lanes else _dot(p.astype(BF16), v)
    return (m_new, alpha * l_run + jnp.sum(p, axis=-1, keepdims=True), alpha * acc + pv)


def _online_init(rows, width):
    return (jnp.full((rows, 1), NEG_INF, F32), jnp.zeros((rows, 1), F32), jnp.zeros((rows, width), F32))


def _online_finish(state):
    _, l_run, acc = state
    return acc / jnp.maximum(l_run, 1e-30)


def _nsa_prompt_kernel(q_ref, gate_ref, kcvc_ref, selkv_ref, winkv_ref, ovl_ref, expand_ref, o_ref, *, tq, n_sel):
    qi = pl.program_id(1)
    q0 = qi * tq
    nc = kcvc_ref.shape[1]
    lo_mask, hi_mask = _lane_half_masks((tq, LANES))
    qpos1 = q0 + _iota((tq, 1), 0)
    qpos4 = jnp.concatenate([qpos1] * 4, axis=0)
    col4 = _iota((4 * tq, tq), 1)
    gate = gate_ref[0]
    lane_g = _iota((tq, LANES), 1)
    tile4 = lambda a: jnp.concatenate([a] * 4, axis=0)

    for g in range(4):
        p, gh = g // 2, g % 2
        keep = hi_mask if gh else lo_mask
        ksl = slice(p * LANES, (p + 1) * LANES)
        vsl = slice(256 + p * LANES, 256 + (p + 1) * LANES)
        qs = []
        for r in range(4):
            h = g * 4 + r
            qh = q_ref[0, :, (h // 2) * LANES:(h // 2 + 1) * LANES].astype(F32)
            if h % 2 != gh:
                qh = pltpu.roll(qh, HEAD_DIM, 1)
            qs.append(jnp.where(keep, qh, 0.0).astype(BF16))
        qst = jnp.concatenate(qs, axis=0)

        kc = kcvc_ref[0, :, ksl].astype(BF16)
        vc = kcvc_ref[0, :, vsl].astype(BF16)
        ment = _iota((4 * tq, nc), 1)
        cmask = (ment >= 1) & (ment * CMP_STRIDE + (L_CMP - CMP_STRIDE - 1) <= qpos4)
        p_c = _masked_softmax(_dot_nt(qst, kc), cmask)
        o_c = _dot(p_c.astype(BF16), vc)
        psum = p_c[0:tq] + p_c[tq:2 * tq] + p_c[2 * tq:3 * tq] + p_c[3 * tq:4 * tq]
        sel = _select_blocks(_dot(psum.astype(BF16), ovl_ref[...]), qpos1, n_sel).astype(BF16)

        def sel_step(kb, state, diag):
            rows = pl.ds(pl.multiple_of(kb * tq, tq), tq)
            chosen = tile4(_dot(sel, expand_ref[kb])) > 0.5
            if diag:
                chosen = chosen & (q0 + col4 <= qpos4)
            return _online_update(state, _dot_nt(qst, selkv_ref[0, rows, ksl]), chosen, selkv_ref[0, rows, vsl])

        st = lax.fori_loop(0, qi, lambda kb, c: sel_step(kb, c, False), _online_init(4 * tq, LANES))
        o_s = _online_finish(sel_step(qi, st, True))

        def win_step(kb, state):
            rows = pl.ds(pl.multiple_of(kb * tq, tq), tq)
            dist = qpos4 - (kb * tq + col4)
            inside = (dist >= 0) & (dist < WINDOW)
            return _online_update(state, _dot_nt(qst, winkv_ref[0, rows, ksl]), inside, winkv_ref[0, rows, vsl])

        kb_lo = jnp.maximum(qi - (WINDOW + tq - 1) // tq, 0)
        o_w = _online_finish(lax.fori_loop(kb_lo, qi + 1, win_step, _online_init(4 * tq, LANES)))

        placed = []
        for r in range(4):
            h = g * 4 + r
            rs = slice(r * tq, (r + 1) * tq)
            gsel = lambda c: jnp.sum(jnp.where(lane_g == c, gate, 0.0), axis=-1, keepdims=True)
            oh = o_c[rs] * gsel(3 * h) + o_s[rs] * gsel(3 * h + 1) + o_w[rs] * gsel(3 * h + 2)
            if h % 2 != gh:
                oh = pltpu.roll(oh, HEAD_DIM, 1)
            placed.append(oh)
        for c in range(2):
            o_ref[0, :, (g * 2 + c) * LANES:(g * 2 + c + 1) * LANES] = jnp.where(lo_mask, placed[2 * c], placed[2 * c + 1])


def _nsa_prompt(q, gate, kcvc, selkv, winkv, b, s, tq):
    n_sel = -(-s // L_SEL)
    nc = kcvc.shape[1]
    ovl = _overlap_table(nc, n_sel, LANES)
    kpos = np.arange(s).reshape(s // tq, 1, tq)
    expand = jnp.asarray((kpos // L_SEL == np.arange(LANES).reshape(1, LANES, 1)).astype(np.float32), dtype=BF16)
    r3 = lambda a: a.reshape(b, s, a.shape[-1])
    tile = lambda w: pl.BlockSpec((1, tq, w), lambda bi, qi: (bi, qi, 0))
    full = lambda n, w: pl.BlockSpec((1, n, w), lambda bi, qi: (bi, 0, 0))
    out = pl.pallas_call(
        functools.partial(_nsa_prompt_kernel, tq=tq, n_sel=n_sel),
        grid=(b, s // tq),
        in_specs=[tile(1024), tile(LANES), full(nc, 512), full(s, 512), full(s, 512),
                  pl.BlockSpec(ovl.shape, lambda bi, qi: (0, 0)), pl.BlockSpec(expand.shape, lambda bi, qi: (0, 0, 0))],
        out_specs=tile(1024),
        out_shape=jax.ShapeDtypeStruct((b, s, 1024), F32),
        compiler_params=_cparams(("arbitrary", "arbitrary")),
        name="nsa_prompt",
    )(r3(q), r3(gate), kcvc, r3(selkv), r3(winkv), ovl, expand)
    return out.reshape(b * s, 1024)


def _nsa_decode_kernel(pt_ref, q_ref, gate_ref, kcvc_ref, ovl_ref, selnew_ref, winnew_ref, winnewf_ref, state_ref,
                       *rest, pages_per_step, n_sel, past_len):
    pp = pages_per_step
    kpages = rest[:pp]
    vpages = rest[pp:2 * pp]
    o_ref, winout_ref = rest[2 * pp:2 * pp + 2]
    s_ref, qbig_ref, sel_ref, oc_ref, m_ref, l_ref, acc_ref = rest[2 * pp + 2:]
    sweep = pl.program_id(1)
    j = pl.program_id(2)
    page = kpages[0].shape[-1]
    head_of_lane = _iota((16, 256), 1) // HEAD_DIM
    row16 = _iota((16, 256), 0)
    own = head_of_lane == row16 // 4
    spread = jnp.where((_iota((16, 16), 0) // 4) * 4 == _iota((16, 16), 1), 1.0, 0.0).astype(BF16)

    @pl.when((sweep == 0) & (j == 0))
    def _():
        tile_lanes = jnp.where(_iota((HEAD_DIM, 256), 0) == _iota((HEAD_DIM, 256), 1) % HEAD_DIM, 1.0, 0.0)
        qb = _dot(q_ref[0], tile_lanes.astype(BF16))
        qbig = jnp.where(own, qb, 0.0).astype(BF16)
        qbig_ref[...] = qbig
        nc = kcvc_ref.shape[1]
        ment = _iota((16, nc), 1)
        cmask = (ment >= 1) & (ment * CMP_STRIDE + (L_CMP - CMP_STRIDE - 1) <= past_len)
        p_c = _masked_softmax(_dot_nt(qbig, kcvc_ref[0, :, 0:256].astype(BF16)), cmask)
        oc_ref[...] = _dot(p_c.astype(BF16), kcvc_ref[0, :, 256:512].astype(BF16))
        pair = p_c + pltpu.roll(p_c, 15, 0)
        psum = pair + pltpu.roll(pair, 14, 0)
        imp = _dot(psum.astype(BF16), ovl_ref[...])
        sel_ref[...] = _select_blocks(imp, jnp.full((16, 1), past_len, jnp.int32), n_sel)
        m_ref[...] = jnp.full(m_ref.shape, NEG_INF, F32)

    qbig = qbig_ref[...]
    qf = qbig.astype(F32)
    sel = sel_ref[...].astype(BF16)
    blocks_per_page = page // L_SEL
    knew = selnew_ref[0]
    s_new = jnp.sum(qf * knew[:, 0:256].astype(F32), axis=-1, keepdims=True)

    def new_token_chosen():
        last = jnp.where(_iota((256, LANES), 0) == past_len // L_SEL, 1.0, 0.0).astype(BF16)
        return _dot(spread, _dot(sel, last).astype(BF16))[:, 0:1] > 0.5

    @pl.when(sweep == 0)
    def _():
        for i in range(pp):
            pg = j * pp + i
            want = jnp.where(_iota((256, page), 0) == pg * blocks_per_page + _iota((256, page), 1) // L_SEL, 1.0, 0.0)
            chosen = _dot(spread, _dot(sel, want.astype(BF16)).astype(BF16)) > 0.5
            s = jnp.where(chosen, _dot(qbig, kpages[i][0, 0].astype(BF16)), NEG_INF)
            s_ref[pg] = s
            m_ref[...] = jnp.maximum(m_ref[...], jnp.max(s, axis=-1, keepdims=True))

    @pl.when((sweep == 1) & (j == 0))
    def _():
        has_new = new_token_chosen()
        m = jnp.maximum(m_ref[...], jnp.where(has_new, s_new, NEG_INF))
        m = jnp.where(m == NEG_INF, 0.0, m)
        m_ref[...] = m
        e_new = jnp.where(has_new, jnp.exp(s_new - m), 0.0)
        total = lax.fori_loop(0, s_ref.shape[0],
                              lambda g, t: t + jnp.sum(jnp.exp(s_ref[g] - m), axis=-1, keepdims=True), e_new)
        l_ref[...] = jnp.maximum(total, 1e-30)
        acc_ref[...] = (e_new / l_ref[...]).astype(BF16).astype(F32) * knew[:, 256:512].astype(F32)

    @pl.when(sweep == 1)
    def _():
        for i in range(pp):
            p = jnp.exp(s_ref[j * pp + i] - m_ref[...]) / l_ref[...]
            acc_ref[...] += _dot_nt(p.astype(BF16), vpages[i][0, 0].astype(BF16))

    @pl.when((sweep == 1) & (j == pl.num_programs(2) - 1))
    def _():
        o_s = acc_ref[...]

        w_buf = state_ref.shape[-1]
        s_w = _dot(qbig, state_ref[0, 0].astype(BF16))
        wpos = past_len - w_buf + _iota(s_w.shape, 1)
        wmask = (past_len - wpos < WINDOW) & (wpos >= 0)
        wnew = winnew_ref[0]
        s_wn = jnp.sum(qf * wnew[:, 0:256].astype(F32), axis=-1, keepdims=True)
        mw = jnp.maximum(jnp.max(jnp.where(wmask, s_w, NEG_INF), axis=-1, keepdims=True), s_wn)
        e = jnp.where(wmask, jnp.exp(s_w - mw), 0.0)
        en = jnp.exp(s_wn - mw)
        denom = jnp.maximum(jnp.sum(e, axis=-1, keepdims=True) + en, 1e-30)
        o_w = (_dot_nt((e / denom).astype(BF16), state_ref[0, 1].astype(BF16))
               + (en / denom).astype(BF16).astype(F32) * wnew[:, 256:512].astype(F32))
        newest = _iota((256, w_buf), 1) == w_buf - 1
        for i in range(2):
            winout_ref[0, i] = jnp.where(newest, winnewf_ref[0, i], pltpu.roll(state_ref[0, i], w_buf - 1, 1))

        gate = jnp.broadcast_to(gate_ref[0], (16, LANES))
        lane = _iota((16, LANES), 1)
        hrow = _iota((16, LANES), 0)
        gsel = lambda br: jnp.sum(jnp.where(lane == 3 * hrow + br, gate, 0.0), axis=-1, keepdims=True)
        o = oc_ref[...] * gsel(0) + o_s * gsel(1) + o_w * gsel(2)
        fold = jnp.where(_iota((256, HEAD_DIM), 0) % HEAD_DIM == _iota((256, HEAD_DIM), 1), 1.0, 0.0).astype(BF16)
        o_ref[0] = _split3_dot(jnp.where(own, o, 0.0), fold)


def _nsa_decode(page_table, q, gate, kcvc, selkv_new, winkv_new, win_new, state, cache_sel, pages_per_step):
    nb, n_pages = page_table.shape
    n_pool, page = cache_sel.shape[:2]
    past_len = n_pages * page
    n_sel = -(-(past_len + 1) // L_SEL)
    assert n_sel <= 256
    pp = pages_per_step
    nc = kcvc.shape[1]
    w_buf = state.shape[1]
    ovl = _overlap_table(nc, n_sel, 256)
    keys_last = lambda a: jnp.transpose(a, (0, 2, 3, 4, 1)).reshape(a.shape[0], 2, 256, a.shape[1])
    cs = keys_last(cache_sel)
    n_steps = n_pages // pp
    per_b = lambda b, sw, j, pt: (b, 0, 0)
    per_b4 = lambda b, sw, j, pt: (b, 0, 0, 0)
    kspecs = [pl.BlockSpec((1, 1, 256, page), functools.partial(
        lambda b, sw, j, pt, i: (pt[b, jnp.where(sw == 0, j, n_steps - 1) * pp + i], 0, 0, 0), i=i)) for i in range(pp)]
    vspecs = [pl.BlockSpec((1, 1, 256, page), functools.partial(
        lambda b, sw, j, pt, i: (pt[b, jnp.where(sw == 1, j, 0) * pp + i], 1, 0, 0), i=i)) for i in range(pp)]
    grid_spec = pltpu.PrefetchScalarGridSpec(
        num_scalar_prefetch=1,
        grid=(nb, 2, n_steps),
        in_specs=[pl.BlockSpec((1, 16, HEAD_DIM), per_b), pl.BlockSpec((1, 1, LANES), per_b),
                  pl.BlockSpec((1, nc, 512), per_b), pl.BlockSpec(ovl.shape, lambda b, sw, j, pt: (0, 0)),
                  pl.BlockSpec((1, 1, 512), per_b), pl.BlockSpec((1, 1, 512), per_b), pl.BlockSpec((1, 2, 256, 1), per_b4),
                  pl.BlockSpec((1, 2, 256, w_buf), per_b4)] + kspecs + vspecs,
        out_specs=[pl.BlockSpec((1, 16, HEAD_DIM), per_b), pl.BlockSpec((1, 2, 256, w_buf), per_b4)],
        scratch_shapes=[pltpu.VMEM((n_pages, 16, page), F32),
                        pltpu.VMEM((16, 256), BF16), pltpu.VMEM((16, 256), F32), pltpu.VMEM((16, 256), F32),
                        pltpu.VMEM((16, 1), F32), pltpu.VMEM((16, 1), F32), pltpu.VMEM((16, 256), F32)],
    )
    o, win_out = pl.pallas_call(
        functools.partial(_nsa_decode_kernel, pages_per_step=pp, n_sel=n_sel, past_len=past_len),
        grid_spec=grid_spec,
        out_shape=[jax.ShapeDtypeStruct((nb, 16, HEAD_DIM), F32), jax.ShapeDtypeStruct((nb, 2, 256, w_buf), F32)],
        compiler_params=_cparams(("arbitrary", "arbitrary", "arbitrary")),
        name="nsa_decode",
    )(page_table, q.reshape(nb, 16, HEAD_DIM), gate.reshape(nb, 1, LANES), kcvc, ovl, selkv_new.reshape(nb, 1, 512),
      winkv_new.reshape(nb, 1, 512), win_new.reshape(nb, 2, 256, 1), keys_last(state), *([cs] * (2 * pp)))
    win_rows = jnp.transpose(win_out.reshape(nb, 2, 4, HEAD_DIM, w_buf), (0, 4, 1, 2, 3))
    return o.reshape(nb, 1024), win_rows


def _outproj_router_kernel(x_ref, o_ref, wo_ref, ng_ref, wr_ref, x1_ref, h_ref, comb_ref):
    x1 = x_ref[...] + _dot(o_ref[...].astype(BF16), wo_ref[...])
    x1_ref[...] = x1
    hf = _rms(x1, ng_ref[...])
    h = hf.astype(BF16)
    h_ref[...] = h
    logits = _dot(h, wr_ref[...])
    lane = _iota(logits.shape, 1)
    logits = jnp.where(lane < N_EXPERTS, logits, NEG_INF)
    m1 = jnp.max(logits, axis=-1, keepdims=True)
    i1 = jnp.min(jnp.where(logits == m1, lane, LANES), axis=-1, keepdims=True)
    rest = jnp.where(lane == i1, NEG_INF, logits)
    m2 = jnp.max(rest, axis=-1, keepdims=True)
    i2 = jnp.min(jnp.where(rest == m2, lane, LANES), axis=-1, keepdims=True)
    e2 = jnp.exp(m2 - m1)
    comb_ref[...] = jnp.where(lane == i1, 1.0 / (1.0 + e2), 0.0) + jnp.where(lane == i2, e2 / (1.0 + e2), 0.0)


def _outproj_router(x, o, w_out, norm_g, w_router, tm):
    t, d = x.shape
    row = lambda i: (i, 0)
    fixed = lambda i: (0, 0)
    wr = jnp.pad(w_router, ((0, 0), (0, LANES - w_router.shape[1]))).astype(BF16)
    outs = [jax.ShapeDtypeStruct((t, d), F32), jax.ShapeDtypeStruct((t, d), BF16), jax.ShapeDtypeStruct((t, LANES), F32)]
    return pl.pallas_call(
        _outproj_router_kernel,
        grid=(t // tm,),
        in_specs=[pl.BlockSpec((tm, d), row), pl.BlockSpec((tm, o.shape[1]), row), pl.BlockSpec(w_out.shape, fixed),
                  pl.BlockSpec((1, d), fixed), pl.BlockSpec(wr.shape, fixed)],
        out_specs=[pl.BlockSpec((tm, a.shape[1]), row) for a in outs],
        out_shape=outs,
        compiler_params=_cparams(("arbitrary",)),
        name="outproj_router",
    )(x, o, w_out.astype(BF16), norm_g.reshape(1, d), wr)


def _moe_kernel(x1_ref, h_ref, comb_ref, wg_ref, wu_ref, wd_ref, y_ref, acc_ref):
    e = pl.program_id(1)
    j = pl.program_id(2)

    @pl.when((e == 0) & (j == 0))
    def _():
        acc_ref[...] = jnp.zeros_like(acc_ref)

    h = h_ref[...]
    g = _dot(h, wg_ref[0])
    u = _dot(h, wu_ref[0])
    a = (g * jax.nn.sigmoid(g) * u).astype(BF16)
    comb = comb_ref[...]
    w_e = jnp.sum(jnp.where(_iota(comb.shape, 1) == e, comb, 0.0), axis=-1, keepdims=True)
    acc_ref[...] += w_e * _dot(a, wd_ref[0])

    @pl.when((e == pl.num_programs(1) - 1) & (j == pl.num_programs(2) - 1))
    def _():
        y_ref[...] = x1_ref[...] + acc_ref[...]


def _moe(x1, h, comb, w_gate_up, w_down, tm, tf):
    t, d = x1.shape
    n_e, ff = w_down.shape[:2]
    nj = ff // tf
    row = lambda i, e, j: (i, 0)
    wgu = w_gate_up.astype(BF16)
    return pl.pallas_call(
        _moe_kernel,
        grid=(t // tm, n_e, nj),
        in_specs=[pl.BlockSpec((tm, d), row), pl.BlockSpec((tm, d), row), pl.BlockSpec((tm, LANES), row),
                  pl.BlockSpec((1, d, tf), lambda i, e, j: (e, 0, j)),
                  pl.BlockSpec((1, d, tf), lambda i, e, j: (e, 0, nj + j)),
                  pl.BlockSpec((1, tf, d), lambda i, e, j: (e, j, 0))],
        out_specs=pl.BlockSpec((tm, d), row),
        out_shape=jax.ShapeDtypeStruct((t, d), F32),
        scratch_shapes=[pltpu.VMEM((tm, d), F32)],
        compiler_params=_cparams(("arbitrary", "arbitrary", "arbitrary")),
        name="moe",
    )(x1, h, comb, wgu, wgu, w_down.astype(BF16))


def kernel(x_prompt, x_sample, cache_l0_diff, cache_l0_sb, cache_l1_cmp, cache_l1_sel, state_l1_win, page_table, l0_norm_mix, l0_w_in, l0_diff_q_norm, l0_diff_k_norm, l0_diff_lambda, l0_diff_subln, l0_w_out, l0_norm_ffn, l0_ffn_w_gate_up, l0_ffn_w_down, l1_norm_mix, l1_w_in, l1_q_norm, l1_k_norm, l1_cmp_pos, l1_cmp_w1, l1_cmp_w2, l1_w_out, l1_norm_ffn, l1_router, l1_moe_w_gate_up, l1_moe_w_down):
    b, s, d = x_prompt.shape
    nb = x_sample.shape[0]
    n_pages, page = page_table.shape[1], cache_l0_diff.shape[1]
    past_len = n_pages * page
    tm = min(512, s)
    tq = min(256, s)
    dec_pages = min(8, n_pages)
    cmp_pages = min(32, n_pages)
    ff_tile = l0_ffn_w_down.shape[0] // 2
    moe_tile = l1_moe_w_down.shape[1] // 7
    tabs_p = _rope_tables(jnp.arange(s, dtype=jnp.int32))
    tabs_s = _rope_tables(jnp.full((nb,), past_len, jnp.int32))
    xp = x_prompt.reshape(b * s, d)
    xs = x_sample.reshape(nb, d)

    dq, diff_p, sq, sb_p, dkv, skv = _proj0(xp, l0_norm_mix, l0_w_in, l0_diff_q_norm, l0_diff_k_norm, tabs_p, tm)
    o = _attn0_prompt(l0_diff_lambda, l0_diff_subln, dq, sq, dkv, skv, b, s, tq)
    xp = _outproj_ffn(xp, o, l0_w_out, l0_norm_ffn, l0_ffn_w_gate_up, l0_ffn_w_down, tm, ff_tile)
    dq, diff_s, sq, sb_s, dkv, skv = _proj0(xs, l0_norm_mix, l0_w_in, l0_diff_q_norm, l0_diff_k_norm, tabs_s, nb)
    o = _attn0_decode(page_table, l0_diff_lambda, l0_diff_subln, dq, sq, dkv, cache_l0_diff, cache_l0_sb, dec_pages)
    xs = _outproj_ffn(xs, o, l0_w_out, l0_norm_ffn, l0_ffn_w_gate_up, l0_ffn_w_down, nb, ff_tile)

    cw = _compress_weights(l1_cmp_pos, l1_cmp_w1, l1_cmp_w2)
    q, cmp_p, sel_p, win_p, gate, selkv, winkv = _proj1(xp, l1_norm_mix, l1_w_in, l1_q_norm, l1_k_norm, tabs_p, tm)
    kcvc = _compress_prompt(cmp_p, b, s, cw)
    o = _nsa_prompt(q, gate, kcvc, selkv, winkv, b, s, tq)
    x1, h, comb = _outproj_router(xp, o, l1_w_out, l1_norm_ffn, l1_router, tm)
    xp = _moe(x1, h, comb, l1_moe_w_gate_up, l1_moe_w_down, min(1024, b * s), moe_tile)

    q, cmp_s, sel_s, win_s_new, gate, selkv, winkv = _proj1(xs, l1_norm_mix, l1_w_in, l1_q_norm, l1_k_norm, tabs_s, nb)
    kcvc = _compress_decode(cache_l1_cmp, page_table, cw, cmp_pages)
    o, win_s = _nsa_decode(page_table, q, gate, kcvc, selkv, winkv, win_s_new, state_l1_win, cache_l1_sel, dec_pages)
    x1, h, comb = _outproj_router(xs, o, l1_w_out, l1_norm_ffn, l1_router, nb)
    xs = _moe(x1, h, comb, l1_moe_w_gate_up, l1_moe_w_down, nb, moe_tile)

    w_keep = min(WINDOW, s)
    hk = cmp_p.shape[-1] // (2 * HEAD_DIM)
    rows5 = lambda a, n: a.reshape(n, -1, 2, a.shape[-1] // (2 * HEAD_DIM), HEAD_DIM)
    return (xp.reshape(b, s, d), xs.reshape(nb, 1, d),
            diff_p.reshape(b, s, 2, 4, 2 * HEAD_DIM), sb_p.reshape(b, s, 2, 8, HEAD_DIM),
            rows5(cmp_p, b), rows5(sel_p, b), rows5(win_p, b)[:, s - w_keep:],
            diff_s.reshape(nb, 1, 2, 4, 2 * HEAD_DIM), sb_s.reshape(nb, 1, 2, 8, HEAD_DIM),
            rows5(cmp_s, nb), rows5(sel_s, nb), win_s)
```

```python
import functools

import numpy as np
import jax
import jax.numpy as jnp
from jax import lax
from jax.experimental import pallas as pl
from jax.experimental.pallas import tpu as pltpu

F32 = jnp.float32
BF16 = jnp.bfloat16

HEAD_DIM = 64
ROT_DIM = HEAD_DIM // 4
ROPE_THETA = 500000.0
EPS = 1e-6
DIFF_LAMBDA_INIT = 0.2
L_CMP = 32
CMP_STRIDE = 16
L_SEL = 64
N_SEL = 16
WINDOW = 512
FORCE_BONUS = 1.0e4
N_EXPERTS = 8
LANES = 128
VMEM_LIMIT = 56 * 1024 * 1024
NEG_INF = float("-inf")
SB_LOG_WEIGHT_FLOOR = -110.0


def _cparams(sem):
    return pltpu.CompilerParams(dimension_semantics=sem, vmem_limit_bytes=VMEM_LIMIT)


def _dot(a, b):
    return jnp.dot(a, b, preferred_element_type=F32)


def _dot_nt(a, b):
    return lax.dot_general(a, b, (((1,), (1,)), ((), ())), preferred_element_type=F32)


def _split2_dot(x, m):
    hi = x.astype(BF16)
    lo = (x - hi.astype(F32)).astype(BF16)
    return _dot(hi, m) + _dot(lo, m)


def _split3_dot(x, m):
    hi = x.astype(BF16)
    r1 = x - hi.astype(F32)
    mid = r1.astype(BF16)
    lo = (r1 - mid.astype(F32)).astype(BF16)
    return _dot(hi, m) + _dot(mid, m) + _dot(lo, m)


def _tree(op, xs):
    xs = list(xs)
    while len(xs) > 1:
        xs = [op(a, b) for a, b in zip(xs[0::2], xs[1::2])] + ([xs[-1]] if len(xs) % 2 else [])
    return xs[0]


def _iota(shape, dim):
    return lax.broadcasted_iota(jnp.int32, shape, dim)


def _group_ones(n, group):
    r = _iota((n, n), 0) // group
    c = _iota((n, n), 1) // group
    return jnp.where(r == c, 1.0, 0.0).astype(BF16)


def _rms(x, g):
    return x * lax.rsqrt(jnp.mean(x * x, axis=-1, keepdims=True) + EPS) * g


def _head_norm_rope(xc, g, ones64, c, sa, sb):
    ms = _split2_dot(xc * xc, ones64) * (1.0 / HEAD_DIM)
    xn = xc * lax.rsqrt(ms + EPS) * g
    return xn * c + pltpu.roll(xn, LANES - ROT_DIM // 2, 1) * sa + pltpu.roll(xn, ROT_DIM // 2, 1) * sb


def _rope_tables(pos):
    half = ROT_DIM // 2
    inv = ROPE_THETA ** (-jnp.arange(half, dtype=F32) * 2.0 / ROT_DIM)
    ang = pos.astype(F32)[:, None] * inv[None, :]
    cos, sin = jnp.cos(ang), jnp.sin(ang)
    n = pos.shape[0]
    one = jnp.ones((n, HEAD_DIM - ROT_DIM), F32)
    zero_h = jnp.zeros((n, half), F32)
    zero_r = jnp.zeros((n, HEAD_DIM - ROT_DIM), F32)
    c = jnp.concatenate([cos, cos, one], axis=1)
    sa = jnp.concatenate([-sin, zero_h, zero_r], axis=1)
    sb = jnp.concatenate([zero_h, sin, zero_r], axis=1)
    tile = lambda t: jnp.concatenate([t, t], axis=1)
    return tile(c), tile(sa), tile(sb)


def _proj0_kernel(x_ref, ng_ref, w_ref, qg_ref, kg_ref, c_ref, sa_ref, sb_ref,
                  dq_ref, drows_ref, sq_ref, srows_ref, dkv_ref, skv_ref):
    h = _rms(x_ref[...], ng_ref[...]).astype(BF16)
    ones64 = _group_ones(LANES, HEAD_DIM)
    c, sa, sb = c_ref[...], sa_ref[...], sb_ref[...]
    scale = HEAD_DIM ** -0.5
    y = _dot(h, w_ref[:, 0:512])
    for j in range(4):
        sl = slice(j * LANES, (j + 1) * LANES)
        dq_ref[:, sl] = (_head_norm_rope(y[:, sl], qg_ref[...], ones64, c, sa, sb) * scale).astype(BF16)
    y = _dot(h, w_ref[:, 512:1024])
    for j in range(4):
        sl = slice(j * LANES, (j + 1) * LANES)
        k = _head_norm_rope(y[:, sl], kg_ref[...], ones64, c, sa, sb)
        drows_ref[:, sl] = k
        dkv_ref[:, sl] = k.astype(BF16)
    y = _dot(h, w_ref[:, 1024:1536])
    drows_ref[:, 512:1024] = y
    dkv_ref[:, 512:1024] = y.astype(BF16)
    y = _dot(h, w_ref[:, 1536:2048])
    sq_ref[...] = (y * scale).astype(BF16)
    y = _dot(h, w_ref[:, 2048:3072])
    srows_ref[...] = y
    skv_ref[...] = y.astype(BF16)


def _proj0(x, norm_g, w_in, q_g, k_g, tabs, tm):
    t, d = x.shape
    tab_blocks = tabs[0].shape[0] // tm
    row = lambda i: (i, 0)
    fixed = lambda i: (0, 0)
    tab = lambda i: (i % tab_blocks, 0)
    g2 = lambda g: jnp.tile(g.reshape(1, HEAD_DIM), (1, LANES // HEAD_DIM))
    outs = [jax.ShapeDtypeStruct((t, 512), BF16), jax.ShapeDtypeStruct((t, 1024), F32),
            jax.ShapeDtypeStruct((t, 512), BF16), jax.ShapeDtypeStruct((t, 1024), F32),
            jax.ShapeDtypeStruct((t, 1024), BF16), jax.ShapeDtypeStruct((t, 1024), BF16)]
    return pl.pallas_call(
        _proj0_kernel,
        grid=(t // tm,),
        in_specs=[pl.BlockSpec((tm, d), row), pl.BlockSpec((1, d), fixed),
                  pl.BlockSpec(w_in.shape, fixed),
                  pl.BlockSpec((1, LANES), fixed), pl.BlockSpec((1, LANES), fixed),
                  pl.BlockSpec((tm, LANES), tab), pl.BlockSpec((tm, LANES), tab), pl.BlockSpec((tm, LANES), tab)],
        out_specs=[pl.BlockSpec((tm, o.shape[1]), row) for o in outs],
        out_shape=outs,
        compiler_params=_cparams(("arbitrary",)),
        name="proj0",
    )(x, norm_g.reshape(1, d), w_in.astype(BF16), g2(q_g), g2(k_g), *tabs)


def _lane_half_masks(shape):
    lane = _iota(shape, 1)
    return lane < HEAD_DIM, lane >= HEAD_DIM


def _diff_lambda(lp):
    a = jnp.sum(lp[0:1] * lp[1:2], axis=-1, keepdims=True)
    b = jnp.sum(lp[2:3] * lp[3:4], axis=-1, keepdims=True)
    return jnp.exp(a) - jnp.exp(b) + DIFF_LAMBDA_INIT


def _attn0_kernel(lam_ref, subg_ref, dq_ref, sq_ref, dkv_ref, skv_ref, o_ref, *, tq):
    qi = pl.program_id(1)
    q0 = qi * tq
    lam = _diff_lambda(lam_ref[...])
    lo_mask, hi_mask = _lane_half_masks((tq, LANES))
    row = _iota((tq, tq), 0)
    col = _iota((tq, tq), 1)
    zero_bf = jnp.zeros((tq, LANES), BF16)

    def split_maps(ref):
        out = []
        for c in range(4):
            q = ref[0, :, c * LANES:(c + 1) * LANES]
            out += [jnp.where(lo_mask, q, zero_bf), jnp.where(hi_mask, q, zero_bf)]
        return out

    qd = split_maps(dq_ref)

    def step(kb, carry, diag):
        rows = pl.ds(pl.multiple_of(kb * tq, tq), tq)
        out = []
        for h in range(4):
            k = dkv_ref[0, rows, h * LANES:(h + 1) * LANES]
            v = dkv_ref[0, rows, 512 + h * LANES:512 + (h + 1) * LANES]
            for m in range(2):
                m_run, l_run, acc = carry[2 * h + m]
                s = _dot_nt(qd[2 * h + m], k)
                if diag:
                    s = jnp.where(col <= row, s, NEG_INF)
                m_new = jnp.maximum(m_run, jnp.max(s, axis=-1, keepdims=True))
                p = jnp.exp(s - m_new)
                alpha = jnp.exp(m_run - m_new)
                l_new = alpha * l_run + jnp.sum(p, axis=-1, keepdims=True)
                out.append((m_new, l_new, alpha * acc + _dot(p.astype(BF16), v)))
        return tuple(out)

    init = tuple((jnp.full((tq, 1), NEG_INF, F32), jnp.zeros((tq, 1), F32), jnp.zeros((tq, LANES), F32))
                 for _ in range(8))
    carry = lax.fori_loop(0, qi, lambda kb, c: step(kb, c, False), step(qi, init, True))
    for h in range(4):
        (_, l0, a0), (_, l1, a1) = carry[2 * h], carry[2 * h + 1]
        o = a0 / l0 - lam * (a1 / l1)
        sl = slice(h * LANES, (h + 1) * LANES)
        o_ref[0, :, sl] = _rms(o, subg_ref[:, sl]) * (1.0 - DIFF_LAMBDA_INIT)

    upper = jnp.where(row > col, 1.0, 0.0).astype(BF16)
    qs = split_maps(sq_ref)

    def sb_step(kb, carry, diag):
        rows = pl.ds(pl.multiple_of(kb * tq, tq), tq)
        out = []
        for c in range(4):
            k = skv_ref[0, rows, c * LANES:(c + 1) * LANES]
            v = skv_ref[0, rows, 512 + c * LANES:512 + (c + 1) * LANES]
            for m in range(2):
                tail, acc = carry[2 * c + m]
                z = _dot_nt(qs[2 * c + m], k)
                t = jnp.log1p(jnp.exp(-jnp.abs(z)))
                lsp = -(jnp.maximum(-z, 0.0) + t)
                l1m = -(jnp.maximum(z, 0.0) + t)
                if diag:
                    l1m = jnp.where(col < row, l1m, 0.0)
                w = jnp.exp(lsp + _split2_dot(l1m, upper) + tail)
                if diag:
                    w = jnp.where(col < row, w, 0.0)
                out.append((tail + jnp.sum(l1m, axis=-1, keepdims=True), acc + _dot(w.astype(BF16), v)))
        return tuple(out)

    def still_visible(carry):
        return jnp.max(_tree(jnp.maximum, [t for t, _ in carry])) > SB_LOG_WEIGHT_FLOOR

    def sb_body(c):
        kb, _, carry = c
        carry = sb_step(kb, carry, False)
        return kb - 1, still_visible(carry), carry

    init = tuple((jnp.zeros((tq, 1), F32), jnp.zeros((tq, LANES), F32)) for _ in range(8))
    first = sb_step(qi, init, True)
    _, _, carry = lax.while_loop(lambda c: (c[0] >= 0) & c[1], sb_body, (qi - 1, still_visible(first), first))
    for c in range(4):
        o_ref[0, :, 512 + c * LANES:512 + (c + 1) * LANES] = jnp.where(lo_mask, carry[2 * c][1], carry[2 * c + 1][1])


def _attn0_prompt(lam_p, subln_g, dq, sq, dkv, skv, b, s, tq):
    r3 = lambda a: a.reshape(b, s, a.shape[-1])
    qspec = pl.BlockSpec((1, tq, 512), lambda bi, qi: (bi, qi, 0))
    kvspec = pl.BlockSpec((1, s, 1024), lambda bi, qi: (bi, 0, 0))
    fixed = lambda bi, qi: (0, 0)
    out = pl.pallas_call(
        functools.partial(_attn0_kernel, tq=tq),
        grid=(b, s // tq),
        in_specs=[pl.BlockSpec((4, HEAD_DIM), fixed), pl.BlockSpec((1, 512), fixed), qspec, qspec, kvspec, kvspec],
        out_specs=pl.BlockSpec((1, tq, 1024), lambda bi, qi: (bi, qi, 0)),
        out_shape=jax.ShapeDtypeStruct((b, s, 1024), F32),
        compiler_params=_cparams(("arbitrary", "arbitrary")),
        name="attn0_prompt",
    )(lam_p, subln_g.reshape(1, 512), r3(dq), r3(sq), r3(dkv), r3(skv))
    return out.reshape(b * s, 1024)


def _outproj_ffn_kernel(x_ref, o_ref, wo_ref, ng_ref, wg_ref, wu_ref, wd_ref, y_ref, x1_ref, h_ref, acc_ref):
    j = pl.program_id(1)

    @pl.when(j == 0)
    def _():
        x1 = x_ref[...] + _dot(o_ref[...].astype(BF16), wo_ref[...])
        x1_ref[...] = x1
        h_ref[...] = _rms(x1, ng_ref[...]).astype(BF16)
        acc_ref[...] = jnp.zeros_like(acc_ref)

    h = h_ref[...]
    g = _dot(h, wg_ref[...])
    u = _dot(h, wu_ref[...])
    a = (g * jax.nn.sigmoid(g) * u).astype(BF16)
    acc_ref[...] += _dot(a, wd_ref[...])

    @pl.when(j == pl.num_programs(1) - 1)
    def _():
        y_ref[...] = x1_ref[...] + acc_ref[...]


def _outproj_ffn(x, o, w_out, norm_g, w_gate_up, w_down, tm, tf):
    t, d = x.shape
    ff = w_down.shape[0]
    nj = ff // tf
    row = lambda i, j: (i, 0)
    fixed = lambda i, j: (0, 0)
    wgu = w_gate_up.astype(BF16)
    return pl.pallas_call(
        _outproj_ffn_kernel,
        grid=(t // tm, nj),
        in_specs=[pl.BlockSpec((tm, d), row), pl.BlockSpec((tm, o.shape[1]), row),
                  pl.BlockSpec(w_out.shape, fixed), pl.BlockSpec((1, d), fixed),
                  pl.BlockSpec((d, tf), lambda i, j: (0, j)), pl.BlockSpec((d, tf), lambda i, j: (0, nj + j)),
                  pl.BlockSpec((tf, d), lambda i, j: (j, 0))],
        out_specs=pl.BlockSpec((tm, d), row),
        out_shape=jax.ShapeDtypeStruct((t, d), F32),
        scratch_shapes=[pltpu.VMEM((tm, d), F32), pltpu.VMEM((tm, d), BF16), pltpu.VMEM((tm, d), F32)],
        compiler_params=_cparams(("arbitrary", "arbitrary")),
        name="outproj_ffn",
    )(x, o, w_out.astype(BF16), norm_g.reshape(1, d), wgu, wgu, w_down.astype(BF16))


def _rows_select(x, group, nrows):
    w = x.shape[-1]
    keep = _iota((nrows, w), 1) // group == _iota((nrows, w), 0)
    return jnp.where(keep, jnp.broadcast_to(x.astype(F32), (nrows, w)), 0.0).astype(x.dtype)


def _fold_chunks(x):
    return x[:, 0:LANES] + x[:, LANES:2 * LANES] + x[:, 2 * LANES:3 * LANES] + x[:, 3 * LANES:4 * LANES]


def _attn0_decode_kernel(pt_ref, lam_ref, subg_ref, dq_ref, sq_ref, dnew_ref, *rest, pages_per_step):
    pp = pages_per_step
    dpages = rest[:pp]
    spages = rest[pp:2 * pp]
    od_ref, os_ref = rest[2 * pp:2 * pp + 2]
    s_ref, md_ref, ld_ref, accd_ref, tail_ref, accs_ref = rest[2 * pp + 2:]
    sweep = pl.program_id(1)
    j = pl.program_id(2)
    n_steps = pl.num_programs(2)
    qd = _rows_select(dq_ref[0], HEAD_DIM, 8)
    qd128 = _fold_chunks(qd.astype(F32)).astype(BF16)
    nrow = dpages[0].shape[1]
    page = nrow // 8
    new = dnew_ref[0]
    s_new = jnp.sum(qd.astype(F32) * new[:, 0:512].astype(F32), axis=-1, keepdims=True)
    lam = _diff_lambda(lam_ref[...])
    even_row = _iota((8, 1), 0) % 2 == 0

    def combine(pn):
        return jnp.where(even_row, pn - lam * pltpu.roll(pn, 7, 0), 0.0)

    @pl.when((sweep == 0) & (j == 0))
    def _():
        md_ref[...] = s_new
        tail_ref[...] = jnp.zeros_like(tail_ref)
        accs_ref[...] = jnp.zeros_like(accs_ref)

    @pl.when(sweep == 0)
    def _():
        qs = _rows_select(sq_ref[0], HEAD_DIM, 8)
        upper = jnp.where(_iota((page, page), 0) > _iota((page, page), 1), 1.0, 0.0).astype(BF16)
        is_key_row = _iota((8, nrow), 1) % 8 == _iota((8, nrow), 0) // 2
        tops = []
        for i in range(pp):
            s = jnp.where(is_key_row, _dot_nt(qd128, dpages[i][0].astype(BF16)), NEG_INF)
            s_ref[j * pp + i] = s
            tops.append(jnp.max(s, axis=-1, keepdims=True))
        md_ref[...] = jnp.maximum(md_ref[...], _tree(jnp.maximum, tops))

        z = jnp.concatenate([_dot(qs, spages[i][0, 0].astype(BF16)) for i in range(pp)], axis=0)
        t = jnp.log1p(jnp.exp(-jnp.abs(z)))
        lsp = -(jnp.maximum(-z, 0.0) + t)
        l1m = -(jnp.maximum(z, 0.0) + t)
        within = _split3_dot(l1m, upper)
        totals = jnp.sum(l1m, axis=-1, keepdims=True)
        run = tail_ref[...]
        tails = []
        for i in range(pp):
            tails.append(run)
            run = run + totals[i * 8:(i + 1) * 8]
        tail_ref[...] = run
        w = jnp.exp(lsp + within + jnp.concatenate(tails, axis=0)).astype(BF16)
        accs_ref[...] += _tree(jnp.add, [_dot_nt(w[i * 8:(i + 1) * 8], spages[i][0, 1].astype(BF16))
                                         for i in range(pp)])

    @pl.when((sweep == 1) & (j == 0))
    def _():
        m = md_ref[...]
        total = lax.fori_loop(0, s_ref.shape[0],
                              lambda g, t: t + jnp.sum(jnp.exp(s_ref[g] - m), axis=-1, keepdims=True),
                              jnp.exp(s_new - m))
        ld_ref[...] = jnp.maximum(total, 1e-30)
        own_head = _iota((8, 512), 1) // LANES == _iota((8, 512), 0) // 2
        v_new = _fold_chunks(jnp.where(own_head, jnp.broadcast_to(new[:, 512:1024].astype(F32), (8, 512)), 0.0))
        p_new = combine(jnp.exp(s_new - m) / ld_ref[...])
        accd_ref[...] = p_new.astype(BF16).astype(F32) * v_new

    @pl.when(sweep == 1)
    def _():
        m, l = md_ref[...], ld_ref[...]
        parts = []
        for i in range(pp):
            pc = combine(jnp.exp(s_ref[j * pp + i] - m) / l)
            p_on_v = pltpu.roll(pc, 4, 1).astype(BF16)
            parts.append(_dot(p_on_v, dpages[i][0].astype(BF16)))
        accd_ref[...] += _tree(jnp.add, parts)

    @pl.when((sweep == 1) & (j == n_steps - 1))
    def _():
        od_ref[0] = _rms(accd_ref[...], subg_ref[...]) * (1.0 - DIFF_LAMBDA_INIT)
        acs = accs_ref[...]
        os_ref[0] = jnp.sum(jnp.where(_iota(acs.shape, 0) == _iota(acs.shape, 1) // HEAD_DIM, acs, 0.0), axis=0,
                            keepdims=True)


def _attn0_decode(page_table, lam_p, subln_g, dq, sq, dkv, cache_diff, cache_sb, pages_per_step):
    nb, n_pages = page_table.shape
    n_pool, page = cache_diff.shape[:2]
    pp = pages_per_step
    cd = cache_diff.reshape(n_pool, page * 8, LANES)
    cs = jnp.transpose(cache_sb, (0, 2, 3, 4, 1)).reshape(n_pool, 2, 512, page)
    n_steps = n_pages // pp
    fixed = lambda b, sw, j, pt: (0, 0)
    per_b = lambda b, sw, j, pt: (b, 0, 0)
    newest_first = lambda i: (lambda b, j, pt: pt[b, n_pages - 1 - (j * pp + i)])

    def dspec(i):
        pick = newest_first(i)
        return pl.BlockSpec((1, page * 8, LANES), lambda b, sw, j, pt: (pick(b, j, pt), 0, 0))

    def sspec(i):
        pick = newest_first(i)
        return pl.BlockSpec((1, 2, 512, page),
                            lambda b, sw, j, pt: (pick(b, jnp.where(sw == 0, j, n_steps - 1), pt), 0, 0, 0))

    grid_spec = pltpu.PrefetchScalarGridSpec(
        num_scalar_prefetch=1,
        grid=(nb, 2, n_steps),
        in_specs=[pl.BlockSpec((4, HEAD_DIM), fixed), pl.BlockSpec((8, LANES), fixed),
                  pl.BlockSpec((1, 1, 512), per_b), pl.BlockSpec((1, 1, 512), per_b),
                  pl.BlockSpec((1, 1, 1024), per_b)]
                 + [dspec(i) for i in range(pp)] + [sspec(i) for i in range(pp)],
        out_specs=[pl.BlockSpec((1, 8, LANES), per_b), pl.BlockSpec((1, 1, 512), per_b)],
        scratch_shapes=[pltpu.VMEM((n_pages, 8, page * 8), F32),
                        pltpu.VMEM((8, 1), F32), pltpu.VMEM((8, 1), F32), pltpu.VMEM((8, LANES), F32),
                        pltpu.VMEM((8, 1), F32), pltpu.VMEM((8, 512), F32)],
    )
    od, osb = pl.pallas_call(
        functools.partial(_attn0_decode_kernel, pages_per_step=pp),
        grid_spec=grid_spec,
        out_shape=[jax.ShapeDtypeStruct((nb, 8, LANES), F32), jax.ShapeDtypeStruct((nb, 1, 512), F32)],
        compiler_params=_cparams(("arbitrary", "arbitrary", "arbitrary")),
        name="attn0_decode",
    )(page_table, lam_p, jnp.repeat(subln_g, 2, axis=0), dq.reshape(nb, 1, 512), sq.reshape(nb, 1, 512),
      dkv.reshape(nb, 1, 1024), *([cd] * pp), *([cs] * pp))
    return jnp.concatenate([od[:, 0::2, :].reshape(nb, 512), osb.reshape(nb, 512)], axis=-1)


def _proj1_kernel(x_ref, ng_ref, w_ref, qg_ref, kg_ref, c_ref, sa_ref, sb_ref,
                  q_ref, cmp_ref, sel_ref, win_ref, gate_ref, selkv_ref, winkv_ref):
    h = _rms(x_ref[...], ng_ref[...]).astype(BF16)
    ones64 = _group_ones(LANES, HEAD_DIM)
    c, sa, sb = c_ref[...], sa_ref[...], sb_ref[...]
    scale = HEAD_DIM ** -0.5
    for half in range(2):
        y = _dot(h, w_ref[:, half * 512:(half + 1) * 512])
        for j in range(4):
            sl = slice(j * LANES, (j + 1) * LANES)
            q = _head_norm_rope(y[:, sl], qg_ref[...], ones64, c, sa, sb) * scale
            q_ref[:, half * 512 + j * LANES:half * 512 + (j + 1) * LANES] = q.astype(BF16)
    for i, (rows_ref, bf_ref) in enumerate(((cmp_ref, None), (sel_ref, selkv_ref), (win_ref, winkv_ref))):
        y = _dot(h, w_ref[:, 1024 + i * 512:1024 + (i + 1) * 512])
        for j in range(2):
            sl = slice(j * LANES, (j + 1) * LANES)
            k = _head_norm_rope(y[:, sl], kg_ref[i:i + 1, :], ones64, c, sa, sb)
            rows_ref[:, sl] = k
            if bf_ref is not None:
                bf_ref[:, sl] = k.astype(BF16)
        rows_ref[:, 256:512] = y[:, 256:512]
        if bf_ref is not None:
            bf_ref[:, 256:512] = y[:, 256:512].astype(BF16)
    gate_ref[...] = jax.nn.sigmoid(_dot(h, w_ref[:, 2560:2688]))


def _proj1(x, norm_g, w_in, q_g, k_g, tabs, tm):
    t, d = x.shape
    tab_blocks = tabs[0].shape[0] // tm
    row = lambda i: (i, 0)
    fixed = lambda i: (0, 0)
    tab = lambda i: (i % tab_blocks, 0)
    w = jnp.pad(w_in, ((0, 0), (0, 2688 - w_in.shape[1]))).astype(BF16)
    rep = LANES // HEAD_DIM
    outs = [jax.ShapeDtypeStruct((t, 1024), BF16), jax.ShapeDtypeStruct((t, 512), F32),
            jax.ShapeDtypeStruct((t, 512), F32), jax.ShapeDtypeStruct((t, 512), F32),
            jax.ShapeDtypeStruct((t, LANES), F32), jax.ShapeDtypeStruct((t, 512), BF16),
            jax.ShapeDtypeStruct((t, 512), BF16)]
    return pl.pallas_call(
        _proj1_kernel,
        grid=(t // tm,),
        in_specs=[pl.BlockSpec((tm, d), row), pl.BlockSpec((1, d), fixed), pl.BlockSpec(w.shape, fixed),
                  pl.BlockSpec((1, LANES), fixed), pl.BlockSpec((3, LANES), fixed),
                  pl.BlockSpec((tm, LANES), tab), pl.BlockSpec((tm, LANES), tab), pl.BlockSpec((tm, LANES), tab)],
        out_specs=[pl.BlockSpec((tm, o.shape[1]), row) for o in outs],
        out_shape=outs,
        compiler_params=_cparams(("arbitrary",)),
        name="proj1",
    )(x, norm_g.reshape(1, d), w, jnp.tile(q_g.reshape(1, HEAD_DIM), (1, rep)), jnp.tile(k_g, (1, rep)), *tabs)


def _gelu_tanh(x):
    return 0.5 * x * (1.0 + jnp.tanh(0.7978845608028654 * (x + 0.044715 * x * x * x)))


def _compress_kernel(*refs, n_in, n_prefetch):
    refs = refs[n_prefetch:]
    row_refs = refs[:n_in]
    posv_ref, w1_ref, w2_ref, out_ref, carry_ref = refs[n_in:]
    t = pl.program_id(1)
    paged = len(row_refs[0].shape) == 5
    chunks = row_refs[0].shape[3] // CMP_STRIDE if paged else row_refs[0].shape[1]
    m_rows = chunks * n_in
    first = _iota((m_rows, 1), 0) == 0
    for i in range(2):
        for p in range(2):
            cols = []
            for l in range(CMP_STRIDE):
                c0 = l * 512 + i * 256 + p * LANES
                if paged:
                    pieces = [r[0, i, p, pl.ds(l, chunks, stride=CMP_STRIDE), :] for r in row_refs]
                else:
                    pieces = [r[0, :, c0:c0 + LANES] for r in row_refs]
                cols.append(pieces[0] if n_in == 1 else jnp.concatenate(pieces, axis=0))
            xcat = jnp.concatenate(cols, axis=1)
            a = _dot((xcat + posv_ref[i, 0]).astype(BF16), w1_ref[i, 0])
            b = _dot((xcat + posv_ref[i, 1]).astype(BF16), w1_ref[i, 1])
            prev = jnp.where(t == 0, jnp.zeros((1, 512), F32), carry_ref[2 * i + p, 0:1, :])
            a_prev = jnp.where(first, prev, pltpu.roll(a, 1, 0))
            carry_ref[2 * i + p, 0:1, :] = a[m_rows - 1:m_rows, :]
            hid = _gelu_tanh(a_prev + b)
            out = _dot(hid.astype(BF16), w2_ref[i])
            out = jnp.where(first & (t == 0), 0.0, out)
            out_ref[0, :, i * 256 + p * LANES:i * 256 + (p + 1) * LANES] = out


def _compress_weights(cmp_pos, w1, w2):
    w1r = w1.reshape(2, 2, CMP_STRIDE, HEAD_DIM, w1.shape[-1])
    hid = w1.shape[-1]
    z = jnp.zeros_like(w1r)
    top = jnp.concatenate([w1r, z], axis=-1)
    bot = jnp.concatenate([z, w1r], axis=-1)
    w1bd = jnp.concatenate([top, bot], axis=3).reshape(2, 2, CMP_STRIDE * LANES, 2 * hid).astype(BF16)
    z2 = jnp.zeros_like(w2)
    w2bd = jnp.concatenate([jnp.concatenate([w2, z2], axis=-1), jnp.concatenate([z2, w2], axis=-1)],
                           axis=1).astype(BF16)
    pv = cmp_pos.reshape(2, CMP_STRIDE, 2, HEAD_DIM)
    pv = jnp.transpose(pv, (2, 0, 1, 3))
    posv = jnp.concatenate([pv, pv], axis=-1).reshape(2, 2, 1, CMP_STRIDE * LANES)
    return posv, w1bd, w2bd


def _compress_call(row_arrays, row_specs, grid, out_map, nb, n_chunks, rows_per_step, weights, prefetch=None):
    posv, w1bd, w2bd = weights
    n_in = len(row_arrays)
    nidx = 2 + (1 if prefetch is not None else 0)
    fix = lambda nd: (lambda *a: (0,) * nd)
    in_specs = list(row_specs) + [pl.BlockSpec(posv.shape, fix(4)), pl.BlockSpec(w1bd.shape, fix(4)),
                                  pl.BlockSpec(w2bd.shape, fix(3))]
    out_spec = pl.BlockSpec((1, rows_per_step, 512), out_map)
    scratch = [pltpu.VMEM((4, 8, 512), F32)]
    kern = functools.partial(_compress_kernel, n_in=n_in, n_prefetch=0 if prefetch is None else 1)
    out_shape = jax.ShapeDtypeStruct((nb, n_chunks, 512), F32)
    if prefetch is None:
        return pl.pallas_call(kern, grid=grid, in_specs=in_specs, out_specs=out_spec, out_shape=out_shape,
                              scratch_shapes=scratch, compiler_params=_cparams(("arbitrary", "arbitrary")),
                              name="compress_prompt")(*row_arrays, posv, w1bd, w2bd)
    gs = pltpu.PrefetchScalarGridSpec(num_scalar_prefetch=1, grid=grid, in_specs=in_specs, out_specs=out_spec,
                                      scratch_shapes=scratch)
    return pl.pallas_call(kern, grid_spec=gs, out_shape=out_shape,
                          compiler_params=_cparams(("arbitrary", "arbitrary")),
                          name="compress_decode")(prefetch, *row_arrays, posv, w1bd, w2bd)


def _compress_prompt(cmp_rows, b, s, weights):
    n_chunks = s // CMP_STRIDE
    y = cmp_rows.reshape(b, n_chunks, CMP_STRIDE * 512)
    spec = pl.BlockSpec((1, n_chunks, CMP_STRIDE * 512), lambda bi, t: (bi, 0, 0))
    return _compress_call([y], [spec], (b, 1), lambda bi, t: (bi, 0, 0), b, n_chunks, n_chunks, weights)


def _compress_decode(cache_cmp, page_table, weights, pages_per_step):
    nb, n_pages = page_table.shape
    n_pool, page = cache_cmp.shape[:2]
    cpp = page // CMP_STRIDE
    pp = pages_per_step
    y = jnp.swapaxes(jnp.transpose(cache_cmp, (0, 2, 3, 4, 1)).reshape(n_pool, 2, 2, LANES, page), -1, -2)
    specs = [pl.BlockSpec((1, 2, 2, page, LANES), functools.partial(
        lambda bi, t, pt, i: (pt[bi, t * pp + i], 0, 0, 0, 0), i=i)) for i in range(pp)]
    return _compress_call([y] * pp, specs, (nb, n_pages // pp), lambda bi, t, pt: (bi, t, 0), nb, n_pages * cpp,
                          pp * cpp, weights, prefetch=page_table)


def _overlap_table(n_entries, n_sel, width):
    start = (np.arange(n_entries)[:, None] - 1) * CMP_STRIDE
    j = np.arange(width)[None, :]
    ov = (start < (j + 1) * L_SEL) & (start + L_CMP > j * L_SEL) & (np.arange(n_entries)[:, None] >= 1) & (j < n_sel)
    return jnp.asarray(ov.astype(np.float32), dtype=BF16)


def _masked_softmax(s, mask):
    s = jnp.where(mask, s, NEG_INF)
    m = jnp.max(s, axis=-1, keepdims=True)
    m = jnp.where(m == NEG_INF, 0.0, m)
    e = jnp.where(mask, jnp.exp(s - m), 0.0)
    return e / jnp.maximum(jnp.sum(e, axis=-1, keepdims=True), 1e-30)


def _select_blocks(imp, qpos, n_sel):
    blk = _iota(imp.shape, 1)
    cur = qpos // L_SEL
    valid = blk * L_SEL <= qpos
    forced = (blk == 0) | (blk == cur) | (blk == cur - 1)
    score = jnp.where(valid, imp + jnp.where(forced, FORCE_BONUS, 0.0), NEG_INF)
    rank = jnp.zeros(imp.shape, F32)
    for k in range(n_sel):
        sk = score[:, k:k + 1]
        ahead = (sk > score) | ((sk == score) & (blk > k))
        rank = rank + jnp.where(ahead, 1.0, 0.0)
    return jnp.where(valid & (rank < N_SEL), 1.0, 0.0)


def _online_update(state, s, mask, v, v_keys_on_lanes=False):
    m_run, l_run, acc = state
    s = jnp.where(mask, s, NEG_INF)
    m_new = jnp.maximum(m_run, jnp.max(s, axis=-1, keepdims=True))
    m_safe = jnp.where(m_new == NEG_INF, 0.0, m_new)
    p = jnp.where(mask, jnp.exp(s - m_safe), 0.0)
    alpha = jnp.exp(m_run - m_safe)
    pv = _dot_nt(p.astype(BF16), v) if v_keys_on_lanes else _dot(p.astype(BF16), v)
    return (m_new, alpha * l_run + jnp.sum(p, axis=-1, keepdims=True), alpha * acc + pv)


def _online_init(rows, width):
    return (jnp.full((rows, 1), NEG_INF, F32), jnp.zeros((rows, 1), F32), jnp.zeros((rows, width), F32))


def _online_finish(state):
    _, l_run, acc = state
    return acc / jnp.maximum(l_run, 1e-30)


def _nsa_prompt_kernel(q_ref, gate_ref, kcvc_ref, selkv_ref, winkv_ref, ovl_ref, expand_ref, o_ref, *, tq, n_sel):
    qi = pl.program_id(1)
    q0 = qi * tq
    nc = kcvc_ref.shape[1]
    lo_mask, hi_mask = _lane_half_masks((tq, LANES))
    qpos1 = q0 + _iota((tq, 1), 0)
    qpos4 = jnp.concatenate([qpos1] * 4, axis=0)
    col4 = _iota((4 * tq, tq), 1)
    gate = gate_ref[0]
    lane_g = _iota((tq, LANES), 1)
    tile4 = lambda a: jnp.concatenate([a] * 4, axis=0)

    for g in range(4):
        p, gh = g // 2, g % 2
        keep = hi_mask if gh else lo_mask
        ksl = slice(p * LANES, (p + 1) * LANES)
        vsl = slice(256 + p * LANES, 256 + (p + 1) * LANES)
        qs = []
        for r in range(4):
            h = g * 4 + r
            qh = q_ref[0, :, (h // 2) * LANES:(h // 2 + 1) * LANES].astype(F32)
            if h % 2 != gh:
                qh = pltpu.roll(qh, HEAD_DIM, 1)
            qs.append(jnp.where(keep, qh, 0.0).astype(BF16))
        qst = jnp.concatenate(qs, axis=0)

        kc = kcvc_ref[0, :, ksl].astype(BF16)
        vc = kcvc_ref[0, :, vsl].astype(BF16)
        ment = _iota((4 * tq, nc), 1)
        cmask = (ment >= 1) & (ment * CMP_STRIDE + (L_CMP - CMP_STRIDE - 1) <= qpos4)
        p_c = _masked_softmax(_dot_nt(qst, kc), cmask)
        o_c = _dot(p_c.astype(BF16), vc)
        psum = p_c[0:tq] + p_c[tq:2 * tq] + p_c[2 * tq:3 * tq] + p_c[3 * tq:4 * tq]
        sel = lax.cond(q0 + tq <= N_SEL * L_SEL,
                       lambda ps: jnp.where(_iota((tq, LANES), 1) * L_SEL <= qpos1, 1.0, 0.0),
                       lambda ps: _select_blocks(_dot(ps.astype(BF16), ovl_ref[...]), qpos1, n_sel),
                       psum).astype(BF16)

        def sel_step(kb, state, diag):
            rows = pl.ds(pl.multiple_of(kb * tq, tq), tq)
            chosen = tile4(_dot(sel, expand_ref[kb])) > 0.5
            if diag:
                chosen = chosen & (q0 + col4 <= qpos4)
            return _online_update(state, _dot_nt(qst, selkv_ref[0, rows, ksl]), chosen, selkv_ref[0, rows, vsl])

        st = lax.fori_loop(0, qi, lambda kb, c: sel_step(kb, c, False), _online_init(4 * tq, LANES))
        o_s = _online_finish(sel_step(qi, st, True))

        def win_step(kb, state):
            rows = pl.ds(pl.multiple_of(kb * tq, tq), tq)
            dist = qpos4 - (kb * tq + col4)
            inside = (dist >= 0) & (dist < WINDOW)
            return _online_update(state, _dot_nt(qst, winkv_ref[0, rows, ksl]), inside, winkv_ref[0, rows, vsl])

        kb_lo = jnp.maximum(qi - (WINDOW + tq - 1) // tq, 0)
        o_w = _online_finish(lax.fori_loop(kb_lo, qi + 1, win_step, _online_init(4 * tq, LANES)))

        placed = []
        for r in range(4):
            h = g * 4 + r
            rs = slice(r * tq, (r + 1) * tq)
            gsel = lambda c: jnp.sum(jnp.where(lane_g == c, gate, 0.0), axis=-1, keepdims=True)
            oh = o_c[rs] * gsel(3 * h) + o_s[rs] * gsel(3 * h + 1) + o_w[rs] * gsel(3 * h + 2)
            if h % 2 != gh:
                oh = pltpu.roll(oh, HEAD_DIM, 1)
            placed.append(oh)
        for c in range(2):
            o_ref[0, :, (g * 2 + c) * LANES:(g * 2 + c + 1) * LANES] = jnp.where(lo_mask, placed[2 * c], placed[2 * c + 1])


def _nsa_prompt(q, gate, kcvc, selkv, winkv, b, s, tq):
    n_sel = -(-s // L_SEL)
    nc = kcvc.shape[1]
    ovl = _overlap_table(nc, n_sel, LANES)
    kpos = np.arange(s).reshape(s // tq, 1, tq)
    expand = jnp.asarray((kpos // L_SEL == np.arange(LANES).reshape(1, LANES, 1)).astype(np.float32), dtype=BF16)
    r3 = lambda a: a.reshape(b, s, a.shape[-1])
    tile = lambda w: pl.BlockSpec((1, tq, w), lambda bi, qi: (bi, qi, 0))
    full = lambda n, w: pl.BlockSpec((1, n, w), lambda bi, qi: (bi, 0, 0))
    out = pl.pallas_call(
        functools.partial(_nsa_prompt_kernel, tq=tq, n_sel=n_sel),
        grid=(b, s // tq),
        in_specs=[tile(1024), tile(LANES), full(nc, 512), full(s, 512), full(s, 512),
                  pl.BlockSpec(ovl.shape, lambda bi, qi: (0, 0)), pl.BlockSpec(expand.shape, lambda bi, qi: (0, 0, 0))],
        out_specs=tile(1024),
        out_shape=jax.ShapeDtypeStruct((b, s, 1024), F32),
        compiler_params=_cparams(("arbitrary", "arbitrary")),
        name="nsa_prompt",
    )(r3(q), r3(gate), kcvc, r3(selkv), r3(winkv), ovl, expand)
    return out.reshape(b * s, 1024)


def _nsa_decode_kernel(pt_ref, q_ref, gate_ref, kcvc_ref, ovl_ref, selnew_ref, winnew_ref, winnewf_ref, state_ref,
                       *rest, pages_per_step, n_sel, past_len):
    pp = pages_per_step
    kpages = rest[:pp]
    vpages = rest[pp:2 * pp]
    o_ref, winout_ref = rest[2 * pp:2 * pp + 2]
    s_ref, qbig_ref, sel_ref, oc_ref, m_ref, l_ref, acc_ref = rest[2 * pp + 2:]
    sweep = pl.program_id(1)
    j = pl.program_id(2)
    page = kpages[0].shape[-1]
    head_of_lane = _iota((16, 256), 1) // HEAD_DIM
    row16 = _iota((16, 256), 0)
    own = head_of_lane == row16 // 4
    spread = jnp.where((_iota((16, 16), 0) // 4) * 4 == _iota((16, 16), 1), 1.0, 0.0).astype(BF16)

    @pl.when((sweep == 0) & (j == 0))
    def _():
        tile_lanes = jnp.where(_iota((HEAD_DIM, 256), 0) == _iota((HEAD_DIM, 256), 1) % HEAD_DIM, 1.0, 0.0)
        qb = _dot(q_ref[0], tile_lanes.astype(BF16))
        qbig = jnp.where(own, qb, 0.0).astype(BF16)
        qbig_ref[...] = qbig
        nc = kcvc_ref.shape[1]
        ment = _iota((16, nc), 1)
        cmask = (ment >= 1) & (ment * CMP_STRIDE + (L_CMP - CMP_STRIDE - 1) <= past_len)
        p_c = _masked_softmax(_dot_nt(qbig, kcvc_ref[0, :, 0:256].astype(BF16)), cmask)
        oc_ref[...] = _dot(p_c.astype(BF16), kcvc_ref[0, :, 256:512].astype(BF16))
        pair = p_c + pltpu.roll(p_c, 15, 0)
        psum = pair + pltpu.roll(pair, 14, 0)
        imp = _dot(psum.astype(BF16), ovl_ref[...])
        sel_ref[...] = _select_blocks(imp, jnp.full((16, 1), past_len, jnp.int32), n_sel)
        m_ref[...] = jnp.full(m_ref.shape, NEG_INF, F32)

    qbig = qbig_ref[...]
    qf = qbig.astype(F32)
    sel = sel_ref[...].astype(BF16)
    blocks_per_page = page // L_SEL
    knew = selnew_ref[0]
    s_new = jnp.sum(qf * knew[:, 0:256].astype(F32), axis=-1, keepdims=True)

    def new_token_chosen():
        last = jnp.where(_iota((256, LANES), 0) == past_len // L_SEL, 1.0, 0.0).astype(BF16)
        return _dot(spread, _dot(sel, last).astype(BF16))[:, 0:1] > 0.5

    @pl.when(sweep == 0)
    def _():
        sel_heads = _dot(spread, sel)
        blk_lane = _iota((16, 256), 1)
        blk_in_page = _iota((16, page), 1) // L_SEL
        tops = []
        for i in range(pp):
            pg = j * pp + i
            chosen = jnp.zeros((16, page), F32)
            for c in range(blocks_per_page):
                col = jnp.sum(jnp.where(blk_lane == pg * blocks_per_page + c, sel_heads, 0.0), axis=-1, keepdims=True)
                chosen = jnp.where(blk_in_page == c, col, chosen)
            s = jnp.where(chosen > 0.5, _dot(qbig, kpages[i][0, 0].astype(BF16)), NEG_INF)
            s_ref[pg] = s
            tops.append(jnp.max(s, axis=-1, keepdims=True))
        m_ref[...] = jnp.maximum(m_ref[...], _tree(jnp.maximum, tops))

    @pl.when((sweep == 1) & (j == 0))
    def _():
        has_new = new_token_chosen()
        m = jnp.maximum(m_ref[...], jnp.where(has_new, s_new, NEG_INF))
        m = jnp.where(m == NEG_INF, 0.0, m)
        m_ref[...] = m
        e_new = jnp.where(has_new, jnp.exp(s_new - m), 0.0)
        total = lax.fori_loop(0, s_ref.shape[0],
                              lambda g, t: t + jnp.sum(jnp.exp(s_ref[g] - m), axis=-1, keepdims=True), e_new)
        l_ref[...] = jnp.maximum(total, 1e-30)
        acc_ref[...] = (e_new / l_ref[...]).astype(BF16).astype(F32) * knew[:, 256:512].astype(F32)

    @pl.when(sweep == 1)
    def _():
        m, l = m_ref[...], l_ref[...]
        acc_ref[...] += _tree(jnp.add, [_dot_nt((jnp.exp(s_ref[j * pp + i] - m) / l).astype(BF16),
                                                vpages[i][0, 0].astype(BF16)) for i in range(pp)])

    @pl.when((sweep == 1) & (j == pl.num_programs(2) - 1))
    def _():
        o_s = acc_ref[...]

        w_buf = state_ref.shape[-1]
        s_w = _dot(qbig, state_ref[0, 0].astype(BF16))
        wpos = past_len - w_buf + _iota(s_w.shape, 1)
        wmask = (past_len - wpos < WINDOW) & (wpos >= 0)
        wnew = winnew_ref[0]
        s_wn = jnp.sum(qf * wnew[:, 0:256].astype(F32), axis=-1, keepdims=True)
        mw = jnp.maximum(jnp.max(jnp.where(wmask, s_w, NEG_INF), axis=-1, keepdims=True), s_wn)
        e = jnp.where(wmask, jnp.exp(s_w - mw), 0.0)
        en = jnp.exp(s_wn - mw)
        denom = jnp.maximum(jnp.sum(e, axis=-1, keepdims=True) + en, 1e-30)
        o_w = (_dot_nt((e / denom).astype(BF16), state_ref[0, 1].astype(BF16))
               + (en / denom).astype(BF16).astype(F32) * wnew[:, 256:512].astype(F32))
        newest = _iota((256, w_buf), 1) == w_buf - 1
        for i in range(2):
            winout_ref[0, i] = jnp.where(newest, winnewf_ref[0, i], pltpu.roll(state_ref[0, i], w_buf - 1, 1))

        gate = jnp.broadcast_to(gate_ref[0], (16, LANES))
        lane = _iota((16, LANES), 1)
        hrow = _iota((16, LANES), 0)
        gsel = lambda br: jnp.sum(jnp.where(lane == 3 * hrow + br, gate, 0.0), axis=-1, keepdims=True)
        o = oc_ref[...] * gsel(0) + o_s * gsel(1) + o_w * gsel(2)
        fold = jnp.where(_iota((256, HEAD_DIM), 0) % HEAD_DIM == _iota((256, HEAD_DIM), 1), 1.0, 0.0).astype(BF16)
        o_ref[0] = _split3_dot(jnp.where(own, o, 0.0), fold)


def _nsa_decode(page_table, q, gate, kcvc, selkv_new, winkv_new, win_new, state, cache_sel, pages_per_step):
    nb, n_pages = page_table.shape
    n_pool, page = cache_sel.shape[:2]
    past_len = n_pages * page
    n_sel = -(-(past_len + 1) // L_SEL)
    assert n_sel <= 256
    pp = pages_per_step
    nc = kcvc.shape[1]
    w_buf = state.shape[1]
    ovl = _overlap_table(nc, n_sel, 256)
    keys_last = lambda a: jnp.transpose(a, (0, 2, 3, 4, 1)).reshape(a.shape[0], 2, 256, a.shape[1])
    cs = keys_last(cache_sel)
    n_steps = n_pages // pp
    per_b = lambda b, sw, j, pt: (b, 0, 0)
    per_b4 = lambda b, sw, j, pt: (b, 0, 0, 0)
    kspecs = [pl.BlockSpec((1, 1, 256, page), functools.partial(
        lambda b, sw, j, pt, i: (pt[b, jnp.where(sw == 0, j, n_steps - 1) * pp + i], 0, 0, 0), i=i)) for i in range(pp)]
    vspecs = [pl.BlockSpec((1, 1, 256, page), functools.partial(
        lambda b, sw, j, pt, i: (pt[b, jnp.where(sw == 1, j, 0) * pp + i], 1, 0, 0), i=i)) for i in range(pp)]
    grid_spec = pltpu.PrefetchScalarGridSpec(
        num_scalar_prefetch=1,
        grid=(nb, 2, n_steps),
        in_specs=[pl.BlockSpec((1, 16, HEAD_DIM), per_b), pl.BlockSpec((1, 1, LANES), per_b),
                  pl.BlockSpec((1, nc, 512), per_b), pl.BlockSpec(ovl.shape, lambda b, sw, j, pt: (0, 0)),
                  pl.BlockSpec((1, 1, 512), per_b), pl.BlockSpec((1, 1, 512), per_b), pl.BlockSpec((1, 2, 256, 1), per_b4),
                  pl.BlockSpec((1, 2, 256, w_buf), per_b4)] + kspecs + vspecs,
        out_specs=[pl.BlockSpec((1, 16, HEAD_DIM), per_b), pl.BlockSpec((1, 2, 256, w_buf), per_b4)],
        scratch_shapes=[pltpu.VMEM((n_pages, 16, page), F32),
                        pltpu.VMEM((16, 256), BF16), pltpu.VMEM((16, 256), F32), pltpu.VMEM((16, 256), F32),
                        pltpu.VMEM((16, 1), F32), pltpu.VMEM((16, 1), F32), pltpu.VMEM((16, 256), F32)],
    )
    o, win_out = pl.pallas_call(
        functools.partial(_nsa_decode_kernel, pages_per_step=pp, n_sel=n_sel, past_len=past_len),
        grid_spec=grid_spec,
        out_shape=[jax.ShapeDtypeStruct((nb, 16, HEAD_DIM), F32), jax.ShapeDtypeStruct((nb, 2, 256, w_buf), F32)],
        compiler_params=_cparams(("arbitrary", "arbitrary", "arbitrary")),
        name="nsa_decode",
    )(page_table, q.reshape(nb, 16, HEAD_DIM), gate.reshape(nb, 1, LANES), kcvc, ovl, selkv_new.reshape(nb, 1, 512),
      winkv_new.reshape(nb, 1, 512), win_new.reshape(nb, 2, 256, 1), keys_last(state), *([cs] * (2 * pp)))
    win_rows = jnp.transpose(win_out.reshape(nb, 2, 4, HEAD_DIM, w_buf), (0, 4, 1, 2, 3))
    return o.reshape(nb, 1024), win_rows


def _outproj_router_kernel(x_ref, o_ref, wo_ref, ng_ref, wr_ref, x1_ref, h_ref, comb_ref):
    x1 = x_ref[...] + _dot(o_ref[...].astype(BF16), wo_ref[...])
    x1_ref[...] = x1
    hf = _rms(x1, ng_ref[...])
    h = hf.astype(BF16)
    h_ref[...] = h
    logits = _dot(h, wr_ref[...])
    lane = _iota(logits.shape, 1)
    logits = jnp.where(lane < N_EXPERTS, logits, NEG_INF)
    m1 = jnp.max(logits, axis=-1, keepdims=True)
    i1 = jnp.min(jnp.where(logits == m1, lane, LANES), axis=-1, keepdims=True)
    rest = jnp.where(lane == i1, NEG_INF, logits)
    m2 = jnp.max(rest, axis=-1, keepdims=True)
    i2 = jnp.min(jnp.where(rest == m2, lane, LANES), axis=-1, keepdims=True)
    e2 = jnp.exp(m2 - m1)
    comb_ref[...] = jnp.where(lane == i1, 1.0 / (1.0 + e2), 0.0) + jnp.where(lane == i2, e2 / (1.0 + e2), 0.0)


def _outproj_router(x, o, w_out, norm_g, w_router, tm):
    t, d = x.shape
    row = lambda i: (i, 0)
    fixed = lambda i: (0, 0)
    wr = jnp.pad(w_router, ((0, 0), (0, LANES - w_router.shape[1]))).astype(BF16)
    outs = [jax.ShapeDtypeStruct((t, d), F32), jax.ShapeDtypeStruct((t, d), BF16), jax.ShapeDtypeStruct((t, LANES), F32)]
    return pl.pallas_call(
        _outproj_router_kernel,
        grid=(t // tm,),
        in_specs=[pl.BlockSpec((tm, d), row), pl.BlockSpec((tm, o.shape[1]), row), pl.BlockSpec(w_out.shape, fixed),
                  pl.BlockSpec((1, d), fixed), pl.BlockSpec(wr.shape, fixed)],
        out_specs=[pl.BlockSpec((tm, a.shape[1]), row) for a in outs],
        out_shape=outs,
        compiler_params=_cparams(("arbitrary",)),
        name="outproj_router",
    )(x, o, w_out.astype(BF16), norm_g.reshape(1, d), wr)


def _moe_kernel(x1_ref, h_ref, comb_ref, wg_ref, wu_ref, wd_ref, y_ref, acc_ref):
    e = pl.program_id(1)
    j = pl.program_id(2)

    @pl.when((e == 0) & (j == 0))
    def _():
        acc_ref[...] = jnp.zeros_like(acc_ref)

    h = h_ref[...]
    g = _dot(h, wg_ref[0])
    u = _dot(h, wu_ref[0])
    a = (g * jax.nn.sigmoid(g) * u).astype(BF16)
    comb = comb_ref[...]
    w_e = jnp.sum(jnp.where(_iota(comb.shape, 1) == e, comb, 0.0), axis=-1, keepdims=True)
    acc_ref[...] += w_e * _dot(a, wd_ref[0])

    @pl.when((e == pl.num_programs(1) - 1) & (j == pl.num_programs(2) - 1))
    def _():
        y_ref[...] = x1_ref[...] + acc_ref[...]


def _moe(x1, h, comb, w_gate_up, w_down, tm, tf):
    t, d = x1.shape
    n_e, ff = w_down.shape[:2]
    nj = ff // tf
    row = lambda i, e, j: (i, 0)
    wgu = w_gate_up.astype(BF16)
    return pl.pallas_call(
        _moe_kernel,
        grid=(t // tm, n_e, nj),
        in_specs=[pl.BlockSpec((tm, d), row), pl.BlockSpec((tm, d), row), pl.BlockSpec((tm, LANES), row),
                  pl.BlockSpec((1, d, tf), lambda i, e, j: (e, 0, j)),
                  pl.BlockSpec((1, d, tf), lambda i, e, j: (e, 0, nj + j)),
                  pl.BlockSpec((1, tf, d), lambda i, e, j: (e, j, 0))],
        out_specs=pl.BlockSpec((tm, d), row),
        out_shape=jax.ShapeDtypeStruct((t, d), F32),
        scratch_shapes=[pltpu.VMEM((tm, d), F32)],
        compiler_params=_cparams(("arbitrary", "arbitrary", "arbitrary")),
        name="moe",
    )(x1, h, comb, wgu, wgu, w_down.astype(BF16))


def kernel(x_prompt, x_sample, cache_l0_diff, cache_l0_sb, cache_l1_cmp, cache_l1_sel, state_l1_win, page_table, l0_norm_mix, l0_w_in, l0_diff_q_norm, l0_diff_k_norm, l0_diff_lambda, l0_diff_subln, l0_w_out, l0_norm_ffn, l0_ffn_w_gate_up, l0_ffn_w_down, l1_norm_mix, l1_w_in, l1_q_norm, l1_k_norm, l1_cmp_pos, l1_cmp_w1, l1_cmp_w2, l1_w_out, l1_norm_ffn, l1_router, l1_moe_w_gate_up, l1_moe_w_down):
    b, s, d = x_prompt.shape
    nb = x_sample.shape[0]
    n_pages, page = page_table.shape[1], cache_l0_diff.shape[1]
    past_len = n_pages * page
    tm = min(512, s)
    tq = min(256, s)
    dec_pages = min(16, n_pages)
    cmp_pages = min(32, n_pages)
    ff_tile = l0_ffn_w_down.shape[0] // 2
    moe_tile = l1_moe_w_down.shape[1] // 7
    tabs_p = _rope_tables(jnp.arange(s, dtype=jnp.int32))
    tabs_s = _rope_tables(jnp.full((nb,), past_len, jnp.int32))
    xp = x_prompt.reshape(b * s, d)
    xs = x_sample.reshape(nb, d)

    dq, diff_p, sq, sb_p, dkv, skv = _proj0(xp, l0_norm_mix, l0_w_in, l0_diff_q_norm, l0_diff_k_norm, tabs_p, tm)
    o = _attn0_prompt(l0_diff_lambda, l0_diff_subln, dq, sq, dkv, skv, b, s, tq)
    xp = _outproj_ffn(xp, o, l0_w_out, l0_norm_ffn, l0_ffn_w_gate_up, l0_ffn_w_down, tm, ff_tile)
    dq, diff_s, sq, sb_s, dkv, skv = _proj0(xs, l0_norm_mix, l0_w_in, l0_diff_q_norm, l0_diff_k_norm, tabs_s, nb)
    o = _attn0_decode(page_table, l0_diff_lambda, l0_diff_subln, dq, sq, dkv, cache_l0_diff, cache_l0_sb, dec_pages)
    xs = _outproj_ffn(xs, o, l0_w_out, l0_norm_ffn, l0_ffn_w_gate_up, l0_ffn_w_down, nb, ff_tile)

    cw = _compress_weights(l1_cmp_pos, l1_cmp_w1, l1_cmp_w2)
    q, cmp_p, sel_p, win_p, gate, selkv, winkv = _proj1(xp, l1_norm_mix, l1_w_in, l1_q_norm, l1_k_norm, tabs_p, tm)
    kcvc = _compress_prompt(cmp_p, b, s, cw)
    o = _nsa_prompt(q, gate, kcvc, selkv, winkv, b, s, tq)
    x1, h, comb = _outproj_router(xp, o, l1_w_out, l1_norm_ffn, l1_router, tm)
    xp = _moe(x1, h, comb, l1_moe_w_gate_up, l1_moe_w_down, min(1024, b * s), moe_tile)

    q, cmp_s, sel_s, win_s_new, gate, selkv, winkv = _proj1(xs, l1_norm_mix, l1_w_in, l1_q_norm, l1_k_norm, tabs_s, nb)
    kcvc = _compress_decode(cache_l1_cmp, page_table, cw, cmp_pages)
    o, win_s = _nsa_decode(page_table, q, gate, kcvc, selkv, winkv, win_s_new, state_l1_win, cache_l1_sel, dec_pages)
    x1, h, comb = _outproj_router(xs, o, l1_w_out, l1_norm_ffn, l1_router, nb)
    xs = _moe(x1, h, comb, l1_moe_w_gate_up, l1_moe_w_down, nb, moe_tile)

    w_keep = min(WINDOW, s)
    hk = cmp_p.shape[-1] // (2 * HEAD_DIM)
    rows5 = lambda a, n: a.reshape(n, -1, 2, a.shape[-1] // (2 * HEAD_DIM), HEAD_DIM)
    return (xp.reshape(b, s, d), xs.reshape(nb, 1, d),
            diff_p.reshape(b, s, 2, 4, 2 * HEAD_DIM), sb_p.reshape(b, s, 2, 8, HEAD_DIM),
            rows5(cmp_p, b), rows5(sel_p, b), rows5(win_p, b)[:, s - w_keep:],
            diff_s.reshape(nb, 1, 2, 4, 2 * HEAD_DIM), sb_s.reshape(nb, 1, 2, 8, HEAD_DIM),
            rows5(cmp_s, nb), rows5(sel_s, nb), win_s)
```

```python
import functools

import numpy as np
import jax
import jax.numpy as jnp
from jax import lax
from jax.experimental import pallas as pl
from jax.experimental.pallas import tpu as pltpu

F32 = jnp.float32
BF16 = jnp.bfloat16

HEAD_DIM = 64
ROT_DIM = HEAD_DIM // 4
ROPE_THETA = 500000.0
EPS = 1e-6
DIFF_LAMBDA_INIT = 0.2
L_CMP = 32
CMP_STRIDE = 16
L_SEL = 64
N_SEL = 16
WINDOW = 512
FORCE_BONUS = 1.0e4
N_EXPERTS = 8
LANES = 128
VMEM_LIMIT = 56 * 1024 * 1024
NEG_INF = float("-inf")
SB_LOG_WEIGHT_FLOOR = -110.0


def _cparams(sem):
    return pltpu.CompilerParams(dimension_semantics=sem, vmem_limit_bytes=VMEM_LIMIT)


def _dot(a, b):
    return jnp.dot(a, b, preferred_element_type=F32)


def _dot_nt(a, b):
    return lax.dot_general(a, b, (((1,), (1,)), ((), ())), preferred_element_type=F32)


def _split2_dot(x, m):
    hi = x.astype(BF16)
    lo = (x - hi.astype(F32)).astype(BF16)
    return _dot(hi, m) + _dot(lo, m)


def _split3_dot(x, m):
    hi = x.astype(BF16)
    r1 = x - hi.astype(F32)
    mid = r1.astype(BF16)
    lo = (r1 - mid.astype(F32)).astype(BF16)
    return _dot(hi, m) + _dot(mid, m) + _dot(lo, m)


def _tree(op, xs):
    xs = list(xs)
    while len(xs) > 1:
        xs = [op(a, b) for a, b in zip(xs[0::2], xs[1::2])] + ([xs[-1]] if len(xs) % 2 else [])
    return xs[0]


def _iota(shape, dim):
    return lax.broadcasted_iota(jnp.int32, shape, dim)


def _group_ones(n, group):
    r = _iota((n, n), 0) // group
    c = _iota((n, n), 1) // group
    return jnp.where(r == c, 1.0, 0.0).astype(BF16)


def _rms(x, g):
    return x * lax.rsqrt(jnp.mean(x * x, axis=-1, keepdims=True) + EPS) * g


def _head_norm_rope(xc, g, ones64, c, sa, sb):
    ms = _split2_dot(xc * xc, ones64) * (1.0 / HEAD_DIM)
    xn = xc * lax.rsqrt(ms + EPS) * g
    return xn * c + pltpu.roll(xn, LANES - ROT_DIM // 2, 1) * sa + pltpu.roll(xn, ROT_DIM // 2, 1) * sb


def _rope_tables(pos):
    half = ROT_DIM // 2
    inv = ROPE_THETA ** (-jnp.arange(half, dtype=F32) * 2.0 / ROT_DIM)
    ang = pos.astype(F32)[:, None] * inv[None, :]
    cos, sin = jnp.cos(ang), jnp.sin(ang)
    n = pos.shape[0]
    one = jnp.ones((n, HEAD_DIM - ROT_DIM), F32)
    zero_h = jnp.zeros((n, half), F32)
    zero_r = jnp.zeros((n, HEAD_DIM - ROT_DIM), F32)
    c = jnp.concatenate([cos, cos, one], axis=1)
    sa = jnp.concatenate([-sin, zero_h, zero_r], axis=1)
    sb = jnp.concatenate([zero_h, sin, zero_r], axis=1)
    tile = lambda t: jnp.concatenate([t, t], axis=1)
    return tile(c), tile(sa), tile(sb)


def _proj0_kernel(x_ref, ng_ref, w_ref, qg_ref, kg_ref, c_ref, sa_ref, sb_ref,
                  dq_ref, drows_ref, sq_ref, srows_ref, dkv_ref, skv_ref):
    h = _rms(x_ref[...], ng_ref[...]).astype(BF16)
    ones64 = _group_ones(LANES, HEAD_DIM)
    c, sa, sb = c_ref[...], sa_ref[...], sb_ref[...]
    scale = HEAD_DIM ** -0.5
    y = _dot(h, w_ref[:, 0:512])
    for j in range(4):
        sl = slice(j * LANES, (j + 1) * LANES)
        dq_ref[:, sl] = (_head_norm_rope(y[:, sl], qg_ref[...], ones64, c, sa, sb) * scale).astype(BF16)
    y = _dot(h, w_ref[:, 512:1024])
    for j in range(4):
        sl = slice(j * LANES, (j + 1) * LANES)
        k = _head_norm_rope(y[:, sl], kg_ref[...], ones64, c, sa, sb)
        drows_ref[:, sl] = k
        dkv_ref[:, sl] = k.astype(BF16)
    y = _dot(h, w_ref[:, 1024:1536])
    drows_ref[:, 512:1024] = y
    dkv_ref[:, 512:1024] = y.astype(BF16)
    y = _dot(h, w_ref[:, 1536:2048])
    sq_ref[...] = (y * scale).astype(BF16)
    y = _dot(h, w_ref[:, 2048:3072])
    srows_ref[...] = y
    skv_ref[...] = y.astype(BF16)


def _proj0(x, norm_g, w_in, q_g, k_g, tabs, tm):
    t, d = x.shape
    tab_blocks = tabs[0].shape[0] // tm
    row = lambda i: (i, 0)
    fixed = lambda i: (0, 0)
    tab = lambda i: (i % tab_blocks, 0)
    g2 = lambda g: jnp.tile(g.reshape(1, HEAD_DIM), (1, LANES // HEAD_DIM))
    outs = [jax.ShapeDtypeStruct((t, 512), BF16), jax.ShapeDtypeStruct((t, 1024), F32),
            jax.ShapeDtypeStruct((t, 512), BF16), jax.ShapeDtypeStruct((t, 1024), F32),
            jax.ShapeDtypeStruct((t, 1024), BF16), jax.ShapeDtypeStruct((t, 1024), BF16)]
    return pl.pallas_call(
        _proj0_kernel,
        grid=(t // tm,),
        in_specs=[pl.BlockSpec((tm, d), row), pl.BlockSpec((1, d), fixed),
                  pl.BlockSpec(w_in.shape, fixed),
                  pl.BlockSpec((1, LANES), fixed), pl.BlockSpec((1, LANES), fixed),
                  pl.BlockSpec((tm, LANES), tab), pl.BlockSpec((tm, LANES), tab), pl.BlockSpec((tm, LANES), tab)],
        out_specs=[pl.BlockSpec((tm, o.shape[1]), row) for o in outs],
        out_shape=outs,
        compiler_params=_cparams(("arbitrary",)),
        name="proj0",
    )(x, norm_g.reshape(1, d), w_in.astype(BF16), g2(q_g), g2(k_g), *tabs)


def _lane_half_masks(shape):
    lane = _iota(shape, 1)
    return lane < HEAD_DIM, lane >= HEAD_DIM


def _diff_lambda(lp):
    a = jnp.sum(lp[0:1] * lp[1:2], axis=-1, keepdims=True)
    b = jnp.sum(lp[2:3] * lp[3:4], axis=-1, keepdims=True)
    return jnp.exp(a) - jnp.exp(b) + DIFF_LAMBDA_INIT


def _attn0_kernel(lam_ref, subg_ref, dq_ref, sq_ref, dkv_ref, skv_ref, o_ref, *, tq):
    qi = pl.program_id(1)
    q0 = qi * tq
    lam = _diff_lambda(lam_ref[...])
    lo_mask, hi_mask = _lane_half_masks((tq, LANES))
    row = _iota((tq, tq), 0)
    col = _iota((tq, tq), 1)
    zero_bf = jnp.zeros((tq, LANES), BF16)

    def split_maps(ref):
        out = []
        for c in range(4):
            q = ref[0, :, c * LANES:(c + 1) * LANES]
            out += [jnp.where(lo_mask, q, zero_bf), jnp.where(hi_mask, q, zero_bf)]
        return out

    qd = split_maps(dq_ref)

    def step(kb, carry, diag):
        rows = pl.ds(pl.multiple_of(kb * tq, tq), tq)
        out = []
        for h in range(4):
            k = dkv_ref[0, rows, h * LANES:(h + 1) * LANES]
            v = dkv_ref[0, rows, 512 + h * LANES:512 + (h + 1) * LANES]
            for m in range(2):
                m_run, l_run, acc = carry[2 * h + m]
                s = _dot_nt(qd[2 * h + m], k)
                if diag:
                    s = jnp.where(col <= row, s, NEG_INF)
                m_new = jnp.maximum(m_run, jnp.max(s, axis=-1, keepdims=True))
                p = jnp.exp(s - m_new)
                alpha = jnp.exp(m_run - m_new)
                l_new = alpha * l_run + jnp.sum(p, axis=-1, keepdims=True)
                out.append((m_new, l_new, alpha * acc + _dot(p.astype(BF16), v)))
        return tuple(out)

    init = tuple((jnp.full((tq, 1), NEG_INF, F32), jnp.zeros((tq, 1), F32), jnp.zeros((tq, LANES), F32))
                 for _ in range(8))
    carry = lax.fori_loop(0, qi, lambda kb, c: step(kb, c, False), step(qi, init, True))
    for h in range(4):
        (_, l0, a0), (_, l1, a1) = carry[2 * h], carry[2 * h + 1]
        o = a0 / l0 - lam * (a1 / l1)
        sl = slice(h * LANES, (h + 1) * LANES)
        o_ref[0, :, sl] = _rms(o, subg_ref[:, sl]) * (1.0 - DIFF_LAMBDA_INIT)

    upper = jnp.where(row > col, 1.0, 0.0).astype(BF16)
    qs = split_maps(sq_ref)

    def sb_step(kb, carry, diag):
        rows = pl.ds(pl.multiple_of(kb * tq, tq), tq)
        out = []
        for c in range(4):
            k = skv_ref[0, rows, c * LANES:(c + 1) * LANES]
            v = skv_ref[0, rows, 512 + c * LANES:512 + (c + 1) * LANES]
            for m in range(2):
                tail, acc = carry[2 * c + m]
                z = _dot_nt(qs[2 * c + m], k)
                t = jnp.log1p(jnp.exp(-jnp.abs(z)))
                lsp = -(jnp.maximum(-z, 0.0) + t)
                l1m = -(jnp.maximum(z, 0.0) + t)
                if diag:
                    l1m = jnp.where(col < row, l1m, 0.0)
                w = jnp.exp(lsp + _split2_dot(l1m, upper) + tail)
                if diag:
                    w = jnp.where(col < row, w, 0.0)
                out.append((tail + jnp.sum(l1m, axis=-1, keepdims=True), acc + _dot(w.astype(BF16), v)))
        return tuple(out)

    def still_visible(carry):
        return jnp.max(_tree(jnp.maximum, [t for t, _ in carry])) > SB_LOG_WEIGHT_FLOOR

    def sb_body(c):
        kb, _, carry = c
        carry = sb_step(kb, carry, False)
        return kb - 1, still_visible(carry), carry

    init = tuple((jnp.zeros((tq, 1), F32), jnp.zeros((tq, LANES), F32)) for _ in range(8))
    first = sb_step(qi, init, True)
    _, _, carry = lax.while_loop(lambda c: (c[0] >= 0) & c[1], sb_body, (qi - 1, still_visible(first), first))
    for c in range(4):
        o_ref[0, :, 512 + c * LANES:512 + (c + 1) * LANES] = jnp.where(lo_mask, carry[2 * c][1], carry[2 * c + 1][1])


def _attn0_prompt(lam_p, subln_g, dq, sq, dkv, skv, b, s, tq):
    r3 = lambda a: a.reshape(b, s, a.shape[-1])
    qspec = pl.BlockSpec((1, tq, 512), lambda bi, qi: (bi, qi, 0))
    kvspec = pl.BlockSpec((1, s, 1024), lambda bi, qi: (bi, 0, 0))
    fixed = lambda bi, qi: (0, 0)
    out = pl.pallas_call(
        functools.partial(_attn0_kernel, tq=tq),
        grid=(b, s // tq),
        in_specs=[pl.BlockSpec((4, HEAD_DIM), fixed), pl.BlockSpec((1, 512), fixed), qspec, qspec, kvspec, kvspec],
        out_specs=pl.BlockSpec((1, tq, 1024), lambda bi, qi: (bi, qi, 0)),
        out_shape=jax.ShapeDtypeStruct((b, s, 1024), F32),
        compiler_params=_cparams(("arbitrary", "arbitrary")),
        name="attn0_prompt",
    )(lam_p, subln_g.reshape(1, 512), r3(dq), r3(sq), r3(dkv), r3(skv))
    return out.reshape(b * s, 1024)


def _outproj_ffn_kernel(x_ref, o_ref, wo_ref, ng_ref, wg_ref, wu_ref, wd_ref, y_ref, x1_ref, h_ref, acc_ref):
    j = pl.program_id(1)

    @pl.when(j == 0)
    def _():
        x1 = x_ref[...] + _dot(o_ref[...].astype(BF16), wo_ref[...])
        x1_ref[...] = x1
        h_ref[...] = _rms(x1, ng_ref[...]).astype(BF16)
        acc_ref[...] = jnp.zeros_like(acc_ref)

    h = h_ref[...]
    g = _dot(h, wg_ref[...])
    u = _dot(h, wu_ref[...])
    a = (g * jax.nn.sigmoid(g) * u).astype(BF16)
    acc_ref[...] += _dot(a, wd_ref[...])

    @pl.when(j == pl.num_programs(1) - 1)
    def _():
        y_ref[...] = x1_ref[...] + acc_ref[...]


def _outproj_ffn(x, o, w_out, norm_g, w_gate_up, w_down, tm, tf):
    t, d = x.shape
    ff = w_down.shape[0]
    nj = ff // tf
    row = lambda i, j: (i, 0)
    fixed = lambda i, j: (0, 0)
    wgu = w_gate_up.astype(BF16)
    return pl.pallas_call(
        _outproj_ffn_kernel,
        grid=(t // tm, nj),
        in_specs=[pl.BlockSpec((tm, d), row), pl.BlockSpec((tm, o.shape[1]), row),
                  pl.BlockSpec(w_out.shape, fixed), pl.BlockSpec((1, d), fixed),
                  pl.BlockSpec((d, tf), lambda i, j: (0, j)), pl.BlockSpec((d, tf), lambda i, j: (0, nj + j)),
                  pl.BlockSpec((tf, d), lambda i, j: (j, 0))],
        out_specs=pl.BlockSpec((tm, d), row),
        out_shape=jax.ShapeDtypeStruct((t, d), F32),
        scratch_shapes=[pltpu.VMEM((tm, d), F32), pltpu.VMEM((tm, d), BF16), pltpu.VMEM((tm, d), F32)],
        compiler_params=_cparams(("arbitrary", "arbitrary")),
        name="outproj_ffn",
    )(x, o, w_out.astype(BF16), norm_g.reshape(1, d), wgu, wgu, w_down.astype(BF16))


def _rows_select(x, group, nrows):
    w = x.shape[-1]
    keep = _iota((nrows, w), 1) // group == _iota((nrows, w), 0)
    return jnp.where(keep, jnp.broadcast_to(x.astype(F32), (nrows, w)), 0.0).astype(x.dtype)


def _fold_chunks(x):
    return x[:, 0:LANES] + x[:, LANES:2 * LANES] + x[:, 2 * LANES:3 * LANES] + x[:, 3 * LANES:4 * LANES]


def _attn0_decode_kernel(pt_ref, lam_ref, subg_ref, dq_ref, sq_ref, dnew_ref, *rest, pages_per_step):
    pp = pages_per_step
    dpages = rest[:pp]
    spages = rest[pp:2 * pp]
    od_ref, os_ref = rest[2 * pp:2 * pp + 2]
    s_ref, md_ref, ld_ref, accd_ref, tail_ref, accs_ref = rest[2 * pp + 2:]
    sweep = pl.program_id(1)
    j = pl.program_id(2)
    n_steps = pl.num_programs(2)
    qd = _rows_select(dq_ref[0], HEAD_DIM, 8)
    qd128 = _fold_chunks(qd.astype(F32)).astype(BF16)
    nrow = dpages[0].shape[1]
    page = nrow // 8
    new = dnew_ref[0]
    s_new = jnp.sum(qd.astype(F32) * new[:, 0:512].astype(F32), axis=-1, keepdims=True)
    lam = _diff_lambda(lam_ref[...])
    even_row = _iota((8, 1), 0) % 2 == 0

    def combine(pn):
        return jnp.where(even_row, pn - lam * pltpu.roll(pn, 7, 0), 0.0)

    @pl.when((sweep == 0) & (j == 0))
    def _():
        md_ref[...] = s_new
        tail_ref[...] = jnp.zeros_like(tail_ref)
        accs_ref[...] = jnp.zeros_like(accs_ref)

    @pl.when(sweep == 0)
    def _():
        qs = _rows_select(sq_ref[0], HEAD_DIM, 8)
        upper = jnp.where(_iota((page, page), 0) > _iota((page, page), 1), 1.0, 0.0).astype(BF16)
        is_key_row = _iota((8, nrow), 1) % 8 == _iota((8, nrow), 0) // 2
        tops = []
        for i in range(pp):
            s = jnp.where(is_key_row, _dot_nt(qd128, dpages[i][0].astype(BF16)), NEG_INF)
            s_ref[j * pp + i] = s
            tops.append(jnp.max(s, axis=-1, keepdims=True))
        md_ref[...] = jnp.maximum(md_ref[...], _tree(jnp.maximum, tops))

        z = jnp.concatenate([_dot(qs, spages[i][0, 0].astype(BF16)) for i in range(pp)], axis=0)
        t = jnp.log1p(jnp.exp(-jnp.abs(z)))
        lsp = -(jnp.maximum(-z, 0.0) + t)
        l1m = -(jnp.maximum(z, 0.0) + t)
        within = _split3_dot(l1m, upper)
        totals = jnp.sum(l1m, axis=-1, keepdims=True)
        run = tail_ref[...]
        tails = []
        for i in range(pp):
            tails.append(run)
            run = run + totals[i * 8:(i + 1) * 8]
        tail_ref[...] = run
        w = jnp.exp(lsp + within + jnp.concatenate(tails, axis=0)).astype(BF16)
        accs_ref[...] += _tree(jnp.add, [_dot_nt(w[i * 8:(i + 1) * 8], spages[i][0, 1].astype(BF16))
                                         for i in range(pp)])

    @pl.when((sweep == 1) & (j == 0))
    def _():
        m = md_ref[...]
        total = lax.fori_loop(0, s_ref.shape[0],
                              lambda g, t: t + jnp.sum(jnp.exp(s_ref[g] - m), axis=-1, keepdims=True),
                              jnp.exp(s_new - m))
        ld_ref[...] = jnp.maximum(total, 1e-30)
        own_head = _iota((8, 512), 1) // LANES == _iota((8, 512), 0) // 2
        v_new = _fold_chunks(jnp.where(own_head, jnp.broadcast_to(new[:, 512:1024].astype(F32), (8, 512)), 0.0))
        p_new = combine(jnp.exp(s_new - m) / ld_ref[...])
        accd_ref[...] = p_new.astype(BF16).astype(F32) * v_new

    @pl.when(sweep == 1)
    def _():
        m, l = md_ref[...], ld_ref[...]
        parts = []
        for i in range(pp):
            pc = combine(jnp.exp(s_ref[j * pp + i] - m) / l)
            p_on_v = pltpu.roll(pc, 4, 1).astype(BF16)
            parts.append(_dot(p_on_v, dpages[i][0].astype(BF16)))
        accd_ref[...] += _tree(jnp.add, parts)

    @pl.when((sweep == 1) & (j == n_steps - 1))
    def _():
        od_ref[0] = _rms(accd_ref[...], subg_ref[...]) * (1.0 - DIFF_LAMBDA_INIT)
        acs = accs_ref[...]
        os_ref[0] = jnp.sum(jnp.where(_iota(acs.shape, 0) == _iota(acs.shape, 1) // HEAD_DIM, acs, 0.0), axis=0,
                            keepdims=True)


def _attn0_decode(page_table, lam_p, subln_g, dq, sq, dkv, cache_diff, cache_sb, pages_per_step):
    nb, n_pages = page_table.shape
    n_pool, page = cache_diff.shape[:2]
    pp = pages_per_step
    cd = cache_diff.reshape(n_pool, page * 8, LANES)
    cs = jnp.transpose(cache_sb, (0, 2, 3, 4, 1)).reshape(n_pool, 2, 512, page)
    n_steps = n_pages // pp
    fixed = lambda b, sw, j, pt: (0, 0)
    per_b = lambda b, sw, j, pt: (b, 0, 0)
    newest_first = lambda i: (lambda b, j, pt: pt[b, n_pages - 1 - (j * pp + i)])

    def dspec(i):
        pick = newest_first(i)
        return pl.BlockSpec((1, page * 8, LANES), lambda b, sw, j, pt: (pick(b, j, pt), 0, 0))

    def sspec(i):
        pick = newest_first(i)
        return pl.BlockSpec((1, 2, 512, page),
                            lambda b, sw, j, pt: (pick(b, jnp.where(sw == 0, j, n_steps - 1), pt), 0, 0, 0))

    grid_spec = pltpu.PrefetchScalarGridSpec(
        num_scalar_prefetch=1,
        grid=(nb, 2, n_steps),
        in_specs=[pl.BlockSpec((4, HEAD_DIM), fixed), pl.BlockSpec((8, LANES), fixed),
                  pl.BlockSpec((1, 1, 512), per_b), pl.BlockSpec((1, 1, 512), per_b),
                  pl.BlockSpec((1, 1, 1024), per_b)]
                 + [dspec(i) for i in range(pp)] + [sspec(i) for i in range(pp)],
        out_specs=[pl.BlockSpec((1, 8, LANES), per_b), pl.BlockSpec((1, 1, 512), per_b)],
        scratch_shapes=[pltpu.VMEM((n_pages, 8, page * 8), F32),
                        pltpu.VMEM((8, 1), F32), pltpu.VMEM((8, 1), F32), pltpu.VMEM((8, LANES), F32),
                        pltpu.VMEM((8, 1), F32), pltpu.VMEM((8, 512), F32)],
    )
    od, osb = pl.pallas_call(
        functools.partial(_attn0_decode_kernel, pages_per_step=pp),
        grid_spec=grid_spec,
        out_shape=[jax.ShapeDtypeStruct((nb, 8, LANES), F32), jax.ShapeDtypeStruct((nb, 1, 512), F32)],
        compiler_params=_cparams(("arbitrary", "arbitrary", "arbitrary")),
        name="attn0_decode",
    )(page_table, lam_p, jnp.repeat(subln_g, 2, axis=0), dq.reshape(nb, 1, 512), sq.reshape(nb, 1, 512),
      dkv.reshape(nb, 1, 1024), *([cd] * pp), *([cs] * pp))
    return jnp.concatenate([od[:, 0::2, :].reshape(nb, 512), osb.reshape(nb, 512)], axis=-1)


def _proj1_kernel(x_ref, ng_ref, w_ref, qg_ref, kg_ref, c_ref, sa_ref, sb_ref,
                  q_ref, cmp_ref, sel_ref, win_ref, gate_ref, selkv_ref, winkv_ref):
    h = _rms(x_ref[...], ng_ref[...]).astype(BF16)
    ones64 = _group_ones(LANES, HEAD_DIM)
    c, sa, sb = c_ref[...], sa_ref[...], sb_ref[...]
    scale = HEAD_DIM ** -0.5
    for half in range(2):
        y = _dot(h, w_ref[:, half * 512:(half + 1) * 512])
        for j in range(4):
            sl = slice(j * LANES, (j + 1) * LANES)
            q = _head_norm_rope(y[:, sl], qg_ref[...], ones64, c, sa, sb) * scale
            q_ref[:, half * 512 + j * LANES:half * 512 + (j + 1) * LANES] = q.astype(BF16)
    for i, (rows_ref, bf_ref) in enumerate(((cmp_ref, None), (sel_ref, selkv_ref), (win_ref, winkv_ref))):
        y = _dot(h, w_ref[:, 1024 + i * 512:1024 + (i + 1) * 512])
        for j in range(2):
            sl = slice(j * LANES, (j + 1) * LANES)
            k = _head_norm_rope(y[:, sl], kg_ref[i:i + 1, :], ones64, c, sa, sb)
            rows_ref[:, sl] = k
            if bf_ref is not None:
                bf_ref[:, sl] = k.astype(BF16)
        rows_ref[:, 256:512] = y[:, 256:512]
        if bf_ref is not None:
            bf_ref[:, 256:512] = y[:, 256:512].astype(BF16)
    gate_ref[...] = jax.nn.sigmoid(_dot(h, w_ref[:, 2560:2688]))


def _proj1(x, norm_g, w_in, q_g, k_g, tabs, tm):
    t, d = x.shape
    tab_blocks = tabs[0].shape[0] // tm
    row = lambda i: (i, 0)
    fixed = lambda i: (0, 0)
    tab = lambda i: (i % tab_blocks, 0)
    w = jnp.pad(w_in, ((0, 0), (0, 2688 - w_in.shape[1]))).astype(BF16)
    rep = LANES // HEAD_DIM
    outs = [jax.ShapeDtypeStruct((t, 1024), BF16), jax.ShapeDtypeStruct((t, 512), F32),
            jax.ShapeDtypeStruct((t, 512), F32), jax.ShapeDtypeStruct((t, 512), F32),
            jax.ShapeDtypeStruct((t, LANES), F32), jax.ShapeDtypeStruct((t, 512), BF16),
            jax.ShapeDtypeStruct((t, 512), BF16)]
    return pl.pallas_call(
        _proj1_kernel,
        grid=(t // tm,),
        in_specs=[pl.BlockSpec((tm, d), row), pl.BlockSpec((1, d), fixed), pl.BlockSpec(w.shape, fixed),
                  pl.BlockSpec((1, LANES), fixed), pl.BlockSpec((3, LANES), fixed),
                  pl.BlockSpec((tm, LANES), tab), pl.BlockSpec((tm, LANES), tab), pl.BlockSpec((tm, LANES), tab)],
        out_specs=[pl.BlockSpec((tm, o.shape[1]), row) for o in outs],
        out_shape=outs,
        compiler_params=_cparams(("arbitrary",)),
        name="proj1",
    )(x, norm_g.reshape(1, d), w, jnp.tile(q_g.reshape(1, HEAD_DIM), (1, rep)), jnp.tile(k_g, (1, rep)), *tabs)


def _gelu_tanh(x):
    return 0.5 * x * (1.0 + jnp.tanh(0.7978845608028654 * (x + 0.044715 * x * x * x)))


def _compress_kernel(*refs, n_in, n_prefetch):
    refs = refs[n_prefetch:]
    row_refs = refs[:n_in]
    posv_ref, w1_ref, w2_ref, out_ref, carry_ref = refs[n_in:]
    t = pl.program_id(1)
    paged = len(row_refs[0].shape) == 5
    chunks = row_refs[0].shape[3] // CMP_STRIDE if paged else row_refs[0].shape[1]
    m_rows = chunks * n_in
    first = _iota((m_rows, 1), 0) == 0
    for i in range(2):
        for p in range(2):
            cols = []
            for l in range(CMP_STRIDE):
                c0 = l * 512 + i * 256 + p * LANES
                if paged:
                    pieces = [r[0, i, p, pl.ds(l, chunks, stride=CMP_STRIDE), :] for r in row_refs]
                else:
                    pieces = [r[0, :, c0:c0 + LANES] for r in row_refs]
                cols.append(pieces[0] if n_in == 1 else jnp.concatenate(pieces, axis=0))
            xcat = jnp.concatenate(cols, axis=1)
            a = _dot((xcat + posv_ref[i, 0]).astype(BF16), w1_ref[i, 0])
            b = _dot((xcat + posv_ref[i, 1]).astype(BF16), w1_ref[i, 1])
            prev = jnp.where(t == 0, jnp.zeros((1, 512), F32), carry_ref[2 * i + p, 0:1, :])
            a_prev = jnp.where(first, prev, pltpu.roll(a, 1, 0))
            carry_ref[2 * i + p, 0:1, :] = a[m_rows - 1:m_rows, :]
            hid = _gelu_tanh(a_prev + b)
            out = _dot(hid.astype(BF16), w2_ref[i])
            out = jnp.where(first & (t == 0), 0.0, out)
            out_ref[0, :, i * 256 + p * LANES:i * 256 + (p + 1) * LANES] = out


def _compress_weights(cmp_pos, w1, w2):
    w1r = w1.reshape(2, 2, CMP_STRIDE, HEAD_DIM, w1.shape[-1])
    hid = w1.shape[-1]
    z = jnp.zeros_like(w1r)
    top = jnp.concatenate([w1r, z], axis=-1)
    bot = jnp.concatenate([z, w1r], axis=-1)
    w1bd = jnp.concatenate([top, bot], axis=3).reshape(2, 2, CMP_STRIDE * LANES, 2 * hid).astype(BF16)
    z2 = jnp.zeros_like(w2)
    w2bd = jnp.concatenate([jnp.concatenate([w2, z2], axis=-1), jnp.concatenate([z2, w2], axis=-1)],
                           axis=1).astype(BF16)
    pv = cmp_pos.reshape(2, CMP_STRIDE, 2, HEAD_DIM)
    pv = jnp.transpose(pv, (2, 0, 1, 3))
    posv = jnp.concatenate([pv, pv], axis=-1).reshape(2, 2, 1, CMP_STRIDE * LANES)
    return posv, w1bd, w2bd


def _compress_call(row_arrays, row_specs, grid, out_map, nb, n_chunks, rows_per_step, weights, prefetch=None):
    posv, w1bd, w2bd = weights
    n_in = len(row_arrays)
    nidx = 2 + (1 if prefetch is not None else 0)
    fix = lambda nd: (lambda *a: (0,) * nd)
    in_specs = list(row_specs) + [pl.BlockSpec(posv.shape, fix(4)), pl.BlockSpec(w1bd.shape, fix(4)),
                                  pl.BlockSpec(w2bd.shape, fix(3))]
    out_spec = pl.BlockSpec((1, rows_per_step, 512), out_map)
    scratch = [pltpu.VMEM((4, 8, 512), F32)]
    kern = functools.partial(_compress_kernel, n_in=n_in, n_prefetch=0 if prefetch is None else 1)
    out_shape = jax.ShapeDtypeStruct((nb, n_chunks, 512), F32)
    if prefetch is None:
        return pl.pallas_call(kern, grid=grid, in_specs=in_specs, out_specs=out_spec, out_shape=out_shape,
                              scratch_shapes=scratch, compiler_params=_cparams(("arbitrary", "arbitrary")),
                              name="compress_prompt")(*row_arrays, posv, w1bd, w2bd)
    gs = pltpu.PrefetchScalarGridSpec(num_scalar_prefetch=1, grid=grid, in_specs=in_specs, out_specs=out_spec,
                                      scratch_shapes=scratch)
    return pl.pallas_call(kern, grid_spec=gs, out_shape=out_shape,
                          compiler_params=_cparams(("arbitrary", "arbitrary")),
                          name="compress_decode")(prefetch, *row_arrays, posv, w1bd, w2bd)


def _compress_prompt(cmp_rows, b, s, weights):
    n_chunks = s // CMP_STRIDE
    y = cmp_rows.reshape(b, n_chunks, CMP_STRIDE * 512)
    spec = pl.BlockSpec((1, n_chunks, CMP_STRIDE * 512), lambda bi, t: (bi, 0, 0))
    return _compress_call([y], [spec], (b, 1), lambda bi, t: (bi, 0, 0), b, n_chunks, n_chunks, weights)


def _compress_decode(cache_cmp, page_table, weights, pages_per_step):
    nb, n_pages = page_table.shape
    n_pool, page = cache_cmp.shape[:2]
    cpp = page // CMP_STRIDE
    pp = pages_per_step
    y = jnp.swapaxes(jnp.transpose(cache_cmp, (0, 2, 3, 4, 1)).reshape(n_pool, 2, 2, LANES, page), -1, -2)
    specs = [pl.BlockSpec((1, 2, 2, page, LANES), functools.partial(
        lambda bi, t, pt, i: (pt[bi, t * pp + i], 0, 0, 0, 0), i=i)) for i in range(pp)]
    return _compress_call([y] * pp, specs, (nb, n_pages // pp), lambda bi, t, pt: (bi, t, 0), nb, n_pages * cpp,
                          pp * cpp, weights, prefetch=page_table)


def _overlap_table(n_entries, n_sel, width):
    start = (np.arange(n_entries)[:, None] - 1) * CMP_STRIDE
    j = np.arange(width)[None, :]
    ov = (start < (j + 1) * L_SEL) & (start + L_CMP > j * L_SEL) & (np.arange(n_entries)[:, None] >= 1) & (j < n_sel)
    return jnp.asarray(ov.astype(np.float32), dtype=BF16)


def _masked_softmax(s, mask):
    s = jnp.where(mask, s, NEG_INF)
    m = jnp.max(s, axis=-1, keepdims=True)
    m = jnp.where(m == NEG_INF, 0.0, m)
    e = jnp.where(mask, jnp.exp(s - m), 0.0)
    return e / jnp.maximum(jnp.sum(e, axis=-1, keepdims=True), 1e-30)


def _select_blocks(imp, qpos, n_sel):
    blk = _iota(imp.shape, 1)
    cur = qpos // L_SEL
    valid = blk * L_SEL <= qpos
    forced = (blk == 0) | (blk == cur) | (blk == cur - 1)
    score = jnp.where(valid, imp + jnp.where(forced, FORCE_BONUS, 0.0), NEG_INF)
    rank = jnp.zeros(imp.shape, F32)
    for k in range(n_sel):
        sk = score[:, k:k + 1]
        ahead = (sk > score) | ((sk == score) & (blk > k))
        rank = rank + jnp.where(ahead, 1.0, 0.0)
    return jnp.where(valid & (rank < N_SEL), 1.0, 0.0)


def _online_update(state, s, mask, v, v_keys_on_lanes=False):
    m_run, l_run, acc = state
    s = jnp.where(mask, s, NEG_INF)
    m_new = jnp.maximum(m_run, jnp.max(s, axis=-1, keepdims=True))
    m_safe = jnp.where(m_new == NEG_INF, 0.0, m_new)
    p = jnp.where(mask, jnp.exp(s - m_safe), 0.0)
    alpha = jnp.exp(m_run - m_safe)
    pv = _dot_nt(p.astype(BF16), v) if v_keys_on_lanes else _dot(p.astype(BF16), v)
    return (m_new, alpha * l_run + jnp.sum(p, axis=-1, keepdims=True), alpha * acc + pv)


def _online_init(rows, width):
    return (jnp.full((rows, 1), NEG_INF, F32), jnp.zeros((rows, 1), F32), jnp.zeros((rows, width), F32))


def _online_finish(state):
    _, l_run, acc = state
    return acc / jnp.maximum(l_run, 1e-30)


def _nsa_prompt_kernel(q_ref, gate_ref, kcvc_ref, selkv_ref, winkv_ref, ovl_ref, expand_ref, o_ref, *, tq, n_sel):
    qi = pl.program_id(1)
    q0 = qi * tq
    nc = kcvc_ref.shape[1]
    lo_mask, hi_mask = _lane_half_masks((tq, LANES))
    qpos1 = q0 + _iota((tq, 1), 0)
    qpos4 = jnp.concatenate([qpos1] * 4, axis=0)
    col4 = _iota((4 * tq, tq), 1)
    gate = gate_ref[0]
    lane_g = _iota((tq, LANES), 1)
    tile4 = lambda a: jnp.concatenate([a] * 4, axis=0)

    for g in range(4):
        p, gh = g // 2, g % 2
        keep = hi_mask if gh else lo_mask
        ksl = slice(p * LANES, (p + 1) * LANES)
        vsl = slice(256 + p * LANES, 256 + (p + 1) * LANES)
        qs = []
        for r in range(4):
            h = g * 4 + r
            qh = q_ref[0, :, (h // 2) * LANES:(h // 2 + 1) * LANES].astype(F32)
            if h % 2 != gh:
                qh = pltpu.roll(qh, HEAD_DIM, 1)
            qs.append(jnp.where(keep, qh, 0.0).astype(BF16))
        qst = jnp.concatenate(qs, axis=0)

        kc = kcvc_ref[0, :, ksl].astype(BF16)
        vc = kcvc_ref[0, :, vsl].astype(BF16)
        ment = _iota((4 * tq, nc), 1)
        cmask = (ment >= 1) & (ment * CMP_STRIDE + (L_CMP - CMP_STRIDE - 1) <= qpos4)
        p_c = _masked_softmax(_dot_nt(qst, kc), cmask)
        o_c = _dot(p_c.astype(BF16), vc)
        psum = p_c[0:tq] + p_c[tq:2 * tq] + p_c[2 * tq:3 * tq] + p_c[3 * tq:4 * tq]
        sel = lax.cond(q0 + tq <= N_SEL * L_SEL,
                       lambda ps: jnp.where(_iota((tq, LANES), 1) * L_SEL <= qpos1, 1.0, 0.0),
                       lambda ps: _select_blocks(_dot(ps.astype(BF16), ovl_ref[...]), qpos1, n_sel),
                       psum).astype(BF16)

        def sel_step(kb, state, diag):
            rows = pl.ds(pl.multiple_of(kb * tq, tq), tq)
            chosen = tile4(_dot(sel, expand_ref[kb])) > 0.5
            if diag:
                chosen = chosen & (q0 + col4 <= qpos4)
            return _online_update(state, _dot_nt(qst, selkv_ref[0, rows, ksl]), chosen, selkv_ref[0, rows, vsl])

        st = lax.fori_loop(0, qi, lambda kb, c: sel_step(kb, c, False), _online_init(4 * tq, LANES))
        o_s = _online_finish(sel_step(qi, st, True))

        def win_step(kb, state):
            rows = pl.ds(pl.multiple_of(kb * tq, tq), tq)
            dist = qpos4 - (kb * tq + col4)
            inside = (dist >= 0) & (dist < WINDOW)
            return _online_update(state, _dot_nt(qst, winkv_ref[0, rows, ksl]), inside, winkv_ref[0, rows, vsl])

        kb_lo = jnp.maximum(qi - (WINDOW + tq - 1) // tq, 0)
        o_w = _online_finish(lax.fori_loop(kb_lo, qi + 1, win_step, _online_init(4 * tq, LANES)))

        placed = []
        for r in range(4):
            h = g * 4 + r
            rs = slice(r * tq, (r + 1) * tq)
            gsel = lambda c: jnp.sum(jnp.where(lane_g == c, gate, 0.0), axis=-1, keepdims=True)
            oh = o_c[rs] * gsel(3 * h) + o_s[rs] * gsel(3 * h + 1) + o_w[rs] * gsel(3 * h + 2)
            if h % 2 != gh:
                oh = pltpu.roll(oh, HEAD_DIM, 1)
            placed.append(oh)
        for c in range(2):
            o_ref[0, :, (g * 2 + c) * LANES:(g * 2 + c + 1) * LANES] = jnp.where(lo_mask, placed[2 * c], placed[2 * c + 1])


def _nsa_prompt(q, gate, kcvc, selkv, winkv, b, s, tq):
    n_sel = -(-s // L_SEL)
    nc = kcvc.shape[1]
    ovl = _overlap_table(nc, n_sel, LANES)
    kpos = np.arange(s).reshape(s // tq, 1, tq)
    expand = jnp.asarray((kpos // L_SEL == np.arange(LANES).reshape(1, LANES, 1)).astype(np.float32), dtype=BF16)
    r3 = lambda a: a.reshape(b, s, a.shape[-1])
    tile = lambda w: pl.BlockSpec((1, tq, w), lambda bi, qi: (bi, qi, 0))
    full = lambda n, w: pl.BlockSpec((1, n, w), lambda bi, qi: (bi, 0, 0))
    out = pl.pallas_call(
        functools.partial(_nsa_prompt_kernel, tq=tq, n_sel=n_sel),
        grid=(b, s // tq),
        in_specs=[tile(1024), tile(LANES), full(nc, 512), full(s, 512), full(s, 512),
                  pl.BlockSpec(ovl.shape, lambda bi, qi: (0, 0)), pl.BlockSpec(expand.shape, lambda bi, qi: (0, 0, 0))],
        out_specs=tile(1024),
        out_shape=jax.ShapeDtypeStruct((b, s, 1024), F32),
        compiler_params=_cparams(("arbitrary", "arbitrary")),
        name="nsa_prompt",
    )(r3(q), r3(gate), kcvc, r3(selkv), r3(winkv), ovl, expand)
    return out.reshape(b * s, 1024)


def _nsa_decode_kernel(pt_ref, q_ref, gate_ref, kcvc_ref, ovl_ref, selnew_ref, winnew_ref, winnewf_ref, state_ref,
                       *rest, pages_per_step, n_sel, past_len):
    pp = pages_per_step
    kpages = rest[:pp]
    vpages = rest[pp:2 * pp]
    o_ref, winout_ref = rest[2 * pp:2 * pp + 2]
    s_ref, qbig_ref, sel_ref, oc_ref, m_ref, l_ref, acc_ref = rest[2 * pp + 2:]
    sweep = pl.program_id(1)
    j = pl.program_id(2)
    page = kpages[0].shape[-1]
    head_of_lane = _iota((16, 256), 1) // HEAD_DIM
    row16 = _iota((16, 256), 0)
    own = head_of_lane == row16 // 4
    spread = jnp.where((_iota((16, 16), 0) // 4) * 4 == _iota((16, 16), 1), 1.0, 0.0).astype(BF16)

    @pl.when((sweep == 0) & (j == 0))
    def _():
        tile_lanes = jnp.where(_iota((HEAD_DIM, 256), 0) == _iota((HEAD_DIM, 256), 1) % HEAD_DIM, 1.0, 0.0)
        qb = _dot(q_ref[0], tile_lanes.astype(BF16))
        qbig = jnp.where(own, qb, 0.0).astype(BF16)
        qbig_ref[...] = qbig
        nc = kcvc_ref.shape[1]
        ment = _iota((16, nc), 1)
        cmask = (ment >= 1) & (ment * CMP_STRIDE + (L_CMP - CMP_STRIDE - 1) <= past_len)
        p_c = _masked_softmax(_dot_nt(qbig, kcvc_ref[0, :, 0:256].astype(BF16)), cmask)
        oc_ref[...] = _dot(p_c.astype(BF16), kcvc_ref[0, :, 256:512].astype(BF16))
        pair = p_c + pltpu.roll(p_c, 15, 0)
        psum = pair + pltpu.roll(pair, 14, 0)
        imp = _dot(psum.astype(BF16), ovl_ref[...])
        sel_ref[...] = _select_blocks(imp, jnp.full((16, 1), past_len, jnp.int32), n_sel)
        m_ref[...] = jnp.full(m_ref.shape, NEG_INF, F32)

    qbig = qbig_ref[...]
    qf = qbig.astype(F32)
    sel = sel_ref[...].astype(BF16)
    blocks_per_page = page // L_SEL
    knew = selnew_ref[0]
    s_new = jnp.sum(qf * knew[:, 0:256].astype(F32), axis=-1, keepdims=True)

    def new_token_chosen():
        last = jnp.where(_iota((256, LANES), 0) == past_len // L_SEL, 1.0, 0.0).astype(BF16)
        return _dot(spread, _dot(sel, last).astype(BF16))[:, 0:1] > 0.5

    @pl.when(sweep == 0)
    def _():
        sel_heads = _dot(spread, sel)
        blk_lane = _iota((16, 256), 1)
        blk_in_page = _iota((16, page), 1) // L_SEL
        tops = []
        for i in range(pp):
            pg = j * pp + i
            chosen = jnp.zeros((16, page), F32)
            for c in range(blocks_per_page):
                col = jnp.sum(jnp.where(blk_lane == pg * blocks_per_page + c, sel_heads, 0.0), axis=-1, keepdims=True)
                chosen = jnp.where(blk_in_page == c, col, chosen)
            s = jnp.where(chosen > 0.5, _dot(qbig, kpages[i][0, 0].astype(BF16)), NEG_INF)
            s_ref[pg] = s
            tops.append(jnp.max(s, axis=-1, keepdims=True))
        m_ref[...] = jnp.maximum(m_ref[...], _tree(jnp.maximum, tops))

    @pl.when((sweep == 1) & (j == 0))
    def _():
        has_new = new_token_chosen()
        m = jnp.maximum(m_ref[...], jnp.where(has_new, s_new, NEG_INF))
        m = jnp.where(m == NEG_INF, 0.0, m)
        m_ref[...] = m
        e_new = jnp.where(has_new, jnp.exp(s_new - m), 0.0)
        total = lax.fori_loop(0, s_ref.shape[0],
                              lambda g, t: t + jnp.sum(jnp.exp(s_ref[g] - m), axis=-1, keepdims=True), e_new)
        l_ref[...] = jnp.maximum(total, 1e-30)
        acc_ref[...] = (e_new / l_ref[...]).astype(BF16).astype(F32) * knew[:, 256:512].astype(F32)

    @pl.when(sweep == 1)
    def _():
        m, l = m_ref[...], l_ref[...]
        acc_ref[...] += _tree(jnp.add, [_dot_nt((jnp.exp(s_ref[j * pp + i] - m) / l).astype(BF16),
                                                vpages[i][0, 0].astype(BF16)) for i in range(pp)])

    @pl.when((sweep == 1) & (j == pl.num_programs(2) - 1))
    def _():
        o_s = acc_ref[...]

        w_buf = state_ref.shape[-1]
        s_w = _dot(qbig, state_ref[0, 0].astype(BF16))
        wpos = past_len - w_buf + _iota(s_w.shape, 1)
        wmask = (past_len - wpos < WINDOW) & (wpos >= 0)
        wnew = winnew_ref[0]
        s_wn = jnp.sum(qf * wnew[:, 0:256].astype(F32), axis=-1, keepdims=True)
        mw = jnp.maximum(jnp.max(jnp.where(wmask, s_w, NEG_INF), axis=-1, keepdims=True), s_wn)
        e = jnp.where(wmask, jnp.exp(s_w - mw), 0.0)
        en = jnp.exp(s_wn - mw)
        denom = jnp.maximum(jnp.sum(e, axis=-1, keepdims=True) + en, 1e-30)
        o_w = (_dot_nt((e / denom).astype(BF16), state_ref[0, 1].astype(BF16))
               + (en / denom).astype(BF16).astype(F32) * wnew[:, 256:512].astype(F32))
        newest = _iota((256, w_buf), 1) == w_buf - 1
        for i in range(2):
            winout_ref[0, i] = jnp.where(newest, winnewf_ref[0, i], pltpu.roll(state_ref[0, i], w_buf - 1, 1))

        gate = jnp.broadcast_to(gate_ref[0], (16, LANES))
        lane = _iota((16, LANES), 1)
        hrow = _iota((16, LANES), 0)
        gsel = lambda br: jnp.sum(jnp.where(lane == 3 * hrow + br, gate, 0.0), axis=-1, keepdims=True)
        o = oc_ref[...] * gsel(0) + o_s * gsel(1) + o_w * gsel(2)
        fold = jnp.where(_iota((256, HEAD_DIM), 0) % HEAD_DIM == _iota((256, HEAD_DIM), 1), 1.0, 0.0).astype(BF16)
        o_ref[0] = _split3_dot(jnp.where(own, o, 0.0), fold)


def _nsa_decode(page_table, q, gate, kcvc, selkv_new, winkv_new, win_new, state, cache_sel, pages_per_step):
    nb, n_pages = page_table.shape
    n_pool, page = cache_sel.shape[:2]
    past_len = n_pages * page
    n_sel = -(-(past_len + 1) // L_SEL)
    assert n_sel <= 256
    pp = pages_per_step
    nc = kcvc.shape[1]
    w_buf = state.shape[1]
    ovl = _overlap_table(nc, n_sel, 256)
    keys_last = lambda a: jnp.transpose(a, (0, 2, 3, 4, 1)).reshape(a.shape[0], 2, 256, a.shape[1])
    cs = keys_last(cache_sel)
    n_steps = n_pages // pp
    per_b = lambda b, sw, j, pt: (b, 0, 0)
    per_b4 = lambda b, sw, j, pt: (b, 0, 0, 0)
    kspecs = [pl.BlockSpec((1, 1, 256, page), functools.partial(
        lambda b, sw, j, pt, i: (pt[b, jnp.where(sw == 0, j, n_steps - 1) * pp + i], 0, 0, 0), i=i)) for i in range(pp)]
    vspecs = [pl.BlockSpec((1, 1, 256, page), functools.partial(
        lambda b, sw, j, pt, i: (pt[b, jnp.where(sw == 1, j, 0) * pp + i], 1, 0, 0), i=i)) for i in range(pp)]
    grid_spec = pltpu.PrefetchScalarGridSpec(
        num_scalar_prefetch=1,
        grid=(nb, 2, n_steps),
        in_specs=[pl.BlockSpec((1, 16, HEAD_DIM), per_b), pl.BlockSpec((1, 1, LANES), per_b),
                  pl.BlockSpec((1, nc, 512), per_b), pl.BlockSpec(ovl.shape, lambda b, sw, j, pt: (0, 0)),
                  pl.BlockSpec((1, 1, 512), per_b), pl.BlockSpec((1, 1, 512), per_b), pl.BlockSpec((1, 2, 256, 1), per_b4),
                  pl.BlockSpec((1, 2, 256, w_buf), per_b4)] + kspecs + vspecs,
        out_specs=[pl.BlockSpec((1, 16, HEAD_DIM), per_b), pl.BlockSpec((1, 2, 256, w_buf), per_b4)],
        scratch_shapes=[pltpu.VMEM((n_pages, 16, page), F32),
                        pltpu.VMEM((16, 256), BF16), pltpu.VMEM((16, 256), F32), pltpu.VMEM((16, 256), F32),
                        pltpu.VMEM((16, 1), F32), pltpu.VMEM((16, 1), F32), pltpu.VMEM((16, 256), F32)],
    )
    o, win_out = pl.pallas_call(
        functools.partial(_nsa_decode_kernel, pages_per_step=pp, n_sel=n_sel, past_len=past_len),
        grid_spec=grid_spec,
        out_shape=[jax.ShapeDtypeStruct((nb, 16, HEAD_DIM), F32), jax.ShapeDtypeStruct((nb, 2, 256, w_buf), F32)],
        compiler_params=_cparams(("arbitrary", "arbitrary", "arbitrary")),
        name="nsa_decode",
    )(page_table, q.reshape(nb, 16, HEAD_DIM), gate.reshape(nb, 1, LANES), kcvc, ovl, selkv_new.reshape(nb, 1, 512),
      winkv_new.reshape(nb, 1, 512), win_new.reshape(nb, 2, 256, 1), keys_last(state), *([cs] * (2 * pp)))
    win_rows = jnp.transpose(win_out.reshape(nb, 2, 4, HEAD_DIM, w_buf), (0, 4, 1, 2, 3))
    return o.reshape(nb, 1024), win_rows


def _outproj_router_kernel(x_ref, o_ref, wo_ref, ng_ref, wr_ref, x1_ref, h_ref, comb_ref):
    x1 = x_ref[...] + _dot(o_ref[...].astype(BF16), wo_ref[...])
    x1_ref[...] = x1
    hf = _rms(x1, ng_ref[...])
    h = hf.astype(BF16)
    h_ref[...] = hf.astype(h_ref.dtype)
    logits = _dot(h, wr_ref[...])
    lane = _iota(logits.shape, 1)
    logits = jnp.where(lane < N_EXPERTS, logits, NEG_INF)
    m1 = jnp.max(logits, axis=-1, keepdims=True)
    i1 = jnp.min(jnp.where(logits == m1, lane, LANES), axis=-1, keepdims=True)
    rest = jnp.where(lane == i1, NEG_INF, logits)
    m2 = jnp.max(rest, axis=-1, keepdims=True)
    i2 = jnp.min(jnp.where(rest == m2, lane, LANES), axis=-1, keepdims=True)
    e2 = jnp.exp(m2 - m1)
    comb_ref[...] = jnp.where(lane == i1, 1.0 / (1.0 + e2), 0.0) + jnp.where(lane == i2, e2 / (1.0 + e2), 0.0)


def _outproj_router(x, o, w_out, norm_g, w_router, tm, h_dtype=BF16):
    t, d = x.shape
    row = lambda i: (i, 0)
    fixed = lambda i: (0, 0)
    wr = jnp.pad(w_router, ((0, 0), (0, LANES - w_router.shape[1]))).astype(BF16)
    outs = [jax.ShapeDtypeStruct((t, d), F32), jax.ShapeDtypeStruct((t, d), h_dtype), jax.ShapeDtypeStruct((t, LANES), F32)]
    return pl.pallas_call(
        _outproj_router_kernel,
        grid=(t // tm,),
        in_specs=[pl.BlockSpec((tm, d), row), pl.BlockSpec((tm, o.shape[1]), row), pl.BlockSpec(w_out.shape, fixed),
                  pl.BlockSpec((1, d), fixed), pl.BlockSpec(wr.shape, fixed)],
        out_specs=[pl.BlockSpec((tm, a.shape[1]), row) for a in outs],
        out_shape=outs,
        compiler_params=_cparams(("arbitrary",)),
        name="outproj_router",
    )(x, o, w_out.astype(BF16), norm_g.reshape(1, d), wr)


def _moe_kernel(x1_ref, h_ref, comb_ref, wg_ref, wu_ref, wd_ref, y_ref, acc_ref):
    e = pl.program_id(1)
    j = pl.program_id(2)

    @pl.when((e == 0) & (j == 0))
    def _():
        acc_ref[...] = jnp.zeros_like(acc_ref)

    h = h_ref[...]
    g = _dot(h, wg_ref[0])
    u = _dot(h, wu_ref[0])
    a = (g * jax.nn.sigmoid(g) * u).astype(BF16)
    comb = comb_ref[...]
    w_e = jnp.sum(jnp.where(_iota(comb.shape, 1) == e, comb, 0.0), axis=-1, keepdims=True)
    acc_ref[...] += w_e * _dot(a, wd_ref[0])

    @pl.when((e == pl.num_programs(1) - 1) & (j == pl.num_programs(2) - 1))
    def _():
        y_ref[...] = x1_ref[...] + acc_ref[...]


def _moe(x1, h, comb, w_gate_up, w_down, tm, tf):
    t, d = x1.shape
    n_e, ff = w_down.shape[:2]
    nj = ff // tf
    row = lambda i, e, j: (i, 0)
    wgu = w_gate_up.astype(BF16)
    return pl.pallas_call(
        _moe_kernel,
        grid=(t // tm, n_e, nj),
        in_specs=[pl.BlockSpec((tm, d), row), pl.BlockSpec((tm, d), row), pl.BlockSpec((tm, LANES), row),
                  pl.BlockSpec((1, d, tf), lambda i, e, j: (e, 0, j)),
                  pl.BlockSpec((1, d, tf), lambda i, e, j: (e, 0, nj + j)),
                  pl.BlockSpec((1, tf, d), lambda i, e, j: (e, j, 0))],
        out_specs=pl.BlockSpec((tm, d), row),
        out_shape=jax.ShapeDtypeStruct((t, d), F32),
        scratch_shapes=[pltpu.VMEM((tm, d), F32)],
        compiler_params=_cparams(("arbitrary", "arbitrary", "arbitrary")),
        name="moe",
    )(x1, h, comb, wgu, wgu, w_down.astype(BF16))


def _moe_plan(comb, tm):
    t = comb.shape[0]
    routed = comb[:, :N_EXPERTS] > 0.0
    r32 = routed.astype(jnp.int32)
    counts = jnp.sum(r32, axis=0)
    padded = ((counts + tm - 1) // tm) * tm
    ends = jnp.cumsum(padded)
    starts = ends - padded
    dest = starts[None, :] + jnp.cumsum(r32, axis=0) - r32
    n_tiles = (2 * t) // tm + N_EXPERTS
    n_rows = n_tiles * tm
    tok = jnp.broadcast_to(jnp.arange(t, dtype=jnp.int32)[:, None], dest.shape)
    tok_of_row = jnp.zeros((n_rows,), jnp.int32).at[jnp.where(routed, dest, n_rows).reshape(-1)].set(
        tok.reshape(-1), mode="drop")
    tile_start = jnp.arange(n_tiles, dtype=jnp.int32) * tm
    tile_expert = jnp.minimum(jnp.sum((tile_start[:, None] >= ends[None, :]).astype(jnp.int32), axis=1),
                              N_EXPERTS - 1)
    tile_used = (tile_start < ends[-1]).astype(jnp.int32)
    first = jnp.argmax(routed, axis=1)
    last = N_EXPERTS - 1 - jnp.argmax(routed[:, ::-1], axis=1)
    row_a = jnp.take_along_axis(dest, first[:, None], axis=1)[:, 0]
    row_b = jnp.take_along_axis(dest, last[:, None], axis=1)[:, 0]
    return tok_of_row.reshape(n_tiles, tm), tile_expert, tile_used, row_a.astype(jnp.int32), row_b.astype(jnp.int32)


def _moe_rows_kernel(te_ref, tu_ref, tok_ref, h_hbm, wg_ref, wu_ref, wd_ref, y_ref, xg_ref, xb_ref, acc_ref, sem):
    i = pl.program_id(0)
    j = pl.program_id(1)
    tm = xg_ref.shape[0]
    used = tu_ref[i] > 0

    def row_copy(r, tok):
        return pltpu.make_async_copy(h_hbm.at[pl.ds(tok, 1), :], xg_ref.at[pl.ds(r, 1), :], sem)

    @pl.when(used & (j == 0))
    def _():
        def start(r, c):
            row_copy(r, tok_ref[0, 0, r]).start()
            return c

        lax.fori_loop(0, tm, start, 0, unroll=8)
        pltpu.make_async_copy(h_hbm.at[pl.ds(0, tm), :], xg_ref, sem).wait()
        xb_ref[...] = xg_ref[...].astype(BF16)
        acc_ref[...] = jnp.zeros_like(acc_ref)

    @pl.when(used)
    def _():
        h = xb_ref[...]
        g = _dot(h, wg_ref[0])
        u = _dot(h, wu_ref[0])
        acc_ref[...] += _dot((g * jax.nn.sigmoid(g) * u).astype(BF16), wd_ref[0])

    last = j == pl.num_programs(1) - 1

    @pl.when(used & last)
    def _():
        y_ref[...] = acc_ref[...]

    @pl.when(jnp.logical_not(used) & last)
    def _():
        y_ref[...] = jnp.zeros_like(y_ref)


def _moe_rows(h, tok_of_row, tile_expert, tile_used, w_gate_up, w_down, tf):
    n_tiles, tm = tok_of_row.shape
    d = h.shape[1]
    n_e, ff = w_down.shape[:2]
    nj = ff // tf
    wgu = w_gate_up.astype(BF16)
    grid_spec = pltpu.PrefetchScalarGridSpec(
        num_scalar_prefetch=2,
        grid=(n_tiles, nj),
        in_specs=[pl.BlockSpec((1, 1, tm), lambda i, j, te, tu: (i, 0, 0), memory_space=pltpu.SMEM),
                  pl.BlockSpec(memory_space=pl.ANY),
                  pl.BlockSpec((1, d, tf), lambda i, j, te, tu: (te[i], 0, j)),
                  pl.BlockSpec((1, d, tf), lambda i, j, te, tu: (te[i], 0, nj + j)),
                  pl.BlockSpec((1, tf, d), lambda i, j, te, tu: (te[i], j, 0))],
        out_specs=pl.BlockSpec((tm, d), lambda i, j, te, tu: (i, 0)),
        scratch_shapes=[pltpu.VMEM((tm, d), F32), pltpu.VMEM((tm, d), BF16), pltpu.VMEM((tm, d), F32),
                        pltpu.SemaphoreType.DMA(())],
    )
    return pl.pallas_call(
        _moe_rows_kernel,
        grid_spec=grid_spec,
        out_shape=jax.ShapeDtypeStruct((n_tiles * tm, d), F32),
        compiler_params=_cparams(("arbitrary", "arbitrary")),
        name="moe_rows",
    )(tile_expert, tile_used, tok_of_row.reshape(n_tiles, 1, tm), h, wgu, wgu, w_down.astype(BF16))


def _moe_combine_kernel(ra_ref, rb_ref, x1_ref, comb_ref, ys_hbm, out_ref, buf_ref, sem):
    tc = x1_ref.shape[0]

    def row_copy(slot, r, src):
        return pltpu.make_async_copy(ys_hbm.at[pl.ds(src, 1), :], buf_ref.at[slot, pl.ds(r, 1), :], sem)

    def start(r, c):
        row_copy(0, r, ra_ref[0, 0, r]).start()
        row_copy(1, r, rb_ref[0, 0, r]).start()
        return c

    lax.fori_loop(0, tc, start, 0, unroll=8)
    for slot in range(2):
        pltpu.make_async_copy(ys_hbm.at[pl.ds(0, tc), :], buf_ref.at[slot], sem).wait()
    comb = comb_ref[...]
    lane = _iota(comb.shape, 1)
    on = comb > 0.0
    ia = jnp.min(jnp.where(on, lane, LANES), axis=-1, keepdims=True)
    ib = jnp.max(jnp.where(on, lane, -1), axis=-1, keepdims=True)
    w_a = jnp.sum(jnp.where(lane == ia, comb, 0.0), axis=-1, keepdims=True)
    w_b = jnp.sum(jnp.where((lane == ib) & (ib != ia), comb, 0.0), axis=-1, keepdims=True)
    out_ref[...] = x1_ref[...] + (w_a * buf_ref[0] + w_b * buf_ref[1])


def _moe_combine(x1, comb, ys, row_a, row_b, tc):
    t, d = x1.shape
    row = lambda i: (i, 0)
    smem_row = pl.BlockSpec((1, 1, tc), lambda i: (i, 0, 0), memory_space=pltpu.SMEM)
    return pl.pallas_call(
        _moe_combine_kernel,
        grid=(t // tc,),
        in_specs=[smem_row, smem_row, pl.BlockSpec((tc, d), row), pl.BlockSpec((tc, LANES), row),
                  pl.BlockSpec(memory_space=pl.ANY)],
        out_specs=pl.BlockSpec((tc, d), row),
        out_shape=jax.ShapeDtypeStruct((t, d), F32),
        scratch_shapes=[pltpu.VMEM((2, tc, d), F32), pltpu.SemaphoreType.DMA(())],
        compiler_params=_cparams(("arbitrary",)),
        name="moe_combine",
    )(row_a.reshape(t // tc, 1, tc), row_b.reshape(t // tc, 1, tc), x1, comb, ys)


def _moe_grouped(x1, h, comb, w_gate_up, w_down, tm, tf, tc):
    tok_of_row, tile_expert, tile_used, row_a, row_b = _moe_plan(comb, tm)
    ys = _moe_rows(h, tok_of_row, tile_expert, tile_used, w_gate_up, w_down, tf)
    return _moe_combine(x1, comb, ys, row_a, row_b, tc)


def kernel(x_prompt, x_sample, cache_l0_diff, cache_l0_sb, cache_l1_cmp, cache_l1_sel, state_l1_win, page_table, l0_norm_mix, l0_w_in, l0_diff_q_norm, l0_diff_k_norm, l0_diff_lambda, l0_diff_subln, l0_w_out, l0_norm_ffn, l0_ffn_w_gate_up, l0_ffn_w_down, l1_norm_mix, l1_w_in, l1_q_norm, l1_k_norm, l1_cmp_pos, l1_cmp_w1, l1_cmp_w2, l1_w_out, l1_norm_ffn, l1_router, l1_moe_w_gate_up, l1_moe_w_down):
    b, s, d = x_prompt.shape
    nb = x_sample.shape[0]
    n_pages, page = page_table.shape[1], cache_l0_diff.shape[1]
    past_len = n_pages * page
    tm = min(512, s)
    tq = min(256, s)
    dec_pages = min(16, n_pages)
    cmp_pages = min(32, n_pages)
    ff_tile = l0_ffn_w_down.shape[0] // 2
    moe_tile = l1_moe_w_down.shape[1] // 7
    tabs_p = _rope_tables(jnp.arange(s, dtype=jnp.int32))
    tabs_s = _rope_tables(jnp.full((nb,), past_len, jnp.int32))
    xp = x_prompt.reshape(b * s, d)
    xs = x_sample.reshape(nb, d)

    dq, diff_p, sq, sb_p, dkv, skv = _proj0(xp, l0_norm_mix, l0_w_in, l0_diff_q_norm, l0_diff_k_norm, tabs_p, tm)
    o = _attn0_prompt(l0_diff_lambda, l0_diff_subln, dq, sq, dkv, skv, b, s, tq)
    xp = _outproj_ffn(xp, o, l0_w_out, l0_norm_ffn, l0_ffn_w_gate_up, l0_ffn_w_down, tm, ff_tile)
    dq, diff_s, sq, sb_s, dkv, skv = _proj0(xs, l0_norm_mix, l0_w_in, l0_diff_q_norm, l0_diff_k_norm, tabs_s, nb)
    o = _attn0_decode(page_table, l0_diff_lambda, l0_diff_subln, dq, sq, dkv, cache_l0_diff, cache_l0_sb, dec_pages)
    xs = _outproj_ffn(xs, o, l0_w_out, l0_norm_ffn, l0_ffn_w_gate_up, l0_ffn_w_down, nb, ff_tile)

    cw = _compress_weights(l1_cmp_pos, l1_cmp_w1, l1_cmp_w2)
    q, cmp_p, sel_p, win_p, gate, selkv, winkv = _proj1(xp, l1_norm_mix, l1_w_in, l1_q_norm, l1_k_norm, tabs_p, tm)
    kcvc = _compress_prompt(cmp_p, b, s, cw)
    o = _nsa_prompt(q, gate, kcvc, selkv, winkv, b, s, tq)
    x1, h, comb = _outproj_router(xp, o, l1_w_out, l1_norm_ffn, l1_router, tm, h_dtype=F32)
    xp = _moe_grouped(x1, h, comb, l1_moe_w_gate_up, l1_moe_w_down, tm, l1_moe_w_down.shape[1] // 2, min(256, s))

    q, cmp_s, sel_s, win_s_new, gate, selkv, winkv = _proj1(xs, l1_norm_mix, l1_w_in, l1_q_norm, l1_k_norm, tabs_s, nb)
    kcvc = _compress_decode(cache_l1_cmp, page_table, cw, cmp_pages)
    o, win_s = _nsa_decode(page_table, q, gate, kcvc, selkv, winkv, win_s_new, state_l1_win, cache_l1_sel, dec_pages)
    x1, h, comb = _outproj_router(xs, o, l1_w_out, l1_norm_ffn, l1_router, nb)
    xs = _moe(x1, h, comb, l1_moe_w_gate_up, l1_moe_w_down, nb, moe_tile)

    w_keep = min(WINDOW, s)
    hk = cmp_p.shape[-1] // (2 * HEAD_DIM)
    rows5 = lambda a, n: a.reshape(n, -1, 2, a.shape[-1] // (2 * HEAD_DIM), HEAD_DIM)
    return (xp.reshape(b, s, d), xs.reshape(nb, 1, d),
            diff_p.reshape(b, s, 2, 4, 2 * HEAD_DIM), sb_p.reshape(b, s, 2, 8, HEAD_DIM),
            rows5(cmp_p, b), rows5(sel_p, b), rows5(win_p, b)[:, s - w_keep:],
            diff_s.reshape(nb, 1, 2, 4, 2 * HEAD_DIM), sb_s.reshape(nb, 1, 2, 8, HEAD_DIM),
            rows5(cmp_s, nb), rows5(sel_s, nb), win_s)
```

```python
import functools

import numpy as np
import jax
import jax.numpy as jnp
from jax import lax
from jax.experimental import pallas as pl
from jax.experimental.pallas import tpu as pltpu

F32 = jnp.float32
BF16 = jnp.bfloat16

HEAD_DIM = 64
ROT_DIM = HEAD_DIM // 4
ROPE_THETA = 500000.0
EPS = 1e-6
DIFF_LAMBDA_INIT = 0.2
L_CMP = 32
CMP_STRIDE = 16
L_SEL = 64
N_SEL = 16
WINDOW = 512
FORCE_BONUS = 1.0e4
N_EXPERTS = 8
LANES = 128
VMEM_LIMIT = 56 * 1024 * 1024
NEG_INF = float("-inf")
SB_LOG_WEIGHT_FLOOR = -110.0


def _cparams(sem):
    return pltpu.CompilerParams(dimension_semantics=sem, vmem_limit_bytes=VMEM_LIMIT)


def _dot(a, b):
    return jnp.dot(a, b, preferred_element_type=F32)


def _dot_nt(a, b):
    return lax.dot_general(a, b, (((1,), (1,)), ((), ())), preferred_element_type=F32)


def _split2_dot(x, m):
    hi = x.astype(BF16)
    lo = (x - hi.astype(F32)).astype(BF16)
    return _dot(hi, m) + _dot(lo, m)


def _split3_dot(x, m):
    hi = x.astype(BF16)
    r1 = x - hi.astype(F32)
    mid = r1.astype(BF16)
    lo = (r1 - mid.astype(F32)).astype(BF16)
    return _dot(hi, m) + _dot(mid, m) + _dot(lo, m)


def _tree(op, xs):
    xs = list(xs)
    while len(xs) > 1:
        xs = [op(a, b) for a, b in zip(xs[0::2], xs[1::2])] + ([xs[-1]] if len(xs) % 2 else [])
    return xs[0]


def _iota(shape, dim):
    return lax.broadcasted_iota(jnp.int32, shape, dim)


def _group_ones(n, group):
    r = _iota((n, n), 0) // group
    c = _iota((n, n), 1) // group
    return jnp.where(r == c, 1.0, 0.0).astype(BF16)


def _rms(x, g):
    return x * lax.rsqrt(jnp.mean(x * x, axis=-1, keepdims=True) + EPS) * g


def _head_norm_rope(xc, g, ones64, c, sa, sb):
    ms = _split2_dot(xc * xc, ones64) * (1.0 / HEAD_DIM)
    xn = xc * lax.rsqrt(ms + EPS) * g
    return xn * c + pltpu.roll(xn, LANES - ROT_DIM // 2, 1) * sa + pltpu.roll(xn, ROT_DIM // 2, 1) * sb


def _rope_tables(pos):
    half = ROT_DIM // 2
    inv = ROPE_THETA ** (-jnp.arange(half, dtype=F32) * 2.0 / ROT_DIM)
    ang = pos.astype(F32)[:, None] * inv[None, :]
    cos, sin = jnp.cos(ang), jnp.sin(ang)
    n = pos.shape[0]
    one = jnp.ones((n, HEAD_DIM - ROT_DIM), F32)
    zero_h = jnp.zeros((n, half), F32)
    zero_r = jnp.zeros((n, HEAD_DIM - ROT_DIM), F32)
    c = jnp.concatenate([cos, cos, one], axis=1)
    sa = jnp.concatenate([-sin, zero_h, zero_r], axis=1)
    sb = jnp.concatenate([zero_h, sin, zero_r], axis=1)
    tile = lambda t: jnp.concatenate([t, t], axis=1)
    return tile(c), tile(sa), tile(sb)


def _proj0_kernel(x_ref, ng_ref, w_ref, qg_ref, kg_ref, c_ref, sa_ref, sb_ref,
                  dq_ref, drows_ref, sq_ref, srows_ref, dkv_ref, skv_ref):
    h = _rms(x_ref[...], ng_ref[...]).astype(BF16)
    ones64 = _group_ones(LANES, HEAD_DIM)
    c, sa, sb = c_ref[...], sa_ref[...], sb_ref[...]
    scale = HEAD_DIM ** -0.5
    y = _dot(h, w_ref[:, 0:512])
    for j in range(4):
        sl = slice(j * LANES, (j + 1) * LANES)
        dq_ref[:, sl] = (_head_norm_rope(y[:, sl], qg_ref[...], ones64, c, sa, sb) * scale).astype(BF16)
    y = _dot(h, w_ref[:, 512:1024])
    for j in range(4):
        sl = slice(j * LANES, (j + 1) * LANES)
        k = _head_norm_rope(y[:, sl], kg_ref[...], ones64, c, sa, sb)
        drows_ref[:, sl] = k
        dkv_ref[:, sl] = k.astype(BF16)
    y = _dot(h, w_ref[:, 1024:1536])
    drows_ref[:, 512:1024] = y
    dkv_ref[:, 512:1024] = y.astype(BF16)
    y = _dot(h, w_ref[:, 1536:2048])
    sq_ref[...] = (y * scale).astype(BF16)
    y = _dot(h, w_ref[:, 2048:3072])
    srows_ref[...] = y
    skv_ref[...] = y.astype(BF16)


def _proj0(x, norm_g, w_in, q_g, k_g, tabs, tm):
    t, d = x.shape
    tab_blocks = tabs[0].shape[0] // tm
    row = lambda i: (i, 0)
    fixed = lambda i: (0, 0)
    tab = lambda i: (i % tab_blocks, 0)
    g2 = lambda g: jnp.tile(g.reshape(1, HEAD_DIM), (1, LANES // HEAD_DIM))
    outs = [jax.ShapeDtypeStruct((t, 512), BF16), jax.ShapeDtypeStruct((t, 1024), F32),
            jax.ShapeDtypeStruct((t, 512), BF16), jax.ShapeDtypeStruct((t, 1024), F32),
            jax.ShapeDtypeStruct((t, 1024), BF16), jax.ShapeDtypeStruct((t, 1024), BF16)]
    return pl.pallas_call(
        _proj0_kernel,
        grid=(t // tm,),
        in_specs=[pl.BlockSpec((tm, d), row), pl.BlockSpec((1, d), fixed),
                  pl.BlockSpec(w_in.shape, fixed),
                  pl.BlockSpec((1, LANES), fixed), pl.BlockSpec((1, LANES), fixed),
                  pl.BlockSpec((tm, LANES), tab), pl.BlockSpec((tm, LANES), tab), pl.BlockSpec((tm, LANES), tab)],
        out_specs=[pl.BlockSpec((tm, o.shape[1]), row) for o in outs],
        out_shape=outs,
        compiler_params=_cparams(("arbitrary",)),
        name="proj0",
    )(x, norm_g.reshape(1, d), w_in.astype(BF16), g2(q_g), g2(k_g), *tabs)


def _lane_half_masks(shape):
    lane = _iota(shape, 1)
    return lane < HEAD_DIM, lane >= HEAD_DIM


def _diff_lambda(lp):
    a = jnp.sum(lp[0:1] * lp[1:2], axis=-1, keepdims=True)
    b = jnp.sum(lp[2:3] * lp[3:4], axis=-1, keepdims=True)
    return jnp.exp(a) - jnp.exp(b) + DIFF_LAMBDA_INIT


def _attn0_kernel(lam_ref, subg_ref, dq_ref, sq_ref, dkv_ref, skv_ref, o_ref, *, tq):
    qi = pl.program_id(1)
    q0 = qi * tq
    lam = _diff_lambda(lam_ref[...])
    lo_mask, hi_mask = _lane_half_masks((tq, LANES))
    row = _iota((tq, tq), 0)
    col = _iota((tq, tq), 1)
    zero_bf = jnp.zeros((tq, LANES), BF16)

    def split_maps(ref):
        out = []
        for c in range(4):
            q = ref[0, :, c * LANES:(c + 1) * LANES]
            out += [jnp.where(lo_mask, q, zero_bf), jnp.where(hi_mask, q, zero_bf)]
        return out

    qd = split_maps(dq_ref)

    def step(kb, carry, diag):
        rows = pl.ds(pl.multiple_of(kb * tq, tq), tq)
        out = []
        for h in range(4):
            k = dkv_ref[0, rows, h * LANES:(h + 1) * LANES]
            v = dkv_ref[0, rows, 512 + h * LANES:512 + (h + 1) * LANES]
            for m in range(2):
                m_run, l_run, acc = carry[2 * h + m]
                s = _dot_nt(qd[2 * h + m], k)
                if diag:
                    s = jnp.where(col <= row, s, NEG_INF)
                m_new = jnp.maximum(m_run, jnp.max(s, axis=-1, keepdims=True))
                p = jnp.exp(s - m_new)
                alpha = jnp.exp(m_run - m_new)
                l_new = alpha * l_run + jnp.sum(p, axis=-1, keepdims=True)
                out.append((m_new, l_new, alpha * acc + _dot(p.astype(BF16), v)))
        return tuple(out)

    init = tuple((jnp.full((tq, 1), NEG_INF, F32), jnp.zeros((tq, 1), F32), jnp.zeros((tq, LANES), F32))
                 for _ in range(8))
    carry = lax.fori_loop(0, qi, lambda kb, c: step(kb, c, False), step(qi, init, True))
    for h in range(4):
        (_, l0, a0), (_, l1, a1) = carry[2 * h], carry[2 * h + 1]
        o = a0 / l0 - lam * (a1 / l1)
        sl = slice(h * LANES, (h + 1) * LANES)
        o_ref[0, :, sl] = _rms(o, subg_ref[:, sl]) * (1.0 - DIFF_LAMBDA_INIT)

    upper = jnp.where(row > col, 1.0, 0.0).astype(BF16)
    qs = split_maps(sq_ref)

    def sb_step(kb, carry, diag):
        rows = pl.ds(pl.multiple_of(kb * tq, tq), tq)
        out = []
        for c in range(4):
            k = skv_ref[0, rows, c * LANES:(c + 1) * LANES]
            v = skv_ref[0, rows, 512 + c * LANES:512 + (c + 1) * LANES]
            for m in range(2):
                tail, acc = carry[2 * c + m]
                z = _dot_nt(qs[2 * c + m], k)
                t = jnp.log1p(jnp.exp(-jnp.abs(z)))
                lsp = -(jnp.maximum(-z, 0.0) + t)
                l1m = -(jnp.maximum(z, 0.0) + t)
                if diag:
                    l1m = jnp.where(col < row, l1m, 0.0)
                w = jnp.exp(lsp + _split2_dot(l1m, upper) + tail)
                if diag:
                    w = jnp.where(col < row, w, 0.0)
                out.append((tail + jnp.sum(l1m, axis=-1, keepdims=True), acc + _dot(w.astype(BF16), v)))
        return tuple(out)

    def still_visible(carry):
        return jnp.max(_tree(jnp.maximum, [t for t, _ in carry])) > SB_LOG_WEIGHT_FLOOR

    def sb_body(c):
        kb, _, carry = c
        carry = sb_step(kb, carry, False)
        return kb - 1, still_visible(carry), carry

    init = tuple((jnp.zeros((tq, 1), F32), jnp.zeros((tq, LANES), F32)) for _ in range(8))
    first = sb_step(qi, init, True)
    _, _, carry = lax.while_loop(lambda c: (c[0] >= 0) & c[1], sb_body, (qi - 1, still_visible(first), first))
    for c in range(4):
        o_ref[0, :, 512 + c * LANES:512 + (c + 1) * LANES] = jnp.where(lo_mask, carry[2 * c][1], carry[2 * c + 1][1])


def _attn0_prompt(lam_p, subln_g, dq, sq, dkv, skv, b, s, tq):
    r3 = lambda a: a.reshape(b, s, a.shape[-1])
    qspec = pl.BlockSpec((1, tq, 512), lambda bi, qi: (bi, qi, 0))
    kvspec = pl.BlockSpec((1, s, 1024), lambda bi, qi: (bi, 0, 0))
    fixed = lambda bi, qi: (0, 0)
    out = pl.pallas_call(
        functools.partial(_attn0_kernel, tq=tq),
        grid=(b, s // tq),
        in_specs=[pl.BlockSpec((4, HEAD_DIM), fixed), pl.BlockSpec((1, 512), fixed), qspec, qspec, kvspec, kvspec],
        out_specs=pl.BlockSpec((1, tq, 1024), lambda bi, qi: (bi, qi, 0)),
        out_shape=jax.ShapeDtypeStruct((b, s, 1024), F32),
        compiler_params=_cparams(("arbitrary", "arbitrary")),
        name="attn0_prompt",
    )(lam_p, subln_g.reshape(1, 512), r3(dq), r3(sq), r3(dkv), r3(skv))
    return out.reshape(b * s, 1024)


def _outproj_ffn_kernel(x_ref, o_ref, wo_ref, ng_ref, wg_ref, wu_ref, wd_ref, y_ref, x1_ref, h_ref, acc_ref):
    j = pl.program_id(1)

    @pl.when(j == 0)
    def _():
        x1 = x_ref[...] + _dot(o_ref[...].astype(BF16), wo_ref[...])
        x1_ref[...] = x1
        h_ref[...] = _rms(x1, ng_ref[...]).astype(BF16)
        acc_ref[...] = jnp.zeros_like(acc_ref)

    h = h_ref[...]
    g = _dot(h, wg_ref[...])
    u = _dot(h, wu_ref[...])
    a = (g * jax.nn.sigmoid(g) * u).astype(BF16)
    acc_ref[...] += _dot(a, wd_ref[...])

    @pl.when(j == pl.num_programs(1) - 1)
    def _():
        y_ref[...] = x1_ref[...] + acc_ref[...]


def _outproj_ffn(x, o, w_out, norm_g, w_gate_up, w_down, tm, tf):
    t, d = x.shape
    ff = w_down.shape[0]
    nj = ff // tf
    row = lambda i, j: (i, 0)
    fixed = lambda i, j: (0, 0)
    wgu = w_gate_up.astype(BF16)
    return pl.pallas_call(
        _outproj_ffn_kernel,
        grid=(t // tm, nj),
        in_specs=[pl.BlockSpec((tm, d), row), pl.BlockSpec((tm, o.shape[1]), row),
                  pl.BlockSpec(w_out.shape, fixed), pl.BlockSpec((1, d), fixed),
                  pl.BlockSpec((d, tf), lambda i, j: (0, j)), pl.BlockSpec((d, tf), lambda i, j: (0, nj + j)),
                  pl.BlockSpec((tf, d), lambda i, j: (j, 0))],
        out_specs=pl.BlockSpec((tm, d), row),
        out_shape=jax.ShapeDtypeStruct((t, d), F32),
        scratch_shapes=[pltpu.VMEM((tm, d), F32), pltpu.VMEM((tm, d), BF16), pltpu.VMEM((tm, d), F32)],
        compiler_params=_cparams(("arbitrary", "arbitrary")),
        name="outproj_ffn",
    )(x, o, w_out.astype(BF16), norm_g.reshape(1, d), wgu, wgu, w_down.astype(BF16))


def _rows_select(x, group, nrows):
    w = x.shape[-1]
    keep = _iota((nrows, w), 1) // group == _iota((nrows, w), 0)
    return jnp.where(keep, jnp.broadcast_to(x.astype(F32), (nrows, w)), 0.0).astype(x.dtype)


def _fold_chunks(x):
    return x[:, 0:LANES] + x[:, LANES:2 * LANES] + x[:, 2 * LANES:3 * LANES] + x[:, 3 * LANES:4 * LANES]


def _attn0_decode_kernel(pt_ref, lam_ref, subg_ref, dq_ref, sq_ref, dnew_ref, *rest, pages_per_step):
    pp = pages_per_step
    dpages = rest[:pp]
    spages = rest[pp:2 * pp]
    od_ref, os_ref = rest[2 * pp:2 * pp + 2]
    s_ref, md_ref, ld_ref, accd_ref, tail_ref, accs_ref = rest[2 * pp + 2:]
    sweep = pl.program_id(1)
    j = pl.program_id(2)
    n_steps = pl.num_programs(2)
    qd = _rows_select(dq_ref[0], HEAD_DIM, 8)
    qd128 = _fold_chunks(qd.astype(F32)).astype(BF16)
    nrow = dpages[0].shape[1]
    page = nrow // 8
    new = dnew_ref[0]
    s_new = jnp.sum(qd.astype(F32) * new[:, 0:512].astype(F32), axis=-1, keepdims=True)
    lam = _diff_lambda(lam_ref[...])
    even_row = _iota((8, 1), 0) % 2 == 0

    def combine(pn):
        return jnp.where(even_row, pn - lam * pltpu.roll(pn, 7, 0), 0.0)

    @pl.when((sweep == 0) & (j == 0))
    def _():
        md_ref[...] = s_new
        tail_ref[...] = jnp.zeros_like(tail_ref)
        accs_ref[...] = jnp.zeros_like(accs_ref)

    @pl.when(sweep == 0)
    def _():
        qs = _rows_select(sq_ref[0], HEAD_DIM, 8)
        upper = jnp.where(_iota((page, page), 0) > _iota((page, page), 1), 1.0, 0.0).astype(BF16)
        is_key_row = _iota((8, nrow), 1) % 8 == _iota((8, nrow), 0) // 2
        tops = []
        for i in range(pp):
            s = jnp.where(is_key_row, _dot_nt(qd128, dpages[i][0].astype(BF16)), NEG_INF)
            s_ref[j * pp + i] = s
            tops.append(jnp.max(s, axis=-1, keepdims=True))
        md_ref[...] = jnp.maximum(md_ref[...], _tree(jnp.maximum, tops))

        z = jnp.concatenate([_dot(qs, spages[i][0, 0].astype(BF16)) for i in range(pp)], axis=0)
        t = jnp.log1p(jnp.exp(-jnp.abs(z)))
        lsp = -(jnp.maximum(-z, 0.0) + t)
        l1m = -(jnp.maximum(z, 0.0) + t)
        within = _split3_dot(l1m, upper)
        totals = jnp.sum(l1m, axis=-1, keepdims=True)
        run = tail_ref[...]
        tails = []
        for i in range(pp):
            tails.append(run)
            run = run + totals[i * 8:(i + 1) * 8]
        tail_ref[...] = run
        w = jnp.exp(lsp + within + jnp.concatenate(tails, axis=0)).astype(BF16)
        accs_ref[...] += _tree(jnp.add, [_dot_nt(w[i * 8:(i + 1) * 8], spages[i][0, 1].astype(BF16))
                                         for i in range(pp)])

    @pl.when((sweep == 1) & (j == 0))
    def _():
        m = md_ref[...]
        total = lax.fori_loop(0, s_ref.shape[0],
                              lambda g, t: t + jnp.sum(jnp.exp(s_ref[g] - m), axis=-1, keepdims=True),
                              jnp.exp(s_new - m))
        ld_ref[...] = jnp.maximum(total, 1e-30)
        own_head = _iota((8, 512), 1) // LANES == _iota((8, 512), 0) // 2
        v_new = _fold_chunks(jnp.where(own_head, jnp.broadcast_to(new[:, 512:1024].astype(F32), (8, 512)), 0.0))
        p_new = combine(jnp.exp(s_new - m) / ld_ref[...])
        accd_ref[...] = p_new.astype(BF16).astype(F32) * v_new

    @pl.when(sweep == 1)
    def _():
        m, l = md_ref[...], ld_ref[...]
        parts = []
        for i in range(pp):
            pc = combine(jnp.exp(s_ref[j * pp + i] - m) / l)
            p_on_v = pltpu.roll(pc, 4, 1).astype(BF16)
            parts.append(_dot(p_on_v, dpages[i][0].astype(BF16)))
        accd_ref[...] += _tree(jnp.add, parts)

    @pl.when((sweep == 1) & (j == n_steps - 1))
    def _():
        od_ref[0] = _rms(accd_ref[...], subg_ref[...]) * (1.0 - DIFF_LAMBDA_INIT)
        acs = accs_ref[...]
        os_ref[0] = jnp.sum(jnp.where(_iota(acs.shape, 0) == _iota(acs.shape, 1) // HEAD_DIM, acs, 0.0), axis=0,
                            keepdims=True)


def _attn0_decode(page_table, lam_p, subln_g, dq, sq, dkv, cache_diff, cache_sb, pages_per_step):
    nb, n_pages = page_table.shape
    n_pool, page = cache_diff.shape[:2]
    pp = pages_per_step
    cd = cache_diff.reshape(n_pool, page * 8, LANES)
    cs = jnp.transpose(cache_sb, (0, 2, 3, 4, 1)).reshape(n_pool, 2, 512, page)
    n_steps = n_pages // pp
    fixed = lambda b, sw, j, pt: (0, 0)
    per_b = lambda b, sw, j, pt: (b, 0, 0)
    newest_first = lambda i: (lambda b, j, pt: pt[b, n_pages - 1 - (j * pp + i)])

    def dspec(i):
        pick = newest_first(i)
        return pl.BlockSpec((1, page * 8, LANES), lambda b, sw, j, pt: (pick(b, j, pt), 0, 0))

    def sspec(i):
        pick = newest_first(i)
        return pl.BlockSpec((1, 2, 512, page),
                            lambda b, sw, j, pt: (pick(b, jnp.where(sw == 0, j, n_steps - 1), pt), 0, 0, 0))

    grid_spec = pltpu.PrefetchScalarGridSpec(
        num_scalar_prefetch=1,
        grid=(nb, 2, n_steps),
        in_specs=[pl.BlockSpec((4, HEAD_DIM), fixed), pl.BlockSpec((8, LANES), fixed),
                  pl.BlockSpec((1, 1, 512), per_b), pl.BlockSpec((1, 1, 512), per_b),
                  pl.BlockSpec((1, 1, 1024), per_b)]
                 + [dspec(i) for i in range(pp)] + [sspec(i) for i in range(pp)],
        out_specs=[pl.BlockSpec((1, 8, LANES), per_b), pl.BlockSpec((1, 1, 512), per_b)],
        scratch_shapes=[pltpu.VMEM((n_pages, 8, page * 8), F32),
                        pltpu.VMEM((8, 1), F32), pltpu.VMEM((8, 1), F32), pltpu.VMEM((8, LANES), F32),
                        pltpu.VMEM((8, 1), F32), pltpu.VMEM((8, 512), F32)],
    )
    od, osb = pl.pallas_call(
        functools.partial(_attn0_decode_kernel, pages_per_step=pp),
        grid_spec=grid_spec,
        out_shape=[jax.ShapeDtypeStruct((nb, 8, LANES), F32), jax.ShapeDtypeStruct((nb, 1, 512), F32)],
        compiler_params=_cparams(("arbitrary", "arbitrary", "arbitrary")),
        name="attn0_decode",
    )(page_table, lam_p, jnp.repeat(subln_g, 2, axis=0), dq.reshape(nb, 1, 512), sq.reshape(nb, 1, 512),
      dkv.reshape(nb, 1, 1024), *([cd] * pp), *([cs] * pp))
    return jnp.concatenate([od[:, 0::2, :].reshape(nb, 512), osb.reshape(nb, 512)], axis=-1)


def _proj1_kernel(x_ref, ng_ref, w_ref, qg_ref, kg_ref, c_ref, sa_ref, sb_ref,
                  q_ref, cmp_ref, sel_ref, win_ref, gate_ref, selkv_ref, winkv_ref):
    h = _rms(x_ref[...], ng_ref[...]).astype(BF16)
    ones64 = _group_ones(LANES, HEAD_DIM)
    c, sa, sb = c_ref[...], sa_ref[...], sb_ref[...]
    scale = HEAD_DIM ** -0.5
    for half in range(2):
        y = _dot(h, w_ref[:, half * 512:(half + 1) * 512])
        for j in range(4):
            sl = slice(j * LANES, (j + 1) * LANES)
            q = _head_norm_rope(y[:, sl], qg_ref[...], ones64, c, sa, sb) * scale
            q_ref[:, half * 512 + j * LANES:half * 512 + (j + 1) * LANES] = q.astype(BF16)
    for i, (rows_ref, bf_ref) in enumerate(((cmp_ref, None), (sel_ref, selkv_ref), (win_ref, winkv_ref))):
        y = _dot(h, w_ref[:, 1024 + i * 512:1024 + (i + 1) * 512])
        for j in range(2):
            sl = slice(j * LANES, (j + 1) * LANES)
            k = _head_norm_rope(y[:, sl], kg_ref[i:i + 1, :], ones64, c, sa, sb)
            rows_ref[:, sl] = k
            if bf_ref is not None:
                bf_ref[:, sl] = k.astype(BF16)
        rows_ref[:, 256:512] = y[:, 256:512]
        if bf_ref is not None:
            bf_ref[:, 256:512] = y[:, 256:512].astype(BF16)
    gate_ref[...] = jax.nn.sigmoid(_dot(h, w_ref[:, 2560:2688]))


def _proj1(x, norm_g, w_in, q_g, k_g, tabs, tm):
    t, d = x.shape
    tab_blocks = tabs[0].shape[0] // tm
    row = lambda i: (i, 0)
    fixed = lambda i: (0, 0)
    tab = lambda i: (i % tab_blocks, 0)
    w = jnp.pad(w_in, ((0, 0), (0, 2688 - w_in.shape[1]))).astype(BF16)
    rep = LANES // HEAD_DIM
    outs = [jax.ShapeDtypeStruct((t, 1024), BF16), jax.ShapeDtypeStruct((t, 512), F32),
            jax.ShapeDtypeStruct((t, 512), F32), jax.ShapeDtypeStruct((t, 512), F32),
            jax.ShapeDtypeStruct((t, LANES), F32), jax.ShapeDtypeStruct((t, 512), BF16),
            jax.ShapeDtypeStruct((t, 512), BF16)]
    return pl.pallas_call(
        _proj1_kernel,
        grid=(t // tm,),
        in_specs=[pl.BlockSpec((tm, d), row), pl.BlockSpec((1, d), fixed), pl.BlockSpec(w.shape, fixed),
                  pl.BlockSpec((1, LANES), fixed), pl.BlockSpec((3, LANES), fixed),
                  pl.BlockSpec((tm, LANES), tab), pl.BlockSpec((tm, LANES), tab), pl.BlockSpec((tm, LANES), tab)],
        out_specs=[pl.BlockSpec((tm, o.shape[1]), row) for o in outs],
        out_shape=outs,
        compiler_params=_cparams(("arbitrary",)),
        name="proj1",
    )(x, norm_g.reshape(1, d), w, jnp.tile(q_g.reshape(1, HEAD_DIM), (1, rep)), jnp.tile(k_g, (1, rep)), *tabs)


def _gelu_tanh(x):
    return 0.5 * x * (1.0 + jnp.tanh(0.7978845608028654 * (x + 0.044715 * x * x * x)))


def _compress_kernel(*refs, n_in, n_prefetch):
    refs = refs[n_prefetch:]
    row_refs = refs[:n_in]
    posv_ref, w1_ref, w2_ref, out_ref, carry_ref = refs[n_in:]
    t = pl.program_id(1)
    paged = len(row_refs[0].shape) == 5
    chunks = row_refs[0].shape[3] // CMP_STRIDE if paged else row_refs[0].shape[1]
    m_rows = chunks * n_in
    first = _iota((m_rows, 1), 0) == 0
    for i in range(2):
        for p in range(2):
            cols = []
            for l in range(CMP_STRIDE):
                c0 = l * 512 + i * 256 + p * LANES
                if paged:
                    pieces = [r[0, i, p, pl.ds(l, chunks, stride=CMP_STRIDE), :] for r in row_refs]
                else:
                    pieces = [r[0, :, c0:c0 + LANES] for r in row_refs]
                cols.append(pieces[0] if n_in == 1 else jnp.concatenate(pieces, axis=0))
            xcat = jnp.concatenate(cols, axis=1)
            a = _dot((xcat + posv_ref[i, 0]).astype(BF16), w1_ref[i, 0])
            b = _dot((xcat + posv_ref[i, 1]).astype(BF16), w1_ref[i, 1])
            prev = jnp.where(t == 0, jnp.zeros((1, 512), F32), carry_ref[2 * i + p, 0:1, :])
            a_prev = jnp.where(first, prev, pltpu.roll(a, 1, 0))
            carry_ref[2 * i + p, 0:1, :] = a[m_rows - 1:m_rows, :]
            hid = _gelu_tanh(a_prev + b)
            out = _dot(hid.astype(BF16), w2_ref[i])
            out = jnp.where(first & (t == 0), 0.0, out)
            out_ref[0, :, i * 256 + p * LANES:i * 256 + (p + 1) * LANES] = out


def _compress_weights(cmp_pos, w1, w2):
    w1r = w1.reshape(2, 2, CMP_STRIDE, HEAD_DIM, w1.shape[-1])
    hid = w1.shape[-1]
    z = jnp.zeros_like(w1r)
    top = jnp.concatenate([w1r, z], axis=-1)
    bot = jnp.concatenate([z, w1r], axis=-1)
    w1bd = jnp.concatenate([top, bot], axis=3).reshape(2, 2, CMP_STRIDE * LANES, 2 * hid).astype(BF16)
    z2 = jnp.zeros_like(w2)
    w2bd = jnp.concatenate([jnp.concatenate([w2, z2], axis=-1), jnp.concatenate([z2, w2], axis=-1)],
                           axis=1).astype(BF16)
    pv = cmp_pos.reshape(2, CMP_STRIDE, 2, HEAD_DIM)
    pv = jnp.transpose(pv, (2, 0, 1, 3))
    posv = jnp.concatenate([pv, pv], axis=-1).reshape(2, 2, 1, CMP_STRIDE * LANES)
    return posv, w1bd, w2bd


def _compress_call(row_arrays, row_specs, grid, out_map, nb, n_chunks, rows_per_step, weights, prefetch=None):
    posv, w1bd, w2bd = weights
    n_in = len(row_arrays)
    nidx = 2 + (1 if prefetch is not None else 0)
    fix = lambda nd: (lambda *a: (0,) * nd)
    in_specs = list(row_specs) + [pl.BlockSpec(posv.shape, fix(4)), pl.BlockSpec(w1bd.shape, fix(4)),
                                  pl.BlockSpec(w2bd.shape, fix(3))]
    out_spec = pl.BlockSpec((1, rows_per_step, 512), out_map)
    scratch = [pltpu.VMEM((4, 8, 512), F32)]
    kern = functools.partial(_compress_kernel, n_in=n_in, n_prefetch=0 if prefetch is None else 1)
    out_shape = jax.ShapeDtypeStruct((nb, n_chunks, 512), F32)
    if prefetch is None:
        return pl.pallas_call(kern, grid=grid, in_specs=in_specs, out_specs=out_spec, out_shape=out_shape,
                              scratch_shapes=scratch, compiler_params=_cparams(("arbitrary", "arbitrary")),
                              name="compress_prompt")(*row_arrays, posv, w1bd, w2bd)
    gs = pltpu.PrefetchScalarGridSpec(num_scalar_prefetch=1, grid=grid, in_specs=in_specs, out_specs=out_spec,
                                      scratch_shapes=scratch)
    return pl.pallas_call(kern, grid_spec=gs, out_shape=out_shape,
                          compiler_params=_cparams(("arbitrary", "arbitrary")),
                          name="compress_decode")(prefetch, *row_arrays, posv, w1bd, w2bd)


def _compress_prompt(cmp_rows, b, s, weights):
    n_chunks = s // CMP_STRIDE
    y = cmp_rows.reshape(b, n_chunks, CMP_STRIDE * 512)
    spec = pl.BlockSpec((1, n_chunks, CMP_STRIDE * 512), lambda bi, t: (bi, 0, 0))
    return _compress_call([y], [spec], (b, 1), lambda bi, t: (bi, 0, 0), b, n_chunks, n_chunks, weights)


def _compress_decode(cache_cmp, page_table, weights, pages_per_step):
    nb, n_pages = page_table.shape
    n_pool, page = cache_cmp.shape[:2]
    cpp = page // CMP_STRIDE
    pp = pages_per_step
    y = jnp.swapaxes(jnp.transpose(cache_cmp, (0, 2, 3, 4, 1)).reshape(n_pool, 2, 2, LANES, page), -1, -2)
    specs = [pl.BlockSpec((1, 2, 2, page, LANES), functools.partial(
        lambda bi, t, pt, i: (pt[bi, t * pp + i], 0, 0, 0, 0), i=i)) for i in range(pp)]
    return _compress_call([y] * pp, specs, (nb, n_pages // pp), lambda bi, t, pt: (bi, t, 0), nb, n_pages * cpp,
                          pp * cpp, weights, prefetch=page_table)


def _overlap_table(n_entries, n_sel, width):
    start = (np.arange(n_entries)[:, None] - 1) * CMP_STRIDE
    j = np.arange(width)[None, :]
    ov = (start < (j + 1) * L_SEL) & (start + L_CMP > j * L_SEL) & (np.arange(n_entries)[:, None] >= 1) & (j < n_sel)
    return jnp.asarray(ov.astype(np.float32), dtype=BF16)


def _masked_softmax(s, mask):
    s = jnp.where(mask, s, NEG_INF)
    m = jnp.max(s, axis=-1, keepdims=True)
    m = jnp.where(m == NEG_INF, 0.0, m)
    e = jnp.where(mask, jnp.exp(s - m), 0.0)
    return e / jnp.maximum(jnp.sum(e, axis=-1, keepdims=True), 1e-30)


def _select_blocks(imp, qpos, n_sel):
    blk = _iota(imp.shape, 1)
    cur = qpos // L_SEL
    valid = blk * L_SEL <= qpos
    forced = (blk == 0) | (blk == cur) | (blk == cur - 1)
    score = jnp.where(valid, imp + jnp.where(forced, FORCE_BONUS, 0.0), NEG_INF)
    rank = jnp.zeros(imp.shape, F32)
    for k in range(n_sel):
        sk = score[:, k:k + 1]
        ahead = (sk > score) | ((sk == score) & (blk > k))
        rank = rank + jnp.where(ahead, 1.0, 0.0)
    return jnp.where(valid & (rank < N_SEL), 1.0, 0.0)


def _online_update(state, s, mask, v, v_keys_on_lanes=False):
    m_run, l_run, acc = state
    s = jnp.where(mask, s, NEG_INF)
    m_new = jnp.maximum(m_run, jnp.max(s, axis=-1, keepdims=True))
    m_safe = jnp.where(m_new == NEG_INF, 0.0, m_new)
    p = jnp.where(mask, jnp.exp(s - m_safe), 0.0)
    alpha = jnp.exp(m_run - m_safe)
    pv = _dot_nt(p.astype(BF16), v) if v_keys_on_lanes else _dot(p.astype(BF16), v)
    return (m_new, alpha * l_run + jnp.sum(p, axis=-1, keepdims=True), alpha * acc + pv)


def _online_init(rows, width):
    return (jnp.full((rows, 1), NEG_INF, F32), jnp.zeros((rows, 1), F32), jnp.zeros((rows, width), F32))


def _online_finish(state):
    _, l_run, acc = state
    return acc / jnp.maximum(l_run, 1e-30)


def _nsa_prompt_kernel(q_ref, gate_ref, kcvc_ref, selkv_ref, winkv_ref, ovl_ref, expand_ref, o_ref, *, tq, n_sel):
    qi = pl.program_id(1)
    q0 = qi * tq
    nc = kcvc_ref.shape[1]
    lo_mask, hi_mask = _lane_half_masks((tq, LANES))
    qpos1 = q0 + _iota((tq, 1), 0)
    qpos4 = jnp.concatenate([qpos1] * 4, axis=0)
    col4 = _iota((4 * tq, tq), 1)
    gate = gate_ref[0]
    lane_g = _iota((tq, LANES), 1)
    tile4 = lambda a: jnp.concatenate([a] * 4, axis=0)

    for g in range(4):
        p, gh = g // 2, g % 2
        keep = hi_mask if gh else lo_mask
        ksl = slice(p * LANES, (p + 1) * LANES)
        vsl = slice(256 + p * LANES, 256 + (p + 1) * LANES)
        qs = []
        for r in range(4):
            h = g * 4 + r
            qh = q_ref[0, :, (h // 2) * LANES:(h // 2 + 1) * LANES].astype(F32)
            if h % 2 != gh:
                qh = pltpu.roll(qh, HEAD_DIM, 1)
            qs.append(jnp.where(keep, qh, 0.0).astype(BF16))
        qst = jnp.concatenate(qs, axis=0)

        kc = kcvc_ref[0, :, ksl].astype(BF16)
        vc = kcvc_ref[0, :, vsl].astype(BF16)
        ment = _iota((4 * tq, nc), 1)
        cmask = (ment >= 1) & (ment * CMP_STRIDE + (L_CMP - CMP_STRIDE - 1) <= qpos4)
        p_c = _masked_softmax(_dot_nt(qst, kc), cmask)
        o_c = _dot(p_c.astype(BF16), vc)
        psum = p_c[0:tq] + p_c[tq:2 * tq] + p_c[2 * tq:3 * tq] + p_c[3 * tq:4 * tq]
        sel = lax.cond(q0 + tq <= N_SEL * L_SEL,
                       lambda ps: jnp.where(_iota((tq, LANES), 1) * L_SEL <= qpos1, 1.0, 0.0),
                       lambda ps: _select_blocks(_dot(ps.astype(BF16), ovl_ref[...]), qpos1, n_sel),
                       psum).astype(BF16)

        def sel_step(kb, state, diag):
            rows = pl.ds(pl.multiple_of(kb * tq, tq), tq)
            chosen = tile4(_dot(sel, expand_ref[kb])) > 0.5
            if diag:
                chosen = chosen & (q0 + col4 <= qpos4)
            return _online_update(state, _dot_nt(qst, selkv_ref[0, rows, ksl]), chosen, selkv_ref[0, rows, vsl])

        st = lax.fori_loop(0, qi, lambda kb, c: sel_step(kb, c, False), _online_init(4 * tq, LANES))
        o_s = _online_finish(sel_step(qi, st, True))

        def win_step(kb, state):
            rows = pl.ds(pl.multiple_of(kb * tq, tq), tq)
            dist = qpos4 - (kb * tq + col4)
            inside = (dist >= 0) & (dist < WINDOW)
            return _online_update(state, _dot_nt(qst, winkv_ref[0, rows, ksl]), inside, winkv_ref[0, rows, vsl])

        kb_lo = jnp.maximum(qi - (WINDOW + tq - 1) // tq, 0)
        o_w = _online_finish(lax.fori_loop(kb_lo, qi + 1, win_step, _online_init(4 * tq, LANES)))

        placed = []
        for r in range(4):
            h = g * 4 + r
            rs = slice(r * tq, (r + 1) * tq)
            gsel = lambda c: jnp.sum(jnp.where(lane_g == c, gate, 0.0), axis=-1, keepdims=True)
            oh = o_c[rs] * gsel(3 * h) + o_s[rs] * gsel(3 * h + 1) + o_w[rs] * gsel(3 * h + 2)
            if h % 2 != gh:
                oh = pltpu.roll(oh, HEAD_DIM, 1)
            placed.append(oh)
        for c in range(2):
            o_ref[0, :, (g * 2 + c) * LANES:(g * 2 + c + 1) * LANES] = jnp.where(lo_mask, placed[2 * c], placed[2 * c + 1])


def _nsa_prompt(q, gate, kcvc, selkv, winkv, b, s, tq):
    n_sel = -(-s // L_SEL)
    nc = kcvc.shape[1]
    ovl = _overlap_table(nc, n_sel, LANES)
    kpos = np.arange(s).reshape(s // tq, 1, tq)
    expand = jnp.asarray((kpos // L_SEL == np.arange(LANES).reshape(1, LANES, 1)).astype(np.float32), dtype=BF16)
    r3 = lambda a: a.reshape(b, s, a.shape[-1])
    tile = lambda w: pl.BlockSpec((1, tq, w), lambda bi, qi: (bi, qi, 0))
    full = lambda n, w: pl.BlockSpec((1, n, w), lambda bi, qi: (bi, 0, 0))
    out = pl.pallas_call(
        functools.partial(_nsa_prompt_kernel, tq=tq, n_sel=n_sel),
        grid=(b, s // tq),
        in_specs=[tile(1024), tile(LANES), full(nc, 512), full(s, 512), full(s, 512),
                  pl.BlockSpec(ovl.shape, lambda bi, qi: (0, 0)), pl.BlockSpec(expand.shape, lambda bi, qi: (0, 0, 0))],
        out_specs=tile(1024),
        out_shape=jax.ShapeDtypeStruct((b, s, 1024), F32),
        compiler_params=_cparams(("arbitrary", "arbitrary")),
        name="nsa_prompt",
    )(r3(q), r3(gate), kcvc, r3(selkv), r3(winkv), ovl, expand)
    return out.reshape(b * s, 1024)


def _nsa_decode_kernel(pt_ref, q_ref, gate_ref, kcvc_ref, ovl_ref, selnew_ref, winnew_ref, winnewf_ref, state_ref,
                       *rest, pages_per_step, n_sel, past_len):
    pp = pages_per_step
    kpages = rest[:pp]
    vpages = rest[pp:2 * pp]
    o_ref, winout_ref = rest[2 * pp:2 * pp + 2]
    s_ref, qbig_ref, sel_ref, oc_ref, m_ref, l_ref, acc_ref = rest[2 * pp + 2:]
    sweep = pl.program_id(1)
    j = pl.program_id(2)
    page = kpages[0].shape[-1]
    head_of_lane = _iota((16, 256), 1) // HEAD_DIM
    row16 = _iota((16, 256), 0)
    own = head_of_lane == row16 // 4
    spread = jnp.where((_iota((16, 16), 0) // 4) * 4 == _iota((16, 16), 1), 1.0, 0.0).astype(BF16)

    @pl.when((sweep == 0) & (j == 0))
    def _():
        tile_lanes = jnp.where(_iota((HEAD_DIM, 256), 0) == _iota((HEAD_DIM, 256), 1) % HEAD_DIM, 1.0, 0.0)
        qb = _dot(q_ref[0], tile_lanes.astype(BF16))
        qbig = jnp.where(own, qb, 0.0).astype(BF16)
        qbig_ref[...] = qbig
        nc = kcvc_ref.shape[1]
        ment = _iota((16, nc), 1)
        cmask = (ment >= 1) & (ment * CMP_STRIDE + (L_CMP - CMP_STRIDE - 1) <= past_len)
        p_c = _masked_softmax(_dot_nt(qbig, kcvc_ref[0, :, 0:256].astype(BF16)), cmask)
        oc_ref[...] = _dot(p_c.astype(BF16), kcvc_ref[0, :, 256:512].astype(BF16))
        pair = p_c + pltpu.roll(p_c, 15, 0)
        psum = pair + pltpu.roll(pair, 14, 0)
        imp = _dot(psum.astype(BF16), ovl_ref[...])
        sel_ref[...] = _select_blocks(imp, jnp.full((16, 1), past_len, jnp.int32), n_sel)
        m_ref[...] = jnp.full(m_ref.shape, NEG_INF, F32)

    qbig = qbig_ref[...]
    qf = qbig.astype(F32)
    sel = sel_ref[...].astype(BF16)
    blocks_per_page = page // L_SEL
    knew = selnew_ref[0]
    s_new = jnp.sum(qf * knew[:, 0:256].astype(F32), axis=-1, keepdims=True)

    def new_token_chosen():
        last = jnp.where(_iota((256, LANES), 0) == past_len // L_SEL, 1.0, 0.0).astype(BF16)
        return _dot(spread, _dot(sel, last).astype(BF16))[:, 0:1] > 0.5

    @pl.when(sweep == 0)
    def _():
        sel_heads = _dot(spread, sel)
        blk_lane = _iota((16, 256), 1)
        blk_in_page = _iota((16, page), 1) // L_SEL
        tops = []
        for i in range(pp):
            pg = j * pp + i
            chosen = jnp.zeros((16, page), F32)
            for c in range(blocks_per_page):
                col = jnp.sum(jnp.where(blk_lane == pg * blocks_per_page + c, sel_heads, 0.0), axis=-1, keepdims=True)
                chosen = jnp.where(blk_in_page == c, col, chosen)
            s = jnp.where(chosen > 0.5, _dot(qbig, kpages[i][0, 0].astype(BF16)), NEG_INF)
            s_ref[pg] = s
            tops.append(jnp.max(s, axis=-1, keepdims=True))
        m_ref[...] = jnp.maximum(m_ref[...], _tree(jnp.maximum, tops))

    @pl.when((sweep == 1) & (j == 0))
    def _():
        has_new = new_token_chosen()
        m = jnp.maximum(m_ref[...], jnp.where(has_new, s_new, NEG_INF))
        m = jnp.where(m == NEG_INF, 0.0, m)
        m_ref[...] = m
        e_new = jnp.where(has_new, jnp.exp(s_new - m), 0.0)
        total = lax.fori_loop(0, s_ref.shape[0],
                              lambda g, t: t + jnp.sum(jnp.exp(s_ref[g] - m), axis=-1, keepdims=True), e_new)
        l_ref[...] = jnp.maximum(total, 1e-30)
        acc_ref[...] = (e_new / l_ref[...]).astype(BF16).astype(F32) * knew[:, 256:512].astype(F32)

    @pl.when(sweep == 1)
    def _():
        m, l = m_ref[...], l_ref[...]
        acc_ref[...] += _tree(jnp.add, [_dot_nt((jnp.exp(s_ref[j * pp + i] - m) / l).astype(BF16),
                                                vpages[i][0, 0].astype(BF16)) for i in range(pp)])

    @pl.when((sweep == 1) & (j == pl.num_programs(2) - 1))
    def _():
        o_s = acc_ref[...]

        w_buf = state_ref.shape[-1]
        s_w = _dot(qbig, state_ref[0, 0].astype(BF16))
        wpos = past_len - w_buf + _iota(s_w.shape, 1)
        wmask = (past_len - wpos < WINDOW) & (wpos >= 0)
        wnew = winnew_ref[0]
        s_wn = jnp.sum(qf * wnew[:, 0:256].astype(F32), axis=-1, keepdims=True)
        mw = jnp.maximum(jnp.max(jnp.where(wmask, s_w, NEG_INF), axis=-1, keepdims=True), s_wn)
        e = jnp.where(wmask, jnp.exp(s_w - mw), 0.0)
        en = jnp.exp(s_wn - mw)
        denom = jnp.maximum(jnp.sum(e, axis=-1, keepdims=True) + en, 1e-30)
        o_w = (_dot_nt((e / denom).astype(BF16), state_ref[0, 1].astype(BF16))
               + (en / denom).astype(BF16).astype(F32) * wnew[:, 256:512].astype(F32))
        newest = _iota((256, w_buf), 1) == w_buf - 1
        for i in range(2):
            winout_ref[0, i] = jnp.where(newest, winnewf_ref[0, i], pltpu.roll(state_ref[0, i], w_buf - 1, 1))

        gate = jnp.broadcast_to(gate_ref[0], (16, LANES))
        lane = _iota((16, LANES), 1)
        hrow = _iota((16, LANES), 0)
        gsel = lambda br: jnp.sum(jnp.where(lane == 3 * hrow + br, gate, 0.0), axis=-1, keepdims=True)
        o = oc_ref[...] * gsel(0) + o_s * gsel(1) + o_w * gsel(2)
        fold = jnp.where(_iota((256, HEAD_DIM), 0) % HEAD_DIM == _iota((256, HEAD_DIM), 1), 1.0, 0.0).astype(BF16)
        o_ref[0] = _split3_dot(jnp.where(own, o, 0.0), fold)


def _nsa_decode(page_table, q, gate, kcvc, selkv_new, winkv_new, win_new, state, cache_sel, pages_per_step):
    nb, n_pages = page_table.shape
    n_pool, page = cache_sel.shape[:2]
    past_len = n_pages * page
    n_sel = -(-(past_len + 1) // L_SEL)
    assert n_sel <= 256
    pp = pages_per_step
    nc = kcvc.shape[1]
    w_buf = state.shape[1]
    ovl = _overlap_table(nc, n_sel, 256)
    keys_last = lambda a: jnp.transpose(a, (0, 2, 3, 4, 1)).reshape(a.shape[0], 2, 256, a.shape[1])
    cs = keys_last(cache_sel)
    n_steps = n_pages // pp
    per_b = lambda b, sw, j, pt: (b, 0, 0)
    per_b4 = lambda b, sw, j, pt: (b, 0, 0, 0)
    kspecs = [pl.BlockSpec((1, 1, 256, page), functools.partial(
        lambda b, sw, j, pt, i: (pt[b, jnp.where(sw == 0, j, n_steps - 1) * pp + i], 0, 0, 0), i=i)) for i in range(pp)]
    vspecs = [pl.BlockSpec((1, 1, 256, page), functools.partial(
        lambda b, sw, j, pt, i: (pt[b, jnp.where(sw == 1, j, 0) * pp + i], 1, 0, 0), i=i)) for i in range(pp)]
    grid_spec = pltpu.PrefetchScalarGridSpec(
        num_scalar_prefetch=1,
        grid=(nb, 2, n_steps),
        in_specs=[pl.BlockSpec((1, 16, HEAD_DIM), per_b), pl.BlockSpec((1, 1, LANES), per_b),
                  pl.BlockSpec((1, nc, 512), per_b), pl.BlockSpec(ovl.shape, lambda b, sw, j, pt: (0, 0)),
                  pl.BlockSpec((1, 1, 512), per_b), pl.BlockSpec((1, 1, 512), per_b), pl.BlockSpec((1, 2, 256, 1), per_b4),
                  pl.BlockSpec((1, 2, 256, w_buf), per_b4)] + kspecs + vspecs,
        out_specs=[pl.BlockSpec((1, 16, HEAD_DIM), per_b), pl.BlockSpec((1, 2, 256, w_buf), per_b4)],
        scratch_shapes=[pltpu.VMEM((n_pages, 16, page), F32),
                        pltpu.VMEM((16, 256), BF16), pltpu.VMEM((16, 256), F32), pltpu.VMEM((16, 256), F32),
                        pltpu.VMEM((16, 1), F32), pltpu.VMEM((16, 1), F32), pltpu.VMEM((16, 256), F32)],
    )
    o, win_out = pl.pallas_call(
        functools.partial(_nsa_decode_kernel, pages_per_step=pp, n_sel=n_sel, past_len=past_len),
        grid_spec=grid_spec,
        out_shape=[jax.ShapeDtypeStruct((nb, 16, HEAD_DIM), F32), jax.ShapeDtypeStruct((nb, 2, 256, w_buf), F32)],
        compiler_params=_cparams(("arbitrary", "arbitrary", "arbitrary")),
        name="nsa_decode",
    )(page_table, q.reshape(nb, 16, HEAD_DIM), gate.reshape(nb, 1, LANES), kcvc, ovl, selkv_new.reshape(nb, 1, 512),
      winkv_new.reshape(nb, 1, 512), win_new.reshape(nb, 2, 256, 1), keys_last(state), *([cs] * (2 * pp)))
    win_rows = jnp.transpose(win_out.reshape(nb, 2, 4, HEAD_DIM, w_buf), (0, 4, 1, 2, 3))
    return o.reshape(nb, 1024), win_rows


def _outproj_router_kernel(x_ref, o_ref, wo_ref, ng_ref, wr_ref, x1_ref, h_ref, comb_ref):
    x1 = x_ref[...] + _dot(o_ref[...].astype(BF16), wo_ref[...])
    x1_ref[...] = x1
    hf = _rms(x1, ng_ref[...])
    h = hf.astype(BF16)
    h_ref[...] = hf.astype(h_ref.dtype)
    logits = _dot(h, wr_ref[...])
    lane = _iota(logits.shape, 1)
    logits = jnp.where(lane < N_EXPERTS, logits, NEG_INF)
    m1 = jnp.max(logits, axis=-1, keepdims=True)
    i1 = jnp.min(jnp.where(logits == m1, lane, LANES), axis=-1, keepdims=True)
    rest = jnp.where(lane == i1, NEG_INF, logits)
    m2 = jnp.max(rest, axis=-1, keepdims=True)
    i2 = jnp.min(jnp.where(rest == m2, lane, LANES), axis=-1, keepdims=True)
    e2 = jnp.exp(m2 - m1)
    comb_ref[...] = jnp.where(lane == i1, 1.0 / (1.0 + e2), 0.0) + jnp.where(lane == i2, e2 / (1.0 + e2), 0.0)


def _outproj_router(x, o, w_out, norm_g, w_router, tm, h_dtype=BF16):
    t, d = x.shape
    row = lambda i: (i, 0)
    fixed = lambda i: (0, 0)
    wr = jnp.pad(w_router, ((0, 0), (0, LANES - w_router.shape[1]))).astype(BF16)
    outs = [jax.ShapeDtypeStruct((t, d), F32), jax.ShapeDtypeStruct((t, d), h_dtype), jax.ShapeDtypeStruct((t, LANES), F32)]
    return pl.pallas_call(
        _outproj_router_kernel,
        grid=(t // tm,),
        in_specs=[pl.BlockSpec((tm, d), row), pl.BlockSpec((tm, o.shape[1]), row), pl.BlockSpec(w_out.shape, fixed),
                  pl.BlockSpec((1, d), fixed), pl.BlockSpec(wr.shape, fixed)],
        out_specs=[pl.BlockSpec((tm, a.shape[1]), row) for a in outs],
        out_shape=outs,
        compiler_params=_cparams(("arbitrary",)),
        name="outproj_router",
    )(x, o, w_out.astype(BF16), norm_g.reshape(1, d), wr)


def _moe_kernel(x1_ref, h_ref, comb_ref, wg_ref, wu_ref, wd_ref, y_ref, acc_ref):
    e = pl.program_id(1)
    j = pl.program_id(2)

    @pl.when((e == 0) & (j == 0))
    def _():
        acc_ref[...] = jnp.zeros_like(acc_ref)

    h = h_ref[...]
    g = _dot(h, wg_ref[0])
    u = _dot(h, wu_ref[0])
    a = (g * jax.nn.sigmoid(g) * u).astype(BF16)
    comb = comb_ref[...]
    w_e = jnp.sum(jnp.where(_iota(comb.shape, 1) == e, comb, 0.0), axis=-1, keepdims=True)
    acc_ref[...] += w_e * _dot(a, wd_ref[0])

    @pl.when((e == pl.num_programs(1) - 1) & (j == pl.num_programs(2) - 1))
    def _():
        y_ref[...] = x1_ref[...] + acc_ref[...]


def _moe(x1, h, comb, w_gate_up, w_down, tm, tf):
    t, d = x1.shape
    n_e, ff = w_down.shape[:2]
    nj = ff // tf
    row = lambda i, e, j: (i, 0)
    wgu = w_gate_up.astype(BF16)
    return pl.pallas_call(
        _moe_kernel,
        grid=(t // tm, n_e, nj),
        in_specs=[pl.BlockSpec((tm, d), row), pl.BlockSpec((tm, d), row), pl.BlockSpec((tm, LANES), row),
                  pl.BlockSpec((1, d, tf), lambda i, e, j: (e, 0, j)),
                  pl.BlockSpec((1, d, tf), lambda i, e, j: (e, 0, nj + j)),
                  pl.BlockSpec((1, tf, d), lambda i, e, j: (e, j, 0))],
        out_specs=pl.BlockSpec((tm, d), row),
        out_shape=jax.ShapeDtypeStruct((t, d), F32),
        scratch_shapes=[pltpu.VMEM((tm, d), F32)],
        compiler_params=_cparams(("arbitrary", "arbitrary", "arbitrary")),
        name="moe",
    )(x1, h, comb, wgu, wgu, w_down.astype(BF16))


def _moe_plan(comb, tm):
    t = comb.shape[0]
    routed = comb[:, :N_EXPERTS] > 0.0
    r32 = routed.astype(jnp.int32)
    counts = jnp.sum(r32, axis=0)
    padded = ((counts + tm - 1) // tm) * tm
    ends = jnp.cumsum(padded)
    starts = ends - padded
    dest = starts[None, :] + jnp.cumsum(r32, axis=0) - r32
    n_tiles = (2 * t) // tm + N_EXPERTS
    tile_start = jnp.arange(n_tiles, dtype=jnp.int32) * tm
    tile_expert = jnp.minimum(jnp.sum((tile_start[:, None] >= ends[None, :]).astype(jnp.int32), axis=1),
                              N_EXPERTS - 1)
    tile_used = (tile_start < ends[-1]).astype(jnp.int32)
    first = jnp.argmax(routed, axis=1)
    last = N_EXPERTS - 1 - jnp.argmax(routed[:, ::-1], axis=1)
    row_a = jnp.take_along_axis(dest, first[:, None], axis=1)[:, 0]
    row_b = jnp.take_along_axis(dest, last[:, None], axis=1)[:, 0]
    return n_tiles, tile_expert, tile_used, row_a.astype(jnp.int32), row_b.astype(jnp.int32)


def _moe_scatter_kernel(ra_ref, rb_ref, h_ref, grouped_in, grouped_hbm, sem):
    tc = h_ref.shape[0]

    def start(r, c):
        pltpu.make_async_copy(h_ref.at[pl.ds(r, 1), :], grouped_hbm.at[pl.ds(ra_ref[0, 0, r], 1), :], sem).start()
        pltpu.make_async_copy(h_ref.at[pl.ds(r, 1), :], grouped_hbm.at[pl.ds(rb_ref[0, 0, r], 1), :], sem).start()
        return c

    lax.fori_loop(0, tc, start, 0, unroll=8)
    for _ in range(2):
        pltpu.make_async_copy(h_ref, grouped_hbm.at[pl.ds(0, tc), :], sem).wait()


def _moe_scatter(h, row_a, row_b, n_rows, tc):
    t, d = h.shape
    smem_row = pl.BlockSpec((1, 1, tc), lambda i: (i, 0, 0), memory_space=pltpu.SMEM)
    return pl.pallas_call(
        _moe_scatter_kernel,
        grid=(t // tc,),
        in_specs=[smem_row, smem_row, pl.BlockSpec((tc, d), lambda i: (i, 0)), pl.BlockSpec(memory_space=pl.ANY)],
        out_specs=pl.BlockSpec(memory_space=pl.ANY),
        out_shape=jax.ShapeDtypeStruct((n_rows, d), h.dtype),
        input_output_aliases={3: 0},
        scratch_shapes=[pltpu.SemaphoreType.DMA(())],
        compiler_params=_cparams(("arbitrary",)),
        name="moe_scatter",
    )(row_a.reshape(t // tc, 1, tc), row_b.reshape(t // tc, 1, tc), h, jnp.zeros((n_rows, d), h.dtype))


def _moe_rows_kernel(te_ref, tu_ref, x_ref, wg_ref, wu_ref, wd_ref, y_ref, acc_ref):
    i = pl.program_id(0)
    j = pl.program_id(1)
    used = tu_ref[i] > 0

    @pl.when(used & (j == 0))
    def _():
        acc_ref[...] = jnp.zeros_like(acc_ref)

    @pl.when(used)
    def _():
        h = x_ref[...].astype(BF16)
        g = _dot(h, wg_ref[0])
        u = _dot(h, wu_ref[0])
        acc_ref[...] += _dot((g * jax.nn.sigmoid(g) * u).astype(BF16), wd_ref[0])

    last = j == pl.num_programs(1) - 1

    @pl.when(used & last)
    def _():
        y_ref[...] = acc_ref[...]

    @pl.when(jnp.logical_not(used) & last)
    def _():
        y_ref[...] = jnp.zeros_like(y_ref)


def _moe_rows(grouped, tile_expert, tile_used, w_gate_up, w_down, tm, tf):
    d = grouped.shape[1]
    n_tiles = grouped.shape[0] // tm
    n_e, ff = w_down.shape[:2]
    nj = ff // tf
    wgu = w_gate_up.astype(BF16)
    grid_spec = pltpu.PrefetchScalarGridSpec(
        num_scalar_prefetch=2,
        grid=(n_tiles, nj),
        in_specs=[pl.BlockSpec((tm, d), lambda i, j, te, tu: (i, 0)),
                  pl.BlockSpec((1, d, tf), lambda i, j, te, tu: (te[i], 0, j)),
                  pl.BlockSpec((1, d, tf), lambda i, j, te, tu: (te[i], 0, nj + j)),
                  pl.BlockSpec((1, tf, d), lambda i, j, te, tu: (te[i], j, 0))],
        out_specs=pl.BlockSpec((tm, d), lambda i, j, te, tu: (i, 0)),
        scratch_shapes=[pltpu.VMEM((tm, d), F32)],
    )
    return pl.pallas_call(
        _moe_rows_kernel,
        grid_spec=grid_spec,
        out_shape=jax.ShapeDtypeStruct((n_tiles * tm, d), F32),
        compiler_params=_cparams(("arbitrary", "arbitrary")),
        name="moe_rows",
    )(tile_expert, tile_used, grouped, wgu, wgu, w_down.astype(BF16))


def _moe_combine_kernel(ra_ref, rb_ref, x1_ref, comb_ref, ys_hbm, out_ref, buf_ref, sem):
    tc = x1_ref.shape[0]

    def row_copy(slot, r, src):
        return pltpu.make_async_copy(ys_hbm.at[pl.ds(src, 1), :], buf_ref.at[slot, pl.ds(r, 1), :], sem)

    def start(r, c):
        row_copy(0, r, ra_ref[0, 0, r]).start()
        row_copy(1, r, rb_ref[0, 0, r]).start()
        return c

    lax.fori_loop(0, tc, start, 0, unroll=8)
    for slot in range(2):
        pltpu.make_async_copy(ys_hbm.at[pl.ds(0, tc), :], buf_ref.at[slot], sem).wait()
    comb = comb_ref[...]
    lane = _iota(comb.shape, 1)
    on = comb > 0.0
    ia = jnp.min(jnp.where(on, lane, LANES), axis=-1, keepdims=True)
    ib = jnp.max(jnp.where(on, lane, -1), axis=-1, keepdims=True)
    w_a = jnp.sum(jnp.where(lane == ia, comb, 0.0), axis=-1, keepdims=True)
    w_b = jnp.sum(jnp.where((lane == ib) & (ib != ia), comb, 0.0), axis=-1, keepdims=True)
    out_ref[...] = x1_ref[...] + (w_a * buf_ref[0] + w_b * buf_ref[1])


def _moe_combine(x1, comb, ys, row_a, row_b, tc):
    t, d = x1.shape
    row = lambda i: (i, 0)
    smem_row = pl.BlockSpec((1, 1, tc), lambda i: (i, 0, 0), memory_space=pltpu.SMEM)
    return pl.pallas_call(
        _moe_combine_kernel,
        grid=(t // tc,),
        in_specs=[smem_row, smem_row, pl.BlockSpec((tc, d), row), pl.BlockSpec((tc, LANES), row),
                  pl.BlockSpec(memory_space=pl.ANY)],
        out_specs=pl.BlockSpec((tc, d), row),
        out_shape=jax.ShapeDtypeStruct((t, d), F32),
        scratch_shapes=[pltpu.VMEM((2, tc, d), F32), pltpu.SemaphoreType.DMA(())],
        compiler_params=_cparams(("arbitrary",)),
        name="moe_combine",
    )(row_a.reshape(t // tc, 1, tc), row_b.reshape(t // tc, 1, tc), x1, comb, ys)


def _moe_grouped(x1, h, comb, w_gate_up, w_down, tm, tf, tc):
    n_tiles, tile_expert, tile_used, row_a, row_b = _moe_plan(comb, tm)
    grouped = _moe_scatter(h, row_a, row_b, n_tiles * tm, tc)
    ys = _moe_rows(grouped, tile_expert, tile_used, w_gate_up, w_down, tm, tf)
    return _moe_combine(x1, comb, ys, row_a, row_b, tc)


def kernel(x_prompt, x_sample, cache_l0_diff, cache_l0_sb, cache_l1_cmp, cache_l1_sel, state_l1_win, page_table, l0_norm_mix, l0_w_in, l0_diff_q_norm, l0_diff_k_norm, l0_diff_lambda, l0_diff_subln, l0_w_out, l0_norm_ffn, l0_ffn_w_gate_up, l0_ffn_w_down, l1_norm_mix, l1_w_in, l1_q_norm, l1_k_norm, l1_cmp_pos, l1_cmp_w1, l1_cmp_w2, l1_w_out, l1_norm_ffn, l1_router, l1_moe_w_gate_up, l1_moe_w_down):
    b, s, d = x_prompt.shape
    nb = x_sample.shape[0]
    n_pages, page = page_table.shape[1], cache_l0_diff.shape[1]
    past_len = n_pages * page
    tm = min(512, s)
    tq = min(256, s)
    dec_pages = min(16, n_pages)
    cmp_pages = min(32, n_pages)
    ff_tile = l0_ffn_w_down.shape[0] // 2
    moe_tile = l1_moe_w_down.shape[1] // 7
    tabs_p = _rope_tables(jnp.arange(s, dtype=jnp.int32))
    tabs_s = _rope_tables(jnp.full((nb,), past_len, jnp.int32))
    xp = x_prompt.reshape(b * s, d)
    xs = x_sample.reshape(nb, d)

    dq, diff_p, sq, sb_p, dkv, skv = _proj0(xp, l0_norm_mix, l0_w_in, l0_diff_q_norm, l0_diff_k_norm, tabs_p, tm)
    o = _attn0_prompt(l0_diff_lambda, l0_diff_subln, dq, sq, dkv, skv, b, s, tq)
    xp = _outproj_ffn(xp, o, l0_w_out, l0_norm_ffn, l0_ffn_w_gate_up, l0_ffn_w_down, tm, ff_tile)
    dq, diff_s, sq, sb_s, dkv, skv = _proj0(xs, l0_norm_mix, l0_w_in, l0_diff_q_norm, l0_diff_k_norm, tabs_s, nb)
    o = _attn0_decode(page_table, l0_diff_lambda, l0_diff_subln, dq, sq, dkv, cache_l0_diff, cache_l0_sb, dec_pages)
    xs = _outproj_ffn(xs, o, l0_w_out, l0_norm_ffn, l0_ffn_w_gate_up, l0_ffn_w_down, nb, ff_tile)

    cw = _compress_weights(l1_cmp_pos, l1_cmp_w1, l1_cmp_w2)
    q, cmp_p, sel_p, win_p, gate, selkv, winkv = _proj1(xp, l1_norm_mix, l1_w_in, l1_q_norm, l1_k_norm, tabs_p, tm)
    kcvc = _compress_prompt(cmp_p, b, s, cw)
    o = _nsa_prompt(q, gate, kcvc, selkv, winkv, b, s, tq)
    x1, h, comb = _outproj_router(xp, o, l1_w_out, l1_norm_ffn, l1_router, tm, h_dtype=F32)
    xp = _moe_grouped(x1, h, comb, l1_moe_w_gate_up, l1_moe_w_down, tm, l1_moe_w_down.shape[1] // 2, min(256, s))

    q, cmp_s, sel_s, win_s_new, gate, selkv, winkv = _proj1(xs, l1_norm_mix, l1_w_in, l1_q_norm, l1_k_norm, tabs_s, nb)
    kcvc = _compress_decode(cache_l1_cmp, page_table, cw, cmp_pages)
    o, win_s = _nsa_decode(page_table, q, gate, kcvc, selkv, winkv, win_s_new, state_l1_win, cache_l1_sel, dec_pages)
    x1, h, comb = _outproj_router(xs, o, l1_w_out, l1_norm_ffn, l1_router, nb)
    xs = _moe(x1, h, comb, l1_moe_w_gate_up, l1_moe_w_down, nb, moe_tile)

    w_keep = min(WINDOW, s)
    hk = cmp_p.shape[-1] // (2 * HEAD_DIM)
    rows5 = lambda a, n: a.reshape(n, -1, 2, a.shape[-1] // (2 * HEAD_DIM), HEAD_DIM)
    return (xp.reshape(b, s, d), xs.reshape(nb, 1, d),
            diff_p.reshape(b, s, 2, 4, 2 * HEAD_DIM), sb_p.reshape(b, s, 2, 8, HEAD_DIM),
            rows5(cmp_p, b), rows5(sel_p, b), rows5(win_p, b)[:, s - w_keep:],
            diff_s.reshape(nb, 1, 2, 4, 2 * HEAD_DIM), sb_s.reshape(nb, 1, 2, 8, HEAD_DIM),
            rows5(cmp_s, nb), rows5(sel_s, nb), win_s)
```

```python
import functools

import numpy as np
import jax
import jax.numpy as jnp
from jax import lax
from jax.experimental import pallas as pl
from jax.experimental.pallas import tpu as pltpu

F32 = jnp.float32
BF16 = jnp.bfloat16

HEAD_DIM = 64
ROT_DIM = HEAD_DIM // 4
ROPE_THETA = 500000.0
EPS = 1e-6
DIFF_LAMBDA_INIT = 0.2
L_CMP = 32
CMP_STRIDE = 16
L_SEL = 64
N_SEL = 16
WINDOW = 512
FORCE_BONUS = 1.0e4
N_EXPERTS = 8
LANES = 128
VMEM_LIMIT = 56 * 1024 * 1024
NEG_INF = float("-inf")
SB_LOG_WEIGHT_FLOOR = -110.0


def _cparams(sem):
    return pltpu.CompilerParams(dimension_semantics=sem, vmem_limit_bytes=VMEM_LIMIT)


def _dot(a, b):
    return jnp.dot(a, b, preferred_element_type=F32)


def _dot_nt(a, b):
    return lax.dot_general(a, b, (((1,), (1,)), ((), ())), preferred_element_type=F32)


def _split2_dot(x, m):
    hi = x.astype(BF16)
    lo = (x - hi.astype(F32)).astype(BF16)
    return _dot(hi, m) + _dot(lo, m)


def _split3_dot(x, m):
    hi = x.astype(BF16)
    r1 = x - hi.astype(F32)
    mid = r1.astype(BF16)
    lo = (r1 - mid.astype(F32)).astype(BF16)
    return _dot(hi, m) + _dot(mid, m) + _dot(lo, m)


def _tree(op, xs):
    xs = list(xs)
    while len(xs) > 1:
        xs = [op(a, b) for a, b in zip(xs[0::2], xs[1::2])] + ([xs[-1]] if len(xs) % 2 else [])
    return xs[0]


def _iota(shape, dim):
    return lax.broadcasted_iota(jnp.int32, shape, dim)


def _group_ones(n, group):
    r = _iota((n, n), 0) // group
    c = _iota((n, n), 1) // group
    return jnp.where(r == c, 1.0, 0.0).astype(BF16)


def _rms(x, g):
    return x * lax.rsqrt(jnp.mean(x * x, axis=-1, keepdims=True) + EPS) * g


def _head_norm_rope(xc, g, ones64, c, sa, sb):
    ms = _split2_dot(xc * xc, ones64) * (1.0 / HEAD_DIM)
    xn = xc * lax.rsqrt(ms + EPS) * g
    return xn * c + pltpu.roll(xn, LANES - ROT_DIM // 2, 1) * sa + pltpu.roll(xn, ROT_DIM // 2, 1) * sb


def _rope_tables(pos):
    half = ROT_DIM // 2
    inv = ROPE_THETA ** (-jnp.arange(half, dtype=F32) * 2.0 / ROT_DIM)
    ang = pos.astype(F32)[:, None] * inv[None, :]
    cos, sin = jnp.cos(ang), jnp.sin(ang)
    n = pos.shape[0]
    one = jnp.ones((n, HEAD_DIM - ROT_DIM), F32)
    zero_h = jnp.zeros((n, half), F32)
    zero_r = jnp.zeros((n, HEAD_DIM - ROT_DIM), F32)
    c = jnp.concatenate([cos, cos, one], axis=1)
    sa = jnp.concatenate([-sin, zero_h, zero_r], axis=1)
    sb = jnp.concatenate([zero_h, sin, zero_r], axis=1)
    tile = lambda t: jnp.concatenate([t, t], axis=1)
    return tile(c), tile(sa), tile(sb)


def _proj0_kernel(x_ref, ng_ref, w_ref, qg_ref, kg_ref, c_ref, sa_ref, sb_ref,
                  dq_ref, drows_ref, sq_ref, srows_ref, dkv_ref, skv_ref):
    h = _rms(x_ref[...], ng_ref[...]).astype(BF16)
    ones64 = _group_ones(LANES, HEAD_DIM)
    c, sa, sb = c_ref[...], sa_ref[...], sb_ref[...]
    scale = HEAD_DIM ** -0.5
    y = _dot(h, w_ref[:, 0:512])
    for j in range(4):
        sl = slice(j * LANES, (j + 1) * LANES)
        dq_ref[:, sl] = (_head_norm_rope(y[:, sl], qg_ref[...], ones64, c, sa, sb) * scale).astype(BF16)
    y = _dot(h, w_ref[:, 512:1024])
    for j in range(4):
        sl = slice(j * LANES, (j + 1) * LANES)
        k = _head_norm_rope(y[:, sl], kg_ref[...], ones64, c, sa, sb)
        drows_ref[:, sl] = k
        dkv_ref[:, sl] = k.astype(BF16)
    y = _dot(h, w_ref[:, 1024:1536])
    drows_ref[:, 512:1024] = y
    dkv_ref[:, 512:1024] = y.astype(BF16)
    y = _dot(h, w_ref[:, 1536:2048])
    sq_ref[...] = (y * scale).astype(BF16)
    y = _dot(h, w_ref[:, 2048:3072])
    srows_ref[...] = y
    skv_ref[...] = y.astype(BF16)


def _proj0(x, norm_g, w_in, q_g, k_g, tabs, tm):
    t, d = x.shape
    tab_blocks = tabs[0].shape[0] // tm
    row = lambda i: (i, 0)
    fixed = lambda i: (0, 0)
    tab = lambda i: (i % tab_blocks, 0)
    g2 = lambda g: jnp.tile(g.reshape(1, HEAD_DIM), (1, LANES // HEAD_DIM))
    outs = [jax.ShapeDtypeStruct((t, 512), BF16), jax.ShapeDtypeStruct((t, 1024), F32),
            jax.ShapeDtypeStruct((t, 512), BF16), jax.ShapeDtypeStruct((t, 1024), F32),
            jax.ShapeDtypeStruct((t, 1024), BF16), jax.ShapeDtypeStruct((t, 1024), BF16)]
    return pl.pallas_call(
        _proj0_kernel,
        grid=(t // tm,),
        in_specs=[pl.BlockSpec((tm, d), row), pl.BlockSpec((1, d), fixed),
                  pl.BlockSpec(w_in.shape, fixed),
                  pl.BlockSpec((1, LANES), fixed), pl.BlockSpec((1, LANES), fixed),
                  pl.BlockSpec((tm, LANES), tab), pl.BlockSpec((tm, LANES), tab), pl.BlockSpec((tm, LANES), tab)],
        out_specs=[pl.BlockSpec((tm, o.shape[1]), row) for o in outs],
        out_shape=outs,
        compiler_params=_cparams(("arbitrary",)),
        name="proj0",
    )(x, norm_g.reshape(1, d), w_in.astype(BF16), g2(q_g), g2(k_g), *tabs)


def _lane_half_masks(shape):
    lane = _iota(shape, 1)
    return lane < HEAD_DIM, lane >= HEAD_DIM


def _diff_lambda(lp):
    a = jnp.sum(lp[0:1] * lp[1:2], axis=-1, keepdims=True)
    b = jnp.sum(lp[2:3] * lp[3:4], axis=-1, keepdims=True)
    return jnp.exp(a) - jnp.exp(b) + DIFF_LAMBDA_INIT


def _attn0_kernel(lam_ref, subg_ref, dq_ref, sq_ref, dkv_ref, skv_ref, o_ref, *, tq):
    qi = pl.program_id(1)
    q0 = qi * tq
    lam = _diff_lambda(lam_ref[...])
    lo_mask, hi_mask = _lane_half_masks((tq, LANES))
    row = _iota((tq, tq), 0)
    col = _iota((tq, tq), 1)
    zero_bf = jnp.zeros((tq, LANES), BF16)

    def split_maps(ref):
        out = []
        for c in range(4):
            q = ref[0, :, c * LANES:(c + 1) * LANES]
            out += [jnp.where(lo_mask, q, zero_bf), jnp.where(hi_mask, q, zero_bf)]
        return out

    qd = split_maps(dq_ref)

    def step(kb, carry, diag):
        rows = pl.ds(pl.multiple_of(kb * tq, tq), tq)
        out = []
        for h in range(4):
            k = dkv_ref[0, rows, h * LANES:(h + 1) * LANES]
            v = dkv_ref[0, rows, 512 + h * LANES:512 + (h + 1) * LANES]
            for m in range(2):
                m_run, l_run, acc = carry[2 * h + m]
                s = _dot_nt(qd[2 * h + m], k)
                if diag:
                    s = jnp.where(col <= row, s, NEG_INF)
                m_new = jnp.maximum(m_run, jnp.max(s, axis=-1, keepdims=True))
                p = jnp.exp(s - m_new)
                alpha = jnp.exp(m_run - m_new)
                l_new = alpha * l_run + jnp.sum(p, axis=-1, keepdims=True)
                out.append((m_new, l_new, alpha * acc + _dot(p.astype(BF16), v)))
        return tuple(out)

    init = tuple((jnp.full((tq, 1), NEG_INF, F32), jnp.zeros((tq, 1), F32), jnp.zeros((tq, LANES), F32))
                 for _ in range(8))
    carry = lax.fori_loop(0, qi, lambda kb, c: step(kb, c, False), step(qi, init, True))
    for h in range(4):
        (_, l0, a0), (_, l1, a1) = carry[2 * h], carry[2 * h + 1]
        o = a0 / l0 - lam * (a1 / l1)
        sl = slice(h * LANES, (h + 1) * LANES)
        o_ref[0, :, sl] = _rms(o, subg_ref[:, sl]) * (1.0 - DIFF_LAMBDA_INIT)

    upper = jnp.where(row > col, 1.0, 0.0).astype(BF16)
    qs = split_maps(sq_ref)

    def sb_step(kb, carry, diag):
        rows = pl.ds(pl.multiple_of(kb * tq, tq), tq)
        out = []
        for c in range(4):
            k = skv_ref[0, rows, c * LANES:(c + 1) * LANES]
            v = skv_ref[0, rows, 512 + c * LANES:512 + (c + 1) * LANES]
            for m in range(2):
                tail, acc = carry[2 * c + m]
                z = _dot_nt(qs[2 * c + m], k)
                t = jnp.log1p(jnp.exp(-jnp.abs(z)))
                lsp = -(jnp.maximum(-z, 0.0) + t)
                l1m = -(jnp.maximum(z, 0.0) + t)
                if diag:
                    l1m = jnp.where(col < row, l1m, 0.0)
                w = jnp.exp(lsp + _split2_dot(l1m, upper) + tail)
                if diag:
                    w = jnp.where(col < row, w, 0.0)
                out.append((tail + jnp.sum(l1m, axis=-1, keepdims=True), acc + _dot(w.astype(BF16), v)))
        return tuple(out)

    def still_visible(carry):
        return jnp.max(_tree(jnp.maximum, [t for t, _ in carry])) > SB_LOG_WEIGHT_FLOOR

    def sb_body(c):
        kb, _, carry = c
        carry = sb_step(kb, carry, False)
        return kb - 1, still_visible(carry), carry

    init = tuple((jnp.zeros((tq, 1), F32), jnp.zeros((tq, LANES), F32)) for _ in range(8))
    first = sb_step(qi, init, True)
    _, _, carry = lax.while_loop(lambda c: (c[0] >= 0) & c[1], sb_body, (qi - 1, still_visible(first), first))
    for c in range(4):
        o_ref[0, :, 512 + c * LANES:512 + (c + 1) * LANES] = jnp.where(lo_mask, carry[2 * c][1], carry[2 * c + 1][1])


def _attn0_prompt(lam_p, subln_g, dq, sq, dkv, skv, b, s, tq):
    r3 = lambda a: a.reshape(b, s, a.shape[-1])
    qspec = pl.BlockSpec((1, tq, 512), lambda bi, qi: (bi, qi, 0))
    kvspec = pl.BlockSpec((1, s, 1024), lambda bi, qi: (bi, 0, 0))
    fixed = lambda bi, qi: (0, 0)
    out = pl.pallas_call(
        functools.partial(_attn0_kernel, tq=tq),
        grid=(b, s // tq),
        in_specs=[pl.BlockSpec((4, HEAD_DIM), fixed), pl.BlockSpec((1, 512), fixed), qspec, qspec, kvspec, kvspec],
        out_specs=pl.BlockSpec((1, tq, 1024), lambda bi, qi: (bi, qi, 0)),
        out_shape=jax.ShapeDtypeStruct((b, s, 1024), F32),
        compiler_params=_cparams(("arbitrary", "arbitrary")),
        name="attn0_prompt",
    )(lam_p, subln_g.reshape(1, 512), r3(dq), r3(sq), r3(dkv), r3(skv))
    return out.reshape(b * s, 1024)


def _outproj_ffn_kernel(x_ref, o_ref, wo_ref, ng_ref, wg_ref, wu_ref, wd_ref, y_ref, x1_ref, h_ref, acc_ref):
    j = pl.program_id(1)

    @pl.when(j == 0)
    def _():
        x1 = x_ref[...] + _dot(o_ref[...].astype(BF16), wo_ref[...])
        x1_ref[...] = x1
        h_ref[...] = _rms(x1, ng_ref[...]).astype(BF16)
        acc_ref[...] = jnp.zeros_like(acc_ref)

    h = h_ref[...]
    g = _dot(h, wg_ref[...])
    u = _dot(h, wu_ref[...])
    a = (g * jax.nn.sigmoid(g) * u).astype(BF16)
    acc_ref[...] += _dot(a, wd_ref[...])

    @pl.when(j == pl.num_programs(1) - 1)
    def _():
        y_ref[...] = x1_ref[...] + acc_ref[...]


def _outproj_ffn(x, o, w_out, norm_g, w_gate_up, w_down, tm, tf):
    t, d = x.shape
    ff = w_down.shape[0]
    nj = ff // tf
    row = lambda i, j: (i, 0)
    fixed = lambda i, j: (0, 0)
    wgu = w_gate_up.astype(BF16)
    return pl.pallas_call(
        _outproj_ffn_kernel,
        grid=(t // tm, nj),
        in_specs=[pl.BlockSpec((tm, d), row), pl.BlockSpec((tm, o.shape[1]), row),
                  pl.BlockSpec(w_out.shape, fixed), pl.BlockSpec((1, d), fixed),
                  pl.BlockSpec((d, tf), lambda i, j: (0, j)), pl.BlockSpec((d, tf), lambda i, j: (0, nj + j)),
                  pl.BlockSpec((tf, d), lambda i, j: (j, 0))],
        out_specs=pl.BlockSpec((tm, d), row),
        out_shape=jax.ShapeDtypeStruct((t, d), F32),
        scratch_shapes=[pltpu.VMEM((tm, d), F32), pltpu.VMEM((tm, d), BF16), pltpu.VMEM((tm, d), F32)],
        compiler_params=_cparams(("arbitrary", "arbitrary")),
        name="outproj_ffn",
    )(x, o, w_out.astype(BF16), norm_g.reshape(1, d), wgu, wgu, w_down.astype(BF16))


def _rows_select(x, group, nrows):
    w = x.shape[-1]
    keep = _iota((nrows, w), 1) // group == _iota((nrows, w), 0)
    return jnp.where(keep, jnp.broadcast_to(x.astype(F32), (nrows, w)), 0.0).astype(x.dtype)


def _fold_chunks(x):
    return x[:, 0:LANES] + x[:, LANES:2 * LANES] + x[:, 2 * LANES:3 * LANES] + x[:, 3 * LANES:4 * LANES]


def _attn0_decode_kernel(pt_ref, lam_ref, subg_ref, dq_ref, sq_ref, dnew_ref, *rest, pages_per_step):
    pp = pages_per_step
    kpages = rest[:pp]
    vpages = rest[pp:2 * pp]
    spages = rest[2 * pp:3 * pp]
    od_ref, os_ref = rest[3 * pp:3 * pp + 2]
    s_ref, md_ref, ld_ref, accd_ref, tail_ref, accs_ref = rest[3 * pp + 2:]
    sweep = pl.program_id(1)
    j = pl.program_id(2)
    n_steps = pl.num_programs(2)
    qd = _rows_select(dq_ref[0], HEAD_DIM, 8)
    qd128 = _fold_chunks(qd.astype(F32)).astype(BF16)
    page = kpages[0].shape[1]
    nrow = page * 4
    rows_of = lambda ref: ref[0, :, 0].reshape(nrow, LANES).astype(BF16)
    new = dnew_ref[0]
    s_new = jnp.sum(qd.astype(F32) * new[:, 0:512].astype(F32), axis=-1, keepdims=True)
    lam = _diff_lambda(lam_ref[...])
    even_row = _iota((8, 1), 0) % 2 == 0

    def combine(pn):
        return jnp.where(even_row, pn - lam * pltpu.roll(pn, 7, 0), 0.0)

    @pl.when((sweep == 0) & (j == 0))
    def _():
        md_ref[...] = s_new
        tail_ref[...] = jnp.zeros_like(tail_ref)
        accs_ref[...] = jnp.zeros_like(accs_ref)

    @pl.when(sweep == 0)
    def _():
        qs = _rows_select(sq_ref[0], HEAD_DIM, 8)
        upper = jnp.where(_iota((page, page), 0) > _iota((page, page), 1), 1.0, 0.0).astype(BF16)
        is_key_row = _iota((8, nrow), 1) % 4 == _iota((8, nrow), 0) // 2
        tops = []
        for i in range(pp):
            s = jnp.where(is_key_row, _dot_nt(qd128, rows_of(kpages[i])), NEG_INF)
            s_ref[j * pp + i] = s
            tops.append(jnp.max(s, axis=-1, keepdims=True))
        md_ref[...] = jnp.maximum(md_ref[...], _tree(jnp.maximum, tops))

        z = jnp.concatenate([_dot(qs, spages[i][0, 0].astype(BF16)) for i in range(pp)], axis=0)
        t = jnp.log1p(jnp.exp(-jnp.abs(z)))
        lsp = -(jnp.maximum(-z, 0.0) + t)
        l1m = -(jnp.maximum(z, 0.0) + t)
        within = _split3_dot(l1m, upper)
        totals = jnp.sum(l1m, axis=-1, keepdims=True)
        run = tail_ref[...]
        tails = []
        for i in range(pp):
            tails.append(run)
            run = run + totals[i * 8:(i + 1) * 8]
        tail_ref[...] = run
        w = jnp.exp(lsp + within + jnp.concatenate(tails, axis=0)).astype(BF16)
        accs_ref[...] += _tree(jnp.add, [_dot_nt(w[i * 8:(i + 1) * 8], spages[i][0, 1].astype(BF16))
                                         for i in range(pp)])

    @pl.when((sweep == 1) & (j == 0))
    def _():
        m = md_ref[...]
        total = lax.fori_loop(0, s_ref.shape[0],
                              lambda g, t: t + jnp.sum(jnp.exp(s_ref[g] - m), axis=-1, keepdims=True),
                              jnp.exp(s_new - m))
        ld_ref[...] = jnp.maximum(total, 1e-30)
        own_head = _iota((8, 512), 1) // LANES == _iota((8, 512), 0) // 2
        v_new = _fold_chunks(jnp.where(own_head, jnp.broadcast_to(new[:, 512:1024].astype(F32), (8, 512)), 0.0))
        p_new = combine(jnp.exp(s_new - m) / ld_ref[...])
        accd_ref[...] = p_new.astype(BF16).astype(F32) * v_new

    @pl.when(sweep == 1)
    def _():
        m, l = md_ref[...], ld_ref[...]
        parts = []
        for i in range(pp):
            pc = combine(jnp.exp(s_ref[j * pp + i] - m) / l)
            parts.append(_dot(pc.astype(BF16), rows_of(vpages[i])))
        accd_ref[...] += _tree(jnp.add, parts)

    @pl.when((sweep == 1) & (j == n_steps - 1))
    def _():
        od_ref[0] = _rms(accd_ref[...], subg_ref[...]) * (1.0 - DIFF_LAMBDA_INIT)
        acs = accs_ref[...]
        os_ref[0] = jnp.sum(jnp.where(_iota(acs.shape, 0) == _iota(acs.shape, 1) // HEAD_DIM, acs, 0.0), axis=0,
                            keepdims=True)


def _attn0_decode(page_table, lam_p, subln_g, dq, sq, dkv, cache_diff, cache_sb, pages_per_step):
    nb, n_pages = page_table.shape
    n_pool, page = cache_diff.shape[:2]
    pp = pages_per_step
    cs = jnp.transpose(cache_sb, (0, 2, 3, 4, 1)).reshape(n_pool, 2, 512, page)
    n_steps = n_pages // pp
    fixed = lambda b, sw, j, pt: (0, 0)
    per_b = lambda b, sw, j, pt: (b, 0, 0)
    newest_first = lambda i: (lambda b, j, pt: pt[b, n_pages - 1 - (j * pp + i)])

    def kspec(i):
        pick = newest_first(i)
        return pl.BlockSpec((1, page, 1, 4, LANES),
                            lambda b, sw, j, pt: (pick(b, jnp.where(sw == 0, j, n_steps - 1), pt), 0, 0, 0, 0))

    def vspec(i):
        pick = newest_first(i)
        return pl.BlockSpec((1, page, 1, 4, LANES),
                            lambda b, sw, j, pt: (pick(b, jnp.where(sw == 1, j, 0), pt), 0, 1, 0, 0))

    def sspec(i):
        pick = newest_first(i)
        return pl.BlockSpec((1, 2, 512, page),
                            lambda b, sw, j, pt: (pick(b, jnp.where(sw == 0, j, n_steps - 1), pt), 0, 0, 0))

    grid_spec = pltpu.PrefetchScalarGridSpec(
        num_scalar_prefetch=1,
        grid=(nb, 2, n_steps),
        in_specs=[pl.BlockSpec((4, HEAD_DIM), fixed), pl.BlockSpec((8, LANES), fixed),
                  pl.BlockSpec((1, 1, 512), per_b), pl.BlockSpec((1, 1, 512), per_b),
                  pl.BlockSpec((1, 1, 1024), per_b)]
                 + [kspec(i) for i in range(pp)] + [vspec(i) for i in range(pp)] + [sspec(i) for i in range(pp)],
        out_specs=[pl.BlockSpec((1, 8, LANES), per_b), pl.BlockSpec((1, 1, 512), per_b)],
        scratch_shapes=[pltpu.VMEM((n_pages, 8, page * 4), F32),
                        pltpu.VMEM((8, 1), F32), pltpu.VMEM((8, 1), F32), pltpu.VMEM((8, LANES), F32),
                        pltpu.VMEM((8, 1), F32), pltpu.VMEM((8, 512), F32)],
    )
    od, osb = pl.pallas_call(
        functools.partial(_attn0_decode_kernel, pages_per_step=pp),
        grid_spec=grid_spec,
        out_shape=[jax.ShapeDtypeStruct((nb, 8, LANES), F32), jax.ShapeDtypeStruct((nb, 1, 512), F32)],
        compiler_params=_cparams(("arbitrary", "arbitrary", "arbitrary")),
        name="attn0_decode",
    )(page_table, lam_p, jnp.repeat(subln_g, 2, axis=0), dq.reshape(nb, 1, 512), sq.reshape(nb, 1, 512),
      dkv.reshape(nb, 1, 1024), *([cache_diff] * (2 * pp)), *([cs] * pp))
    return jnp.concatenate([od[:, 0::2, :].reshape(nb, 512), osb.reshape(nb, 512)], axis=-1)


def _proj1_kernel(x_ref, ng_ref, w_ref, qg_ref, kg_ref, c_ref, sa_ref, sb_ref,
                  q_ref, cmp_ref, sel_ref, win_ref, gate_ref, selkv_ref, winkv_ref):
    h = _rms(x_ref[...], ng_ref[...]).astype(BF16)
    ones64 = _group_ones(LANES, HEAD_DIM)
    c, sa, sb = c_ref[...], sa_ref[...], sb_ref[...]
    scale = HEAD_DIM ** -0.5
    for half in range(2):
        y = _dot(h, w_ref[:, half * 512:(half + 1) * 512])
        for j in range(4):
            sl = slice(j * LANES, (j + 1) * LANES)
            q = _head_norm_rope(y[:, sl], qg_ref[...], ones64, c, sa, sb) * scale
            q_ref[:, half * 512 + j * LANES:half * 512 + (j + 1) * LANES] = q.astype(BF16)
    for i, (rows_ref, bf_ref) in enumerate(((cmp_ref, None), (sel_ref, selkv_ref), (win_ref, winkv_ref))):
        y = _dot(h, w_ref[:, 1024 + i * 512:1024 + (i + 1) * 512])
        for j in range(2):
            sl = slice(j * LANES, (j + 1) * LANES)
            k = _head_norm_rope(y[:, sl], kg_ref[i:i + 1, :], ones64, c, sa, sb)
            rows_ref[:, sl] = k
            if bf_ref is not None:
                bf_ref[:, sl] = k.astype(BF16)
        rows_ref[:, 256:512] = y[:, 256:512]
        if bf_ref is not None:
            bf_ref[:, 256:512] = y[:, 256:512].astype(BF16)
    gate_ref[...] = jax.nn.sigmoid(_dot(h, w_ref[:, 2560:2688]))


def _proj1(x, norm_g, w_in, q_g, k_g, tabs, tm):
    t, d = x.shape
    tab_blocks = tabs[0].shape[0] // tm
    row = lambda i: (i, 0)
    fixed = lambda i: (0, 0)
    tab = lambda i: (i % tab_blocks, 0)
    w = jnp.pad(w_in, ((0, 0), (0, 2688 - w_in.shape[1]))).astype(BF16)
    rep = LANES // HEAD_DIM
    outs = [jax.ShapeDtypeStruct((t, 1024), BF16), jax.ShapeDtypeStruct((t, 512), F32),
            jax.ShapeDtypeStruct((t, 512), F32), jax.ShapeDtypeStruct((t, 512), F32),
            jax.ShapeDtypeStruct((t, LANES), F32), jax.ShapeDtypeStruct((t, 512), BF16),
            jax.ShapeDtypeStruct((t, 512), BF16)]
    return pl.pallas_call(
        _proj1_kernel,
        grid=(t // tm,),
        in_specs=[pl.BlockSpec((tm, d), row), pl.BlockSpec((1, d), fixed), pl.BlockSpec(w.shape, fixed),
                  pl.BlockSpec((1, LANES), fixed), pl.BlockSpec((3, LANES), fixed),
                  pl.BlockSpec((tm, LANES), tab), pl.BlockSpec((tm, LANES), tab), pl.BlockSpec((tm, LANES), tab)],
        out_specs=[pl.BlockSpec((tm, o.shape[1]), row) for o in outs],
        out_shape=outs,
        compiler_params=_cparams(("arbitrary",)),
        name="proj1",
    )(x, norm_g.reshape(1, d), w, jnp.tile(q_g.reshape(1, HEAD_DIM), (1, rep)), jnp.tile(k_g, (1, rep)), *tabs)


def _gelu_tanh(x):
    return 0.5 * x * (1.0 + jnp.tanh(0.7978845608028654 * (x + 0.044715 * x * x * x)))


def _compress_kernel(*refs, n_in, n_prefetch):
    refs = refs[n_prefetch:]
    row_refs = refs[:n_in]
    posv_ref, w1_ref, w2_ref, out_ref, carry_ref = refs[n_in:]
    t = pl.program_id(1)
    paged = len(row_refs[0].shape) == 5
    chunks = row_refs[0].shape[3] // CMP_STRIDE if paged else row_refs[0].shape[1]
    m_rows = chunks * n_in
    first = _iota((m_rows, 1), 0) == 0
    for i in range(2):
        for p in range(2):
            cols = []
            for l in range(CMP_STRIDE):
                c0 = l * 512 + i * 256 + p * LANES
                if paged:
                    pieces = [r[0, i, p, pl.ds(l, chunks, stride=CMP_STRIDE), :] for r in row_refs]
                else:
                    pieces = [r[0, :, c0:c0 + LANES] for r in row_refs]
                cols.append(pieces[0] if n_in == 1 else jnp.concatenate(pieces, axis=0))
            xcat = jnp.concatenate(cols, axis=1)
            a = _dot((xcat + posv_ref[i, 0]).astype(BF16), w1_ref[i, 0])
            b = _dot((xcat + posv_ref[i, 1]).astype(BF16), w1_ref[i, 1])
            prev = jnp.where(t == 0, jnp.zeros((1, 512), F32), carry_ref[2 * i + p, 0:1, :])
            a_prev = jnp.where(first, prev, pltpu.roll(a, 1, 0))
            carry_ref[2 * i + p, 0:1, :] = a[m_rows - 1:m_rows, :]
            hid = _gelu_tanh(a_prev + b)
            out = _dot(hid.astype(BF16), w2_ref[i])
            out = jnp.where(first & (t == 0), 0.0, out)
            out_ref[0, :, i * 256 + p * LANES:i * 256 + (p + 1) * LANES] = out


def _compress_weights(cmp_pos, w1, w2):
    w1r = w1.reshape(2, 2, CMP_STRIDE, HEAD_DIM, w1.shape[-1])
    hid = w1.shape[-1]
    z = jnp.zeros_like(w1r)
    top = jnp.concatenate([w1r, z], axis=-1)
    bot = jnp.concatenate([z, w1r], axis=-1)
    w1bd = jnp.concatenate([top, bot], axis=3).reshape(2, 2, CMP_STRIDE * LANES, 2 * hid).astype(BF16)
    z2 = jnp.zeros_like(w2)
    w2bd = jnp.concatenate([jnp.concatenate([w2, z2], axis=-1), jnp.concatenate([z2, w2], axis=-1)],
                           axis=1).astype(BF16)
    pv = cmp_pos.reshape(2, CMP_STRIDE, 2, HEAD_DIM)
    pv = jnp.transpose(pv, (2, 0, 1, 3))
    posv = jnp.concatenate([pv, pv], axis=-1).reshape(2, 2, 1, CMP_STRIDE * LANES)
    return posv, w1bd, w2bd


def _compress_call(row_arrays, row_specs, grid, out_map, nb, n_chunks, rows_per_step, weights, prefetch=None):
    posv, w1bd, w2bd = weights
    n_in = len(row_arrays)
    nidx = 2 + (1 if prefetch is not None else 0)
    fix = lambda nd: (lambda *a: (0,) * nd)
    in_specs = list(row_specs) + [pl.BlockSpec(posv.shape, fix(4)), pl.BlockSpec(w1bd.shape, fix(4)),
                                  pl.BlockSpec(w2bd.shape, fix(3))]
    out_spec = pl.BlockSpec((1, rows_per_step, 512), out_map)
    scratch = [pltpu.VMEM((4, 8, 512), F32)]
    kern = functools.partial(_compress_kernel, n_in=n_in, n_prefetch=0 if prefetch is None else 1)
    out_shape = jax.ShapeDtypeStruct((nb, n_chunks, 512), F32)
    if prefetch is None:
        return pl.pallas_call(kern, grid=grid, in_specs=in_specs, out_specs=out_spec, out_shape=out_shape,
                              scratch_shapes=scratch, compiler_params=_cparams(("arbitrary", "arbitrary")),
                              name="compress_prompt")(*row_arrays, posv, w1bd, w2bd)
    gs = pltpu.PrefetchScalarGridSpec(num_scalar_prefetch=1, grid=grid, in_specs=in_specs, out_specs=out_spec,
                                      scratch_shapes=scratch)
    return pl.pallas_call(kern, grid_spec=gs, out_shape=out_shape,
                          compiler_params=_cparams(("arbitrary", "arbitrary")),
                          name="compress_decode")(prefetch, *row_arrays, posv, w1bd, w2bd)


def _compress_prompt(cmp_rows, b, s, weights):
    n_chunks = s // CMP_STRIDE
    y = cmp_rows.reshape(b, n_chunks, CMP_STRIDE * 512)
    spec = pl.BlockSpec((1, n_chunks, CMP_STRIDE * 512), lambda bi, t: (bi, 0, 0))
    return _compress_call([y], [spec], (b, 1), lambda bi, t: (bi, 0, 0), b, n_chunks, n_chunks, weights)


def _compress_decode(cache_cmp, page_table, weights, pages_per_step):
    nb, n_pages = page_table.shape
    n_pool, page = cache_cmp.shape[:2]
    cpp = page // CMP_STRIDE
    pp = pages_per_step
    y = jnp.swapaxes(jnp.transpose(cache_cmp, (0, 2, 3, 4, 1)).reshape(n_pool, 2, 2, LANES, page), -1, -2)
    specs = [pl.BlockSpec((1, 2, 2, page, LANES), functools.partial(
        lambda bi, t, pt, i: (pt[bi, t * pp + i], 0, 0, 0, 0), i=i)) for i in range(pp)]
    return _compress_call([y] * pp, specs, (nb, n_pages // pp), lambda bi, t, pt: (bi, t, 0), nb, n_pages * cpp,
                          pp * cpp, weights, prefetch=page_table)


def _overlap_table(n_entries, n_sel, width):
    start = (np.arange(n_entries)[:, None] - 1) * CMP_STRIDE
    j = np.arange(width)[None, :]
    ov = (start < (j + 1) * L_SEL) & (start + L_CMP > j * L_SEL) & (np.arange(n_entries)[:, None] >= 1) & (j < n_sel)
    return jnp.asarray(ov.astype(np.float32), dtype=BF16)


def _masked_softmax(s, mask):
    s = jnp.where(mask, s, NEG_INF)
    m = jnp.max(s, axis=-1, keepdims=True)
    m = jnp.where(m == NEG_INF, 0.0, m)
    e = jnp.where(mask, jnp.exp(s - m), 0.0)
    return e / jnp.maximum(jnp.sum(e, axis=-1, keepdims=True), 1e-30)


def _select_blocks(imp, qpos, n_sel):
    blk = _iota(imp.shape, 1)
    cur = qpos // L_SEL
    valid = blk * L_SEL <= qpos
    forced = (blk == 0) | (blk == cur) | (blk == cur - 1)
    score = jnp.where(valid, imp + jnp.where(forced, FORCE_BONUS, 0.0), NEG_INF)
    rank = jnp.zeros(imp.shape, F32)
    for k in range(n_sel):
        sk = score[:, k:k + 1]
        ahead = (sk > score) | ((sk == score) & (blk > k))
        rank = rank + jnp.where(ahead, 1.0, 0.0)
    return jnp.where(valid & (rank < N_SEL), 1.0, 0.0)


def _online_update(state, s, mask, v, v_keys_on_lanes=False):
    m_run, l_run, acc = state
    s = jnp.where(mask, s, NEG_INF)
    m_new = jnp.maximum(m_run, jnp.max(s, axis=-1, keepdims=True))
    m_safe = jnp.where(m_new == NEG_INF, 0.0, m_new)
    p = jnp.exp(s - m_safe)
    alpha = jnp.exp(m_run - m_safe)
    pv =_dot_nt(p.astype(BF16), v) if v_keys_on_lanes else _dot(p.astype(BF16), v)
    return (m_new, alpha * l_run + jnp.sum(p, axis=-1, keepdims=True), alpha * acc + pv)


def _online_init(rows, width):
    return (jnp.full((rows, 1), NEG_INF, F32), jnp.zeros((rows, 1), F32), jnp.zeros((rows, width), F32))


def _online_finish(state):
    _, l_run, acc = state
    return acc / jnp.maximum(l_run, 1e-30)


def _nsa_prompt_kernel(q_ref, gate_ref, kcvc_ref, selkv_ref, winkv_ref, ovl_ref, expand_ref, o_ref, *, tq, n_sel):
    qi = pl.program_id(1)
    q0 = qi * tq
    nc = kcvc_ref.shape[1]
    lo_mask, hi_mask = _lane_half_masks((tq, LANES))
    qpos1 = q0 + _iota((tq, 1), 0)
    qpos4 = jnp.concatenate([qpos1] * 4, axis=0)
    col4 = _iota((4 * tq, tq), 1)
    gate = gate_ref[0]
    lane_g = _iota((tq, LANES), 1)
    tile4 = lambda a: jnp.concatenate([a] * 4, axis=0)

    for g in range(4):
        p, gh = g // 2, g % 2
        keep = hi_mask if gh else lo_mask
        ksl = slice(p * LANES, (p + 1) * LANES)
        vsl = slice(256 + p * LANES, 256 + (p + 1) * LANES)
        qs = []
        for r in range(4):
            h = g * 4 + r
            qh = q_ref[0, :, (h // 2) * LANES:(h // 2 + 1) * LANES].astype(F32)
            if h % 2 != gh:
                qh = pltpu.roll(qh, HEAD_DIM, 1)
            qs.append(jnp.where(keep, qh, 0.0).astype(BF16))
        qst = jnp.concatenate(qs, axis=0)

        kc = kcvc_ref[0, :, ksl].astype(BF16)
        vc = kcvc_ref[0, :, vsl].astype(BF16)
        ment = _iota((4 * tq, nc), 1)
        cmask = (ment >= 1) & (ment * CMP_STRIDE + (L_CMP - CMP_STRIDE - 1) <= qpos4)
        p_c = _masked_softmax(_dot_nt(qst, kc), cmask)
        o_c = _dot(p_c.astype(BF16), vc)
        psum = p_c[0:tq] + p_c[tq:2 * tq] + p_c[2 * tq:3 * tq] + p_c[3 * tq:4 * tq]
        sel = lax.cond(q0 + tq <= N_SEL * L_SEL,
                       lambda ps: jnp.where(_iota((tq, LANES), 1) * L_SEL <= qpos1, 1.0, 0.0),
                       lambda ps: _select_blocks(_dot(ps.astype(BF16), ovl_ref[...]), qpos1, n_sel),
                       psum).astype(BF16)

        def sel_step(kb, state, diag):
            rows = pl.ds(pl.multiple_of(kb * tq, tq), tq)
            chosen = tile4(_dot(sel, expand_ref[kb])) > 0.5
            if diag:
                chosen = chosen & (q0 + col4 <= qpos4)
            return _online_update(state, _dot_nt(qst, selkv_ref[0, rows, ksl]), chosen, selkv_ref[0, rows, vsl])

        st = lax.fori_loop(0, qi, lambda kb, c: sel_step(kb, c, False), _online_init(4 * tq, LANES))
        o_s = _online_finish(sel_step(qi, st, True))

        def win_step(kb, state):
            rows = pl.ds(pl.multiple_of(kb * tq, tq), tq)
            dist = qpos4 - (kb * tq + col4)
            inside = (dist >= 0) & (dist < WINDOW)
            return _online_update(state, _dot_nt(qst, winkv_ref[0, rows, ksl]), inside, winkv_ref[0, rows, vsl])

        kb_lo = jnp.maximum(qi - (WINDOW + tq - 1) // tq, 0)
        o_w = _online_finish(lax.fori_loop(kb_lo, qi + 1, win_step, _online_init(4 * tq, LANES)))

        placed = []
        for r in range(4):
            h = g * 4 + r
            rs = slice(r * tq, (r + 1) * tq)
            gsel = lambda c: jnp.sum(jnp.where(lane_g == c, gate, 0.0), axis=-1, keepdims=True)
            oh = o_c[rs] * gsel(3 * h) + o_s[rs] * gsel(3 * h + 1) + o_w[rs] * gsel(3 * h + 2)
            if h % 2 != gh:
                oh = pltpu.roll(oh, HEAD_DIM, 1)
            placed.append(oh)
        for c in range(2):
            o_ref[0, :, (g * 2 + c) * LANES:(g * 2 + c + 1) * LANES] = jnp.where(lo_mask, placed[2 * c], placed[2 * c + 1])


def _nsa_prompt(q, gate, kcvc, selkv, winkv, b, s, tq):
    n_sel = -(-s // L_SEL)
    nc = kcvc.shape[1]
    ovl = _overlap_table(nc, n_sel, LANES)
    kpos = np.arange(s).reshape(s // tq, 1, tq)
    expand = jnp.asarray((kpos // L_SEL == np.arange(LANES).reshape(1, LANES, 1)).astype(np.float32), dtype=BF16)
    r3 = lambda a: a.reshape(b, s, a.shape[-1])
    tile = lambda w: pl.BlockSpec((1, tq, w), lambda bi, qi: (bi, qi, 0))
    full = lambda n, w: pl.BlockSpec((1, n, w), lambda bi, qi: (bi, 0, 0))
    out = pl.pallas_call(
        functools.partial(_nsa_prompt_kernel, tq=tq, n_sel=n_sel),
        grid=(b, s // tq),
        in_specs=[tile(1024), tile(LANES), full(nc, 512), full(s, 512), full(s, 512),
                  pl.BlockSpec(ovl.shape, lambda bi, qi: (0, 0)), pl.BlockSpec(expand.shape, lambda bi, qi: (0, 0, 0))],
        out_specs=tile(1024),
        out_shape=jax.ShapeDtypeStruct((b, s, 1024), F32),
        compiler_params=_cparams(("arbitrary", "arbitrary")),
        name="nsa_prompt",
    )(r3(q), r3(gate), kcvc, r3(selkv), r3(winkv), ovl, expand)
    return out.reshape(b * s, 1024)


def _nsa_decode_kernel(pt_ref, q_ref, gate_ref, kcvc_ref, ovl_ref, selnew_ref, winnew_ref, winnewf_ref, state_ref,
                       *rest, pages_per_step, n_sel, past_len):
    pp = pages_per_step
    kpages = rest[:pp]
    vpages = rest[pp:2 * pp]
    o_ref, winout_ref = rest[2 * pp:2 * pp + 2]
    s_ref, qbig_ref, sel_ref, oc_ref, m_ref, l_ref, acc_ref = rest[2 * pp + 2:]
    sweep = pl.program_id(1)
    j = pl.program_id(2)
    page = kpages[0].shape[-1]
    head_of_lane = _iota((16, 256), 1) // HEAD_DIM
    row16 = _iota((16, 256), 0)
    own = head_of_lane == row16 // 4
    spread = jnp.where((_iota((16, 16), 0) // 4) * 4 == _iota((16, 16), 1), 1.0, 0.0).astype(BF16)

    @pl.when((sweep == 0) & (j == 0))
    def _():
        tile_lanes = jnp.where(_iota((HEAD_DIM, 256), 0) == _iota((HEAD_DIM, 256), 1) % HEAD_DIM, 1.0, 0.0)
        qb = _dot(q_ref[0], tile_lanes.astype(BF16))
        qbig = jnp.where(own, qb, 0.0).astype(BF16)
        qbig_ref[...] = qbig
        nc = kcvc_ref.shape[1]
        ment = _iota((16, nc), 1)
        cmask = (ment >= 1) & (ment * CMP_STRIDE + (L_CMP - CMP_STRIDE - 1) <= past_len)
        p_c = _masked_softmax(_dot_nt(qbig, kcvc_ref[0, :, 0:256].astype(BF16)), cmask)
        oc_ref[...] = _dot(p_c.astype(BF16), kcvc_ref[0, :, 256:512].astype(BF16))
        pair = p_c + pltpu.roll(p_c, 15, 0)
        psum = pair + pltpu.roll(pair, 14, 0)
        imp = _dot(psum.astype(BF16), ovl_ref[...])
        sel_ref[...] = _select_blocks(imp, jnp.full((16, 1), past_len, jnp.int32), n_sel)
        m_ref[...] = jnp.full(m_ref.shape, NEG_INF, F32)

    qbig = qbig_ref[...]
    qf = qbig.astype(F32)
    sel = sel_ref[...].astype(BF16)
    blocks_per_page = page // L_SEL
    knew = selnew_ref[0]
    s_new = jnp.sum(qf * knew[:, 0:256].astype(F32), axis=-1, keepdims=True)

    def new_token_chosen():
        last = jnp.where(_iota((256, LANES), 0) == past_len // L_SEL, 1.0, 0.0).astype(BF16)
        return _dot(spread, _dot(sel, last).astype(BF16))[:, 0:1] > 0.5

    @pl.when(sweep == 0)
    def _():
        sel_heads = _dot(spread, sel)
        blk_lane = _iota((16, 256), 1)
        blk_in_page = _iota((16, page), 1) // L_SEL
        tops = []
        for i in range(pp):
            pg = j * pp + i
            chosen = jnp.zeros((16, page), F32)
            for c in range(blocks_per_page):
                col = jnp.sum(jnp.where(blk_lane == pg * blocks_per_page + c, sel_heads, 0.0), axis=-1, keepdims=True)
                chosen = jnp.where(blk_in_page == c, col, chosen)
            s = jnp.where(chosen > 0.5, _dot(qbig, kpages[i][0, 0].astype(BF16)), NEG_INF)
            s_ref[pg] = s
            tops.append(jnp.max(s, axis=-1, keepdims=True))
        m_ref[...] = jnp.maximum(m_ref[...], _tree(jnp.maximum, tops))

    @pl.when((sweep == 1) & (j == 0))
    def _():
        has_new = new_token_chosen()
        m = jnp.maximum(m_ref[...], jnp.where(has_new, s_new, NEG_INF))
        m = jnp.where(m == NEG_INF, 0.0, m)
        m_ref[...] = m
        e_new = jnp.where(has_new, jnp.exp(s_new - m), 0.0)
        total = lax.fori_loop(0, s_ref.shape[0],
                              lambda g, t: t + jnp.sum(jnp.exp(s_ref[g] - m), axis=-1, keepdims=True), e_new)
        l_ref[...] = jnp.maximum(total, 1e-30)
        acc_ref[...] = (e_new / l_ref[...]).astype(BF16).astype(F32) * knew[:, 256:512].astype(F32)

    @pl.when(sweep == 1)
    def _():
        m, l = m_ref[...], l_ref[...]
        acc_ref[...] += _tree(jnp.add, [_dot_nt((jnp.exp(s_ref[j * pp + i] - m) / l).astype(BF16),
                                                vpages[i][0, 0].astype(BF16)) for i in range(pp)])

    @pl.when((sweep == 1) & (j == pl.num_programs(2) - 1))
    def _():
        o_s = acc_ref[...]

        w_buf = state_ref.shape[-1]
        s_w = _dot(qbig, state_ref[0, 0].astype(BF16))
        wpos = past_len - w_buf + _iota(s_w.shape, 1)
        wmask = (past_len - wpos < WINDOW) & (wpos >= 0)
        wnew = winnew_ref[0]
        s_wn = jnp.sum(qf * wnew[:, 0:256].astype(F32), axis=-1, keepdims=True)
        mw = jnp.maximum(jnp.max(jnp.where(wmask, s_w, NEG_INF), axis=-1, keepdims=True), s_wn)
        e = jnp.where(wmask, jnp.exp(s_w - mw), 0.0)
        en = jnp.exp(s_wn - mw)
        denom = jnp.maximum(jnp.sum(e, axis=-1, keepdims=True) + en, 1e-30)
        o_w = (_dot_nt((e / denom).astype(BF16), state_ref[0, 1].astype(BF16))
               + (en / denom).astype(BF16).astype(F32) * wnew[:, 256:512].astype(F32))
        newest = _iota((256, w_buf), 1) == w_buf - 1
        for i in range(2):
            winout_ref[0, i] = jnp.where(newest, winnewf_ref[0, i], pltpu.roll(state_ref[0, i], w_buf - 1, 1))

        gate = jnp.broadcast_to(gate_ref[0], (16, LANES))
        lane = _iota((16, LANES), 1)
        hrow = _iota((16, LANES), 0)
        gsel = lambda br: jnp.sum(jnp.where(lane == 3 * hrow + br, gate, 0.0), axis=-1, keepdims=True)
        o = oc_ref[...] * gsel(0) + o_s * gsel(1) + o_w * gsel(2)
        fold = jnp.where(_iota((256, HEAD_DIM), 0) % HEAD_DIM == _iota((256, HEAD_DIM), 1), 1.0, 0.0).astype(BF16)
        o_ref[0] = _split3_dot(jnp.where(own, o, 0.0), fold)


def _nsa_decode(page_table, q, gate, kcvc, selkv_new, winkv_new, win_new, state, cache_sel, pages_per_step):
    nb, n_pages = page_table.shape
    n_pool, page = cache_sel.shape[:2]
    past_len = n_pages * page
    n_sel = -(-(past_len + 1) // L_SEL)
    assert n_sel <= 256
    pp = pages_per_step
    nc = kcvc.shape[1]
    w_buf = state.shape[1]
    ovl = _overlap_table(nc, n_sel, 256)
    keys_last = lambda a: jnp.transpose(a, (0, 2, 3, 4, 1)).reshape(a.shape[0], 2, 256, a.shape[1])
    cs = keys_last(cache_sel)
    n_steps = n_pages // pp
    per_b = lambda b, sw, j, pt: (b, 0, 0)
    per_b4 = lambda b, sw, j, pt: (b, 0, 0, 0)
    kspecs = [pl.BlockSpec((1, 1, 256, page), functools.partial(
        lambda b, sw, j, pt, i: (pt[b, jnp.where(sw == 0, j, n_steps - 1) * pp + i], 0, 0, 0), i=i)) for i in range(pp)]
    vspecs = [pl.BlockSpec((1, 1, 256, page), functools.partial(
        lambda b, sw, j, pt, i: (pt[b, jnp.where(sw == 1, j, 0) * pp + i], 1, 0, 0), i=i)) for i in range(pp)]
    grid_spec = pltpu.PrefetchScalarGridSpec(
        num_scalar_prefetch=1,
        grid=(nb, 2, n_steps),
        in_specs=[pl.BlockSpec((1, 16, HEAD_DIM), per_b), pl.BlockSpec((1, 1, LANES), per_b),
                  pl.BlockSpec((1, nc, 512), per_b), pl.BlockSpec(ovl.shape, lambda b, sw, j, pt: (0, 0)),
                  pl.BlockSpec((1, 1, 512), per_b), pl.BlockSpec((1, 1, 512), per_b), pl.BlockSpec((1, 2, 256, 1), per_b4),
                  pl.BlockSpec((1, 2, 256, w_buf), per_b4)] + kspecs + vspecs,
        out_specs=[pl.BlockSpec((1, 16, HEAD_DIM), per_b), pl.BlockSpec((1, 2, 256, w_buf), per_b4)],
        scratch_shapes=[pltpu.VMEM((n_pages, 16, page), F32),
                        pltpu.VMEM((16, 256), BF16), pltpu.VMEM((16, 256), F32), pltpu.VMEM((16, 256), F32),
                        pltpu.VMEM((16, 1), F32), pltpu.VMEM((16, 1), F32), pltpu.VMEM((16, 256), F32)],
    )
    o, win_out = pl.pallas_call(
        functools.partial(_nsa_decode_kernel, pages_per_step=pp, n_sel=n_sel, past_len=past_len),
        grid_spec=grid_spec,
        out_shape=[jax.ShapeDtypeStruct((nb, 16, HEAD_DIM), F32), jax.ShapeDtypeStruct((nb, 2, 256, w_buf), F32)],
        compiler_params=_cparams(("arbitrary", "arbitrary", "arbitrary")),
        name="nsa_decode",
    )(page_table, q.reshape(nb, 16, HEAD_DIM), gate.reshape(nb, 1, LANES), kcvc, ovl, selkv_new.reshape(nb, 1, 512),
      winkv_new.reshape(nb, 1, 512), win_new.reshape(nb, 2, 256, 1), keys_last(state), *([cs] * (2 * pp)))
    win_rows = jnp.transpose(win_out.reshape(nb, 2, 4, HEAD_DIM, w_buf), (0, 4, 1, 2, 3))
    return o.reshape(nb, 1024), win_rows


def _outproj_router_kernel(x_ref, o_ref, wo_ref, ng_ref, wr_ref, x1_ref, h_ref, comb_ref):
    x1 = x_ref[...] + _dot(o_ref[...].astype(BF16), wo_ref[...])
    x1_ref[...] = x1
    hf = _rms(x1, ng_ref[...])
    h = hf.astype(BF16)
    h_ref[...] = hf.astype(h_ref.dtype)
    logits = _dot(h, wr_ref[...])
    lane = _iota(logits.shape, 1)
    logits = jnp.where(lane < N_EXPERTS, logits, NEG_INF)
    m1 = jnp.max(logits, axis=-1, keepdims=True)
    i1 = jnp.min(jnp.where(logits == m1, lane, LANES), axis=-1, keepdims=True)
    rest = jnp.where(lane == i1, NEG_INF, logits)
    m2 = jnp.max(rest, axis=-1, keepdims=True)
    i2 = jnp.min(jnp.where(rest == m2, lane, LANES), axis=-1, keepdims=True)
    e2 = jnp.exp(m2 - m1)
    comb_ref[...] = jnp.where(lane == i1, 1.0 / (1.0 + e2), 0.0) + jnp.where(lane == i2, e2 / (1.0 + e2), 0.0)


def _outproj_router(x, o, w_out, norm_g, w_router, tm, h_dtype=BF16):
    t, d = x.shape
    row = lambda i: (i, 0)
    fixed = lambda i: (0, 0)
    wr = jnp.pad(w_router, ((0, 0), (0, LANES - w_router.shape[1]))).astype(BF16)
    outs = [jax.ShapeDtypeStruct((t, d), F32), jax.ShapeDtypeStruct((t, d), h_dtype), jax.ShapeDtypeStruct((t, LANES), F32)]
    return pl.pallas_call(
        _outproj_router_kernel,
        grid=(t // tm,),
        in_specs=[pl.BlockSpec((tm, d), row), pl.BlockSpec((tm, o.shape[1]), row), pl.BlockSpec(w_out.shape, fixed),
                  pl.BlockSpec((1, d), fixed), pl.BlockSpec(wr.shape, fixed)],
        out_specs=[pl.BlockSpec((tm, a.shape[1]), row) for a in outs],
        out_shape=outs,
        compiler_params=_cparams(("arbitrary",)),
        name="outproj_router",
    )(x, o, w_out.astype(BF16), norm_g.reshape(1, d), wr)


def _moe_kernel(x1_ref, h_ref, comb_ref, wg_ref, wu_ref, wd_ref, y_ref, acc_ref):
    e = pl.program_id(1)
    j = pl.program_id(2)

    @pl.when((e == 0) & (j == 0))
    def _():
        acc_ref[...] = jnp.zeros_like(acc_ref)

    h = h_ref[...]
    g = _dot(h, wg_ref[0])
    u = _dot(h, wu_ref[0])
    a = (g * jax.nn.sigmoid(g) * u).astype(BF16)
    comb = comb_ref[...]
    w_e = jnp.sum(jnp.where(_iota(comb.shape, 1) == e, comb, 0.0), axis=-1, keepdims=True)
    acc_ref[...] += w_e * _dot(a, wd_ref[0])

    @pl.when((e == pl.num_programs(1) - 1) & (j == pl.num_programs(2) - 1))
    def _():
        y_ref[...] = x1_ref[...] + acc_ref[...]


def _moe(x1, h, comb, w_gate_up, w_down, tm, tf):
    t, d = x1.shape
    n_e, ff = w_down.shape[:2]
    nj = ff // tf
    row = lambda i, e, j: (i, 0)
    wgu = w_gate_up.astype(BF16)
    return pl.pallas_call(
        _moe_kernel,
        grid=(t // tm, n_e, nj),
        in_specs=[pl.BlockSpec((tm, d), row), pl.BlockSpec((tm, d), row), pl.BlockSpec((tm, LANES), row),
                  pl.BlockSpec((1, d, tf), lambda i, e, j: (e, 0, j)),
                  pl.BlockSpec((1, d, tf), lambda i, e, j: (e, 0, nj + j)),
                  pl.BlockSpec((1, tf, d), lambda i, e, j: (e, j, 0))],
        out_specs=pl.BlockSpec((tm, d), row),
        out_shape=jax.ShapeDtypeStruct((t, d), F32),
        scratch_shapes=[pltpu.VMEM((tm, d), F32)],
        compiler_params=_cparams(("arbitrary", "arbitrary", "arbitrary")),
        name="moe",
    )(x1, h, comb, wgu, wgu, w_down.astype(BF16))


def _moe_plan(comb, tm):
    t = comb.shape[0]
    routed = comb[:, :N_EXPERTS] > 0.0
    r32 = routed.astype(jnp.int32)
    counts = jnp.sum(r32, axis=0)
    padded = ((counts + tm - 1) // tm) * tm
    ends = jnp.cumsum(padded)
    starts = ends - padded
    dest = starts[None, :] + jnp.cumsum(r32, axis=0) - r32
    n_tiles = (2 * t) // tm + N_EXPERTS
    tile_start = jnp.arange(n_tiles, dtype=jnp.int32) * tm
    tile_expert = jnp.minimum(jnp.sum((tile_start[:, None] >= ends[None, :]).astype(jnp.int32), axis=1),
                              N_EXPERTS - 1)
    tile_used = (tile_start < ends[-1]).astype(jnp.int32)
    first = jnp.argmax(routed, axis=1)
    last = N_EXPERTS - 1 - jnp.argmax(routed[:, ::-1], axis=1)
    row_a = jnp.take_along_axis(dest, first[:, None], axis=1)[:, 0]
    row_b = jnp.take_along_axis(dest, last[:, None], axis=1)[:, 0]
    return n_tiles, tile_expert, tile_used, row_a.astype(jnp.int32), row_b.astype(jnp.int32)


def _moe_scatter_kernel(ra_ref, rb_ref, h_ref, grouped_in, grouped_hbm, sem):
    tc = h_ref.shape[0]

    def start(r, c):
        pltpu.make_async_copy(h_ref.at[pl.ds(r, 1), :], grouped_hbm.at[pl.ds(ra_ref[0, 0, r], 1), :], sem).start()
        pltpu.make_async_copy(h_ref.at[pl.ds(r, 1), :], grouped_hbm.at[pl.ds(rb_ref[0, 0, r], 1), :], sem).start()
        return c

    lax.fori_loop(0, tc, start, 0, unroll=8)
    for _ in range(2):
        pltpu.make_async_copy(h_ref, grouped_hbm.at[pl.ds(0, tc), :], sem).wait()


def _moe_scatter(h, row_a, row_b, n_rows, tc):
    t, d = h.shape
    smem_row = pl.BlockSpec((1, 1, tc), lambda i: (i, 0, 0), memory_space=pltpu.SMEM)
    return pl.pallas_call(
        _moe_scatter_kernel,
        grid=(t // tc,),
        in_specs=[smem_row, smem_row, pl.BlockSpec((tc, d), lambda i: (i, 0)), pl.BlockSpec(memory_space=pl.ANY)],
        out_specs=pl.BlockSpec(memory_space=pl.ANY),
        out_shape=jax.ShapeDtypeStruct((n_rows, d), h.dtype),
        input_output_aliases={3: 0},
        scratch_shapes=[pltpu.SemaphoreType.DMA(())],
        compiler_params=_cparams(("arbitrary",)),
        name="moe_scatter",
    )(row_a.reshape(t // tc, 1, tc), row_b.reshape(t // tc, 1, tc), h, jnp.zeros((n_rows, d), h.dtype))


def _moe_rows_kernel(te_ref, tu_ref, x_ref, wg_ref, wu_ref, wd_ref, y_ref, acc_ref):
    i = pl.program_id(0)
    j = pl.program_id(1)
    used = tu_ref[i] > 0

    @pl.when(used & (j == 0))
    def _():
        acc_ref[...] = jnp.zeros_like(acc_ref)

    @pl.when(used)
    def _():
        h = x_ref[...].astype(BF16)
        g = _dot(h, wg_ref[0])
        u = _dot(h, wu_ref[0])
        acc_ref[...] += _dot((g * jax.nn.sigmoid(g) * u).astype(BF16), wd_ref[0])

    last = j == pl.num_programs(1) - 1

    @pl.when(used & last)
    def _():
        y_ref[...] = acc_ref[...]

    @pl.when(jnp.logical_not(used) & last)
    def _():
        y_ref[...] = jnp.zeros_like(y_ref)


def _moe_rows(grouped, tile_expert, tile_used, w_gate_up, w_down, tm, tf):
    d = grouped.shape[1]
    n_tiles = grouped.shape[0] // tm
    n_e, ff = w_down.shape[:2]
    nj = ff // tf
    wgu = w_gate_up.astype(BF16)
    grid_spec = pltpu.PrefetchScalarGridSpec(
        num_scalar_prefetch=2,
        grid=(n_tiles, nj),
        in_specs=[pl.BlockSpec((tm, d), lambda i, j, te, tu: (i, 0)),
                  pl.BlockSpec((1, d, tf), lambda i, j, te, tu: (te[i], 0, j)),
                  pl.BlockSpec((1, d, tf), lambda i, j, te, tu: (te[i], 0, nj + j)),
                  pl.BlockSpec((1, tf, d), lambda i, j, te, tu: (te[i], j, 0))],
        out_specs=pl.BlockSpec((tm, d), lambda i, j, te, tu: (i, 0)),
        scratch_shapes=[pltpu.VMEM((tm, d), F32)],
    )
    return pl.pallas_call(
        _moe_rows_kernel,
        grid_spec=grid_spec,
        out_shape=jax.ShapeDtypeStruct((n_tiles * tm, d), F32),
        compiler_params=_cparams(("arbitrary", "arbitrary")),
        name="moe_rows",
    )(tile_expert, tile_used, grouped, wgu, wgu, w_down.astype(BF16))


def _moe_combine_kernel(ra_ref, rb_ref, x1_ref, comb_ref, ys_hbm, out_ref, buf_ref, sem):
    tc = x1_ref.shape[0]

    def row_copy(slot, r, src):
        return pltpu.make_async_copy(ys_hbm.at[pl.ds(src, 1), :], buf_ref.at[slot, pl.ds(r, 1), :], sem)

    def start(r, c):
        row_copy(0, r, ra_ref[0, 0, r]).start()
        row_copy(1, r, rb_ref[0, 0, r]).start()
        return c

    lax.fori_loop(0, tc, start, 0, unroll=8)
    for slot in range(2):
        pltpu.make_async_copy(ys_hbm.at[pl.ds(0, tc), :], buf_ref.at[slot], sem).wait()
    comb = comb_ref[...]
    lane = _iota(comb.shape, 1)
    on = comb > 0.0
    ia = jnp.min(jnp.where(on, lane, LANES), axis=-1, keepdims=True)
    ib = jnp.max(jnp.where(on, lane, -1), axis=-1, keepdims=True)
    w_a = jnp.sum(jnp.where(lane == ia, comb, 0.0), axis=-1, keepdims=True)
    w_b = jnp.sum(jnp.where((lane == ib) & (ib != ia), comb, 0.0), axis=-1, keepdims=True)
    out_ref[...] = x1_ref[...] + (w_a * buf_ref[0] + w_b * buf_ref[1])


def _moe_combine(x1, comb, ys, row_a, row_b, tc):
    t, d = x1.shape
    row = lambda i: (i, 0)
    smem_row = pl.BlockSpec((1, 1, tc), lambda i: (i, 0, 0), memory_space=pltpu.SMEM)
    return pl.pallas_call(
        _moe_combine_kernel,
        grid=(t // tc,),
        in_specs=[smem_row, smem_row, pl.BlockSpec((tc, d), row), pl.BlockSpec((tc, LANES), row),
                  pl.BlockSpec(memory_space=pl.ANY)],
        out_specs=pl.BlockSpec((tc, d), row),
        out_shape=jax.ShapeDtypeStruct((t, d), F32),
        scratch_shapes=[pltpu.VMEM((2, tc, d), F32), pltpu.SemaphoreType.DMA(())],
        compiler_params=_cparams(("arbitrary",)),
        name="moe_combine",
    )(row_a.reshape(t // tc, 1, tc), row_b.reshape(t // tc, 1, tc), x1, comb, ys)


def _moe_grouped(x1, h, comb, w_gate_up, w_down, tm, tf, tc):
    n_tiles, tile_expert, tile_used, row_a, row_b = _moe_plan(comb, tm)
    grouped = _moe_scatter(h, row_a, row_b, n_tiles * tm, tc)
    ys = _moe_rows(grouped, tile_expert, tile_used, w_gate_up, w_down, tm, tf)
    return _moe_combine(x1, comb, ys, row_a, row_b, tc)


def kernel(x_prompt, x_sample, cache_l0_diff, cache_l0_sb, cache_l1_cmp, cache_l1_sel, state_l1_win, page_table, l0_norm_mix, l0_w_in, l0_diff_q_norm, l0_diff_k_norm, l0_diff_lambda, l0_diff_subln, l0_w_out, l0_norm_ffn, l0_ffn_w_gate_up, l0_ffn_w_down, l1_norm_mix, l1_w_in, l1_q_norm, l1_k_norm, l1_cmp_pos, l1_cmp_w1, l1_cmp_w2, l1_w_out, l1_norm_ffn, l1_router, l1_moe_w_gate_up, l1_moe_w_down):
    b, s, d = x_prompt.shape
    nb = x_sample.shape[0]
    n_pages, page = page_table.shape[1], cache_l0_diff.shape[1]
    past_len = n_pages * page
    tm = min(512, s)
    tq = min(256, s)
    dec_pages = min(16, n_pages)
    cmp_pages = min(32, n_pages)
    ff_tile = l0_ffn_w_down.shape[0] // 2
    moe_tile = l1_moe_w_down.shape[1] // 7
    tabs_p = _rope_tables(jnp.arange(s, dtype=jnp.int32))
    tabs_s = _rope_tables(jnp.full((nb,), past_len, jnp.int32))
    xp = x_prompt.reshape(b * s, d)
    xs = x_sample.reshape(nb, d)

    dq, diff_p, sq, sb_p, dkv, skv = _proj0(xp, l0_norm_mix, l0_w_in, l0_diff_q_norm, l0_diff_k_norm, tabs_p, tm)
    o = _attn0_prompt(l0_diff_lambda, l0_diff_subln, dq, sq, dkv, skv, b, s, tq)
    xp = _outproj_ffn(xp, o, l0_w_out, l0_norm_ffn, l0_ffn_w_gate_up, l0_ffn_w_down, tm, ff_tile)
    dq, diff_s, sq, sb_s, dkv, skv = _proj0(xs, l0_norm_mix, l0_w_in, l0_diff_q_norm, l0_diff_k_norm, tabs_s, nb)
    o = _attn0_decode(page_table, l0_diff_lambda, l0_diff_subln, dq, sq, dkv, cache_l0_diff, cache_l0_sb, dec_pages)
    xs = _outproj_ffn(xs, o, l0_w_out, l0_norm_ffn, l0_ffn_w_gate_up, l0_ffn_w_down, nb, ff_tile)

    cw = _compress_weights(l1_cmp_pos, l1_cmp_w1, l1_cmp_w2)
    q, cmp_p, sel_p, win_p, gate, selkv, winkv = _proj1(xp, l1_norm_mix, l1_w_in, l1_q_norm, l1_k_norm, tabs_p, tm)
    kcvc = _compress_prompt(cmp_p, b, s, cw)
    o = _nsa_prompt(q, gate, kcvc, selkv, winkv, b, s, tq)
    x1, h, comb = _outproj_router(xp, o, l1_w_out, l1_norm_ffn, l1_router, tm, h_dtype=F32)
    xp = _moe_grouped(x1, h, comb, l1_moe_w_gate_up, l1_moe_w_down, tm, l1_moe_w_down.shape[1] // 2, min(256, s))

    q, cmp_s, sel_s, win_s_new, gate, selkv, winkv = _proj1(xs, l1_norm_mix, l1_w_in, l1_q_norm, l1_k_norm, tabs_s, nb)
    kcvc = _compress_decode(cache_l1_cmp, page_table, cw, cmp_pages)
    o, win_s = _nsa_decode(page_table, q, gate, kcvc, selkv, winkv, win_s_new, state_l1_win, cache_l1_sel, dec_pages)
    x1, h, comb = _outproj_router(xs, o, l1_w_out, l1_norm_ffn, l1_router, nb)
    xs = _moe(x1, h, comb, l1_moe_w_gate_up, l1_moe_w_down, nb, moe_tile)

    w_keep = min(WINDOW, s)
    hk = cmp_p.shape[-1] // (2 * HEAD_DIM)
    rows5 = lambda a, n: a.reshape(n, -1, 2, a.shape[-1] // (2 * HEAD_DIM), HEAD_DIM)
    return (xp.reshape(b, s, d), xs.reshape(nb, 1, d),
            diff_p.reshape(b, s, 2, 4, 2 * HEAD_DIM), sb_p.reshape(b, s, 2, 8, HEAD_DIM),
            rows5(cmp_p, b), rows5(sel_p, b), rows5(win_p, b)[:, s - w_keep:],
            diff_s.reshape(nb, 1, 2, 4, 2 * HEAD_DIM), sb_s.reshape(nb, 1, 2, 8, HEAD_DIM),
            rows5(cmp_s, nb), rows5(sel_s, nb), win_s)
```

```python
import functools

import numpy as np
import jax
import jax.numpy as jnp
from jax import lax
from jax.experimental import pallas as pl
from jax.experimental.pallas import tpu as pltpu

F32 = jnp.float32
BF16 = jnp.bfloat16

HEAD_DIM = 64
ROT_DIM = HEAD_DIM // 4
ROPE_THETA = 500000.0
EPS = 1e-6
DIFF_LAMBDA_INIT = 0.2
L_CMP = 32
CMP_STRIDE = 16
L_SEL = 64
N_SEL = 16
WINDOW = 512
FORCE_BONUS = 1.0e4
N_EXPERTS = 8
LANES = 128
VMEM_LIMIT = 56 * 1024 * 1024
NEG_INF = float("-inf")
SB_LOG_WEIGHT_FLOOR = -110.0


def _cparams(sem):
    return pltpu.CompilerParams(dimension_semantics=sem, vmem_limit_bytes=VMEM_LIMIT)


def _dot(a, b):
    return jnp.dot(a, b, preferred_element_type=F32)


def _dot_nt(a, b):
    return lax.dot_general(a, b, (((1,), (1,)), ((), ())), preferred_element_type=F32)


def _split2_dot(x, m):
    hi = x.astype(BF16)
    lo = (x - hi.astype(F32)).astype(BF16)
    return _dot(hi, m) + _dot(lo, m)


def _split3_dot(x, m):
    hi = x.astype(BF16)
    r1 = x - hi.astype(F32)
    mid = r1.astype(BF16)
    lo = (r1 - mid.astype(F32)).astype(BF16)
    return _dot(hi, m) + _dot(mid, m) + _dot(lo, m)


def _tree(op, xs):
    xs = list(xs)
    while len(xs) > 1:
        xs = [op(a, b) for a, b in zip(xs[0::2], xs[1::2])] + ([xs[-1]] if len(xs) % 2 else [])
    return xs[0]


def _iota(shape, dim):
    return lax.broadcasted_iota(jnp.int32, shape, dim)


def _group_ones(n, group):
    r = _iota((n, n), 0) // group
    c = _iota((n, n), 1) // group
    return jnp.where(r == c, 1.0, 0.0).astype(BF16)


def _rms(x, g):
    return x * lax.rsqrt(jnp.mean(x * x, axis=-1, keepdims=True) + EPS) * g


def _head_norm_rope(xc, g, ones64, c, sa, sb):
    ms = _split2_dot(xc * xc, ones64) * (1.0 / HEAD_DIM)
    xn = xc * lax.rsqrt(ms + EPS) * g
    return xn * c + pltpu.roll(xn, LANES - ROT_DIM // 2, 1) * sa + pltpu.roll(xn, ROT_DIM // 2, 1) * sb


def _rope_tables(pos):
    half = ROT_DIM // 2
    inv = ROPE_THETA ** (-jnp.arange(half, dtype=F32) * 2.0 / ROT_DIM)
    ang = pos.astype(F32)[:, None] * inv[None, :]
    cos, sin = jnp.cos(ang), jnp.sin(ang)
    n = pos.shape[0]
    one = jnp.ones((n, HEAD_DIM - ROT_DIM), F32)
    zero_h = jnp.zeros((n, half), F32)
    zero_r = jnp.zeros((n, HEAD_DIM - ROT_DIM), F32)
    c = jnp.concatenate([cos, cos, one], axis=1)
    sa = jnp.concatenate([-sin, zero_h, zero_r], axis=1)
    sb = jnp.concatenate([zero_h, sin, zero_r], axis=1)
    tile = lambda t: jnp.concatenate([t, t], axis=1)
    return tile(c), tile(sa), tile(sb)


def _proj0_kernel(x_ref, ng_ref, w_ref, qg_ref, kg_ref, c_ref, sa_ref, sb_ref,
                  dq_ref, drows_ref, sq_ref, srows_ref, dkv_ref, skv_ref):
    h = _rms(x_ref[...], ng_ref[...]).astype(BF16)
    ones64 = _group_ones(LANES, HEAD_DIM)
    c, sa, sb = c_ref[...], sa_ref[...], sb_ref[...]
    scale = HEAD_DIM ** -0.5
    y = _dot(h, w_ref[:, 0:512])
    for j in range(4):
        sl = slice(j * LANES, (j + 1) * LANES)
        dq_ref[:, sl] = (_head_norm_rope(y[:, sl], qg_ref[...], ones64, c, sa, sb) * scale).astype(BF16)
    y = _dot(h, w_ref[:, 512:1024])
    for j in range(4):
        sl = slice(j * LANES, (j + 1) * LANES)
        k = _head_norm_rope(y[:, sl], kg_ref[...], ones64, c, sa, sb)
        drows_ref[:, sl] = k
        dkv_ref[:, sl] = k.astype(BF16)
    y = _dot(h, w_ref[:, 1024:1536])
    drows_ref[:, 512:1024] = y
    dkv_ref[:, 512:1024] = y.astype(BF16)
    y = _dot(h, w_ref[:, 1536:2048])
    sq_ref[...] = (y * scale).astype(BF16)
    y = _dot(h, w_ref[:, 2048:3072])
    srows_ref[...] = y
    skv_ref[...] = y.astype(BF16)


def _proj0(x, norm_g, w_in, q_g, k_g, tabs, tm):
    t, d = x.shape
    tab_blocks = tabs[0].shape[0] // tm
    row = lambda i: (i, 0)
    fixed = lambda i: (0, 0)
    tab = lambda i: (i % tab_blocks, 0)
    g2 = lambda g: jnp.tile(g.reshape(1, HEAD_DIM), (1, LANES // HEAD_DIM))
    outs = [jax.ShapeDtypeStruct((t, 512), BF16), jax.ShapeDtypeStruct((t, 1024), F32),
            jax.ShapeDtypeStruct((t, 512), BF16), jax.ShapeDtypeStruct((t, 1024), F32),
            jax.ShapeDtypeStruct((t, 1024), BF16), jax.ShapeDtypeStruct((t, 1024), BF16)]
    return pl.pallas_call(
        _proj0_kernel,
        grid=(t // tm,),
        in_specs=[pl.BlockSpec((tm, d), row), pl.BlockSpec((1, d), fixed),
                  pl.BlockSpec(w_in.shape, fixed),
                  pl.BlockSpec((1, LANES), fixed), pl.BlockSpec((1, LANES), fixed),
                  pl.BlockSpec((tm, LANES), tab), pl.BlockSpec((tm, LANES), tab), pl.BlockSpec((tm, LANES), tab)],
        out_specs=[pl.BlockSpec((tm, o.shape[1]), row) for o in outs],
        out_shape=outs,
        compiler_params=_cparams(("arbitrary",)),
        name="proj0",
    )(x, norm_g.reshape(1, d), w_in.astype(BF16), g2(q_g), g2(k_g), *tabs)


def _lane_half_masks(shape):
    lane = _iota(shape, 1)
    return lane < HEAD_DIM, lane >= HEAD_DIM


def _diff_lambda(lp):
    a = jnp.sum(lp[0:1] * lp[1:2], axis=-1, keepdims=True)
    b = jnp.sum(lp[2:3] * lp[3:4], axis=-1, keepdims=True)
    return jnp.exp(a) - jnp.exp(b) + DIFF_LAMBDA_INIT


def _attn0_kernel(lam_ref, subg_ref, dq_ref, sq_ref, dkv_ref, skv_ref, o_ref, *, tq):
    qi = pl.program_id(1)
    q0 = qi * tq
    lam = _diff_lambda(lam_ref[...])
    lo_mask, hi_mask = _lane_half_masks((tq, LANES))
    row = _iota((tq, tq), 0)
    col = _iota((tq, tq), 1)
    zero_bf = jnp.zeros((tq, LANES), BF16)

    def split_maps(ref):
        out = []
        for c in range(4):
            q = ref[0, :, c * LANES:(c + 1) * LANES]
            out += [jnp.where(lo_mask, q, zero_bf), jnp.where(hi_mask, q, zero_bf)]
        return out

    qd = split_maps(dq_ref)

    def step(kb, carry, diag):
        rows = pl.ds(pl.multiple_of(kb * tq, tq), tq)
        out = []
        for h in range(4):
            k = dkv_ref[0, rows, h * LANES:(h + 1) * LANES]
            v = dkv_ref[0, rows, 512 + h * LANES:512 + (h + 1) * LANES]
            for m in range(2):
                m_run, l_run, acc = carry[2 * h + m]
                s = _dot_nt(qd[2 * h + m], k)
                if diag:
                    s = jnp.where(col <= row, s, NEG_INF)
                m_new = jnp.maximum(m_run, jnp.max(s, axis=-1, keepdims=True))
                p = jnp.exp(s - m_new)
                alpha = jnp.exp(m_run - m_new)
                l_new = alpha * l_run + jnp.sum(p, axis=-1, keepdims=True)
                out.append((m_new, l_new, alpha * acc + _dot(p.astype(BF16), v)))
        return tuple(out)

    init = tuple((jnp.full((tq, 1), NEG_INF, F32), jnp.zeros((tq, 1), F32), jnp.zeros((tq, LANES), F32))
                 for _ in range(8))
    carry = lax.fori_loop(0, qi, lambda kb, c: step(kb, c, False), step(qi, init, True))
    for h in range(4):
        (_, l0, a0), (_, l1, a1) = carry[2 * h], carry[2 * h + 1]
        o = a0 / l0 - lam * (a1 / l1)
        sl = slice(h * LANES, (h + 1) * LANES)
        o_ref[0, :, sl] = _rms(o, subg_ref[:, sl]) * (1.0 - DIFF_LAMBDA_INIT)

    upper = jnp.where(row > col, 1.0, 0.0).astype(BF16)
    qs = split_maps(sq_ref)

    def sb_step(kb, carry, diag):
        rows = pl.ds(pl.multiple_of(kb * tq, tq), tq)
        out = []
        for c in range(4):
            k = skv_ref[0, rows, c * LANES:(c + 1) * LANES]
            v = skv_ref[0, rows, 512 + c * LANES:512 + (c + 1) * LANES]
            for m in range(2):
                tail, acc = carry[2 * c + m]
                z = _dot_nt(qs[2 * c + m], k)
                t = jnp.log1p(jnp.exp(-jnp.abs(z)))
                lsp = -(jnp.maximum(-z, 0.0) + t)
                l1m = -(jnp.maximum(z, 0.0) + t)
                if diag:
                    l1m = jnp.where(col < row, l1m, 0.0)
                w = jnp.exp(lsp + _split2_dot(l1m, upper) + tail)
                if diag:
                    w = jnp.where(col < row, w, 0.0)
                out.append((tail + jnp.sum(l1m, axis=-1, keepdims=True), acc + _dot(w.astype(BF16), v)))
        return tuple(out)

    def still_visible(carry):
        return jnp.max(_tree(jnp.maximum, [t for t, _ in carry])) > SB_LOG_WEIGHT_FLOOR

    def sb_body(c):
        kb, _, carry = c
        carry = sb_step(kb, carry, False)
        return kb - 1, still_visible(carry), carry

    init = tuple((jnp.zeros((tq, 1), F32), jnp.zeros((tq, LANES), F32)) for _ in range(8))
    first = sb_step(qi, init, True)
    _, _, carry = lax.while_loop(lambda c: (c[0] >= 0) & c[1], sb_body, (qi - 1, still_visible(first), first))
    for c in range(4):
        o_ref[0, :, 512 + c * LANES:512 + (c + 1) * LANES] = jnp.where(lo_mask, carry[2 * c][1], carry[2 * c + 1][1])


def _attn0_prompt(lam_p, subln_g, dq, sq, dkv, skv, b, s, tq):
    r3 = lambda a: a.reshape(b, s, a.shape[-1])
    qspec = pl.BlockSpec((1, tq, 512), lambda bi, qi: (bi, qi, 0))
    kvspec = pl.BlockSpec((1, s, 1024), lambda bi, qi: (bi, 0, 0))
    fixed = lambda bi, qi: (0, 0)
    out = pl.pallas_call(
        functools.partial(_attn0_kernel, tq=tq),
        grid=(b, s // tq),
        in_specs=[pl.BlockSpec((4, HEAD_DIM), fixed), pl.BlockSpec((1, 512), fixed), qspec, qspec, kvspec, kvspec],
        out_specs=pl.BlockSpec((1, tq, 1024), lambda bi, qi: (bi, qi, 0)),
        out_shape=jax.ShapeDtypeStruct((b, s, 1024), F32),
        compiler_params=_cparams(("arbitrary", "arbitrary")),
        name="attn0_prompt",
    )(lam_p, subln_g.reshape(1, 512), r3(dq), r3(sq), r3(dkv), r3(skv))
    return out.reshape(b * s, 1024)


def _outproj_ffn_kernel(x_ref, o_ref, wo_ref, ng_ref, wg_ref, wu_ref, wd_ref, y_ref, x1_ref, h_ref, acc_ref):
    j = pl.program_id(1)

    @pl.when(j == 0)
    def _():
        x1 = x_ref[...] + _dot(o_ref[...].astype(BF16), wo_ref[...])
        x1_ref[...] = x1
        h_ref[...] = _rms(x1, ng_ref[...]).astype(BF16)
        acc_ref[...] = jnp.zeros_like(acc_ref)

    h = h_ref[...]
    g = _dot(h, wg_ref[...])
    u = _dot(h, wu_ref[...])
    a = (g * jax.nn.sigmoid(g) * u).astype(BF16)
    acc_ref[...] += _dot(a, wd_ref[...])

    @pl.when(j == pl.num_programs(1) - 1)
    def _():
        y_ref[...] = x1_ref[...] + acc_ref[...]


def _outproj_ffn(x, o, w_out, norm_g, w_gate_up, w_down, tm, tf):
    t, d = x.shape
    ff = w_down.shape[0]
    nj = ff // tf
    row = lambda i, j: (i, 0)
    fixed = lambda i, j: (0, 0)
    wgu = w_gate_up.astype(BF16)
    return pl.pallas_call(
        _outproj_ffn_kernel,
        grid=(t // tm, nj),
        in_specs=[pl.BlockSpec((tm, d), row), pl.BlockSpec((tm, o.shape[1]), row),
                  pl.BlockSpec(w_out.shape, fixed), pl.BlockSpec((1, d), fixed),
                  pl.BlockSpec((d, tf), lambda i, j: (0, j)), pl.BlockSpec((d, tf), lambda i, j: (0, nj + j)),
                  pl.BlockSpec((tf, d), lambda i, j: (j, 0))],
        out_specs=pl.BlockSpec((tm, d), row),
        out_shape=jax.ShapeDtypeStruct((t, d), F32),
        scratch_shapes=[pltpu.VMEM((tm, d), F32), pltpu.VMEM((tm, d), BF16), pltpu.VMEM((tm, d), F32)],
        compiler_params=_cparams(("arbitrary", "arbitrary")),
        name="outproj_ffn",
    )(x, o, w_out.astype(BF16), norm_g.reshape(1, d), wgu, wgu, w_down.astype(BF16))


def _rows_select(x, group, nrows):
    w = x.shape[-1]
    keep = _iota((nrows, w), 1) // group == _iota((nrows, w), 0)
    return jnp.where(keep, jnp.broadcast_to(x.astype(F32), (nrows, w)), 0.0).astype(x.dtype)


def _fold_chunks(x):
    return x[:, 0:LANES] + x[:, LANES:2 * LANES] + x[:, 2 * LANES:3 * LANES] + x[:, 3 * LANES:4 * LANES]


def _attn0_decode_kernel(pt_ref, lam_ref, subg_ref, dq_ref, sq_ref, dnew_ref, *rest, pages_per_step):
    pp = pages_per_step
    kpages = rest[:pp]
    vpages = rest[pp:2 * pp]
    spages = rest[2 * pp:3 * pp]
    od_ref, os_ref = rest[3 * pp:3 * pp + 2]
    s_ref, md_ref, ld_ref, accd_ref, tail_ref, accs_ref = rest[3 * pp + 2:]
    sweep = pl.program_id(1)
    j = pl.program_id(2)
    n_steps = pl.num_programs(2)
    qd = _rows_select(dq_ref[0], HEAD_DIM, 8)
    qd128 = _fold_chunks(qd.astype(F32)).astype(BF16)
    page = kpages[0].shape[1]
    nrow = page * 4
    rows_of = lambda ref: ref[0, :, 0].reshape(nrow, LANES).astype(BF16)
    new = dnew_ref[0]
    s_new = jnp.sum(qd.astype(F32) * new[:, 0:512].astype(F32), axis=-1, keepdims=True)
    lam = _diff_lambda(lam_ref[...])
    even_row = _iota((8, 1), 0) % 2 == 0

    def combine(pn):
        return jnp.where(even_row, pn - lam * pltpu.roll(pn, 7, 0), 0.0)

    @pl.when((sweep == 0) & (j == 0))
    def _():
        md_ref[...] = s_new
        tail_ref[...] = jnp.zeros_like(tail_ref)
        accs_ref[...] = jnp.zeros_like(accs_ref)

    @pl.when(sweep == 0)
    def _():
        qs = _rows_select(sq_ref[0], HEAD_DIM, 8)
        upper = jnp.where(_iota((page, page), 0) > _iota((page, page), 1), 1.0, 0.0).astype(BF16)
        is_key_row = _iota((8, nrow), 1) % 4 == _iota((8, nrow), 0) // 2
        tops = []
        for i in range(pp):
            s = jnp.where(is_key_row, _dot_nt(qd128, rows_of(kpages[i])), NEG_INF)
            s_ref[j * pp + i] = s
            tops.append(jnp.max(s, axis=-1, keepdims=True))
        md_ref[...] = jnp.maximum(md_ref[...], _tree(jnp.maximum, tops))

        z = jnp.concatenate([_dot(qs, spages[i][0, 0].astype(BF16)) for i in range(pp)], axis=0)
        t = jnp.log1p(jnp.exp(-jnp.abs(z)))
        lsp = -(jnp.maximum(-z, 0.0) + t)
        l1m = -(jnp.maximum(z, 0.0) + t)
        within = _split3_dot(l1m, upper)
        totals = jnp.sum(l1m, axis=-1, keepdims=True)
        run = tail_ref[...]
        tails = []
        for i in range(pp):
            tails.append(run)
            run = run + totals[i * 8:(i + 1) * 8]
        tail_ref[...] = run
        w = jnp.exp(lsp + within + jnp.concatenate(tails, axis=0)).astype(BF16)
        accs_ref[...] += _tree(jnp.add, [_dot_nt(w[i * 8:(i + 1) * 8], spages[i][0, 1].astype(BF16))
                                         for i in range(pp)])

    @pl.when((sweep == 1) & (j == 0))
    def _():
        m = md_ref[...]
        total = lax.fori_loop(0, s_ref.shape[0],
                              lambda g, t: t + jnp.sum(jnp.exp(s_ref[g] - m), axis=-1, keepdims=True),
                              jnp.exp(s_new - m))
        ld_ref[...] = jnp.maximum(total, 1e-30)
        own_head = _iota((8, 512), 1) // LANES == _iota((8, 512), 0) // 2
        v_new = _fold_chunks(jnp.where(own_head, jnp.broadcast_to(new[:, 512:1024].astype(F32), (8, 512)), 0.0))
        p_new = combine(jnp.exp(s_new - m) / ld_ref[...])
        accd_ref[...] = p_new.astype(BF16).astype(F32) * v_new

    @pl.when(sweep == 1)
    def _():
        m, l = md_ref[...], ld_ref[...]
        parts = []
        for i in range(pp):
            pc = combine(jnp.exp(s_ref[j * pp + i] - m) / l)
            parts.append(_dot(pc.astype(BF16), rows_of(vpages[i])))
        accd_ref[...] += _tree(jnp.add, parts)

    @pl.when((sweep == 1) & (j == n_steps - 1))
    def _():
        od_ref[0] = _rms(accd_ref[...], subg_ref[...]) * (1.0 - DIFF_LAMBDA_INIT)
        acs = accs_ref[...]
        os_ref[0] = jnp.sum(jnp.where(_iota(acs.shape, 0) == _iota(acs.shape, 1) // HEAD_DIM, acs, 0.0), axis=0,
                            keepdims=True)


def _attn0_decode(page_table, lam_p, subln_g, dq, sq, dkv, cache_diff, cache_sb, pages_per_step):
    nb, n_pages = page_table.shape
    n_pool, page = cache_diff.shape[:2]
    pp = pages_per_step
    cs = jnp.transpose(cache_sb, (0, 2, 3, 4, 1)).reshape(n_pool, 2, 512, page)
    n_steps = n_pages // pp
    fixed = lambda b, sw, j, pt: (0, 0)
    per_b = lambda b, sw, j, pt: (b, 0, 0)
    newest_first = lambda i: (lambda b, j, pt: pt[b, n_pages - 1 - (j * pp + i)])

    def kspec(i):
        pick = newest_first(i)
        return pl.BlockSpec((1, page, 1, 4, LANES),
                            lambda b, sw, j, pt: (pick(b, jnp.where(sw == 0, j, n_steps - 1), pt), 0, 0, 0, 0))

    def vspec(i):
        pick = newest_first(i)
        return pl.BlockSpec((1, page, 1, 4, LANES),
                            lambda b, sw, j, pt: (pick(b, jnp.where(sw == 1, j, 0), pt), 0, 1, 0, 0))

    def sspec(i):
        pick = newest_first(i)
        return pl.BlockSpec((1, 2, 512, page),
                            lambda b, sw, j, pt: (pick(b, jnp.where(sw == 0, j, n_steps - 1), pt), 0, 0, 0))

    grid_spec = pltpu.PrefetchScalarGridSpec(
        num_scalar_prefetch=1,
        grid=(nb, 2, n_steps),
        in_specs=[pl.BlockSpec((4, HEAD_DIM), fixed), pl.BlockSpec((8, LANES), fixed),
                  pl.BlockSpec((1, 1, 512), per_b), pl.BlockSpec((1, 1, 512), per_b),
                  pl.BlockSpec((1, 1, 1024), per_b)]
                 + [kspec(i) for i in range(pp)] + [vspec(i) for i in range(pp)] + [sspec(i) for i in range(pp)],
        out_specs=[pl.BlockSpec((1, 8, LANES), per_b), pl.BlockSpec((1, 1, 512), per_b)],
        scratch_shapes=[pltpu.VMEM((n_pages, 8, page * 4), F32),
                        pltpu.VMEM((8, 1), F32), pltpu.VMEM((8, 1), F32), pltpu.VMEM((8, LANES), F32),
                        pltpu.VMEM((8, 1), F32), pltpu.VMEM((8, 512), F32)],
    )
    od, osb = pl.pallas_call(
        functools.partial(_attn0_decode_kernel, pages_per_step=pp),
        grid_spec=grid_spec,
        out_shape=[jax.ShapeDtypeStruct((nb, 8, LANES), F32), jax.ShapeDtypeStruct((nb, 1, 512), F32)],
        compiler_params=_cparams(("arbitrary", "arbitrary", "arbitrary")),
        name="attn0_decode",
    )(page_table, lam_p, jnp.repeat(subln_g, 2, axis=0), dq.reshape(nb, 1, 512), sq.reshape(nb, 1, 512),
      dkv.reshape(nb, 1, 1024), *([cache_diff] * (2 * pp)), *([cs] * pp))
    return jnp.concatenate([od[:, 0::2, :].reshape(nb, 512), osb.reshape(nb, 512)], axis=-1)


def _proj1_kernel(x_ref, ng_ref, w_ref, qg_ref, kg_ref, c_ref, sa_ref, sb_ref,
                  q_ref, cmp_ref, sel_ref, win_ref, gate_ref, selkv_ref, winkv_ref):
    h = _rms(x_ref[...], ng_ref[...]).astype(BF16)
    ones64 = _group_ones(LANES, HEAD_DIM)
    c, sa, sb = c_ref[...], sa_ref[...], sb_ref[...]
    scale = HEAD_DIM ** -0.5
    for half in range(2):
        y = _dot(h, w_ref[:, half * 512:(half + 1) * 512])
        for j in range(4):
            sl = slice(j * LANES, (j + 1) * LANES)
            q = _head_norm_rope(y[:, sl], qg_ref[...], ones64, c, sa, sb) * scale
            q_ref[:, half * 512 + j * LANES:half * 512 + (j + 1) * LANES] = q.astype(BF16)
    for i, (rows_ref, bf_ref) in enumerate(((cmp_ref, None), (sel_ref, selkv_ref), (win_ref, winkv_ref))):
        y = _dot(h, w_ref[:, 1024 + i * 512:1024 + (i + 1) * 512])
        for j in range(2):
            sl = slice(j * LANES, (j + 1) * LANES)
            k = _head_norm_rope(y[:, sl], kg_ref[i:i + 1, :], ones64, c, sa, sb)
            rows_ref[:, sl] = k
            if bf_ref is not None:
                bf_ref[:, sl] = k.astype(BF16)
        rows_ref[:, 256:512] = y[:, 256:512]
        if bf_ref is not None:
            bf_ref[:, 256:512] = y[:, 256:512].astype(BF16)
    gate_ref[...] = jax.nn.sigmoid(_dot(h, w_ref[:, 2560:2688]))


def _proj1(x, norm_g, w_in, q_g, k_g, tabs, tm):
    t, d = x.shape
    tab_blocks = tabs[0].shape[0] // tm
    row = lambda i: (i, 0)
    fixed = lambda i: (0, 0)
    tab = lambda i: (i % tab_blocks, 0)
    w = jnp.pad(w_in, ((0, 0), (0, 2688 - w_in.shape[1]))).astype(BF16)
    rep = LANES // HEAD_DIM
    outs = [jax.ShapeDtypeStruct((t, 1024), BF16), jax.ShapeDtypeStruct((t, 512), F32),
            jax.ShapeDtypeStruct((t, 512), F32), jax.ShapeDtypeStruct((t, 512), F32),
            jax.ShapeDtypeStruct((t, LANES), F32), jax.ShapeDtypeStruct((t, 512), BF16),
            jax.ShapeDtypeStruct((t, 512), BF16)]
    return pl.pallas_call(
        _proj1_kernel,
        grid=(t // tm,),
        in_specs=[pl.BlockSpec((tm, d), row), pl.BlockSpec((1, d), fixed), pl.BlockSpec(w.shape, fixed),
                  pl.BlockSpec((1, LANES), fixed), pl.BlockSpec((3, LANES), fixed),
                  pl.BlockSpec((tm, LANES), tab), pl.BlockSpec((tm, LANES), tab), pl.BlockSpec((tm, LANES), tab)],
        out_specs=[pl.BlockSpec((tm, o.shape[1]), row) for o in outs],
        out_shape=outs,
        compiler_params=_cparams(("arbitrary",)),
        name="proj1",
    )(x, norm_g.reshape(1, d), w, jnp.tile(q_g.reshape(1, HEAD_DIM), (1, rep)), jnp.tile(k_g, (1, rep)), *tabs)


def _gelu_tanh(x):
    return 0.5 * x * (1.0 + jnp.tanh(0.7978845608028654 * (x + 0.044715 * x * x * x)))


def _compress_kernel(*refs, n_in, n_prefetch):
    refs = refs[n_prefetch:]
    row_refs = refs[:n_in]
    posv_ref, w1_ref, w2_ref, out_ref, carry_ref = refs[n_in:]
    t = pl.program_id(1)
    paged = len(row_refs[0].shape) == 5
    chunks = row_refs[0].shape[3] // CMP_STRIDE if paged else row_refs[0].shape[1]
    m_rows = chunks * n_in
    first = _iota((m_rows, 1), 0) == 0
    for i in range(2):
        for p in range(2):
            cols = []
            for l in range(CMP_STRIDE):
                c0 = l * 512 + i * 256 + p * LANES
                if paged:
                    pieces = [r[0, i, p, pl.ds(l, chunks, stride=CMP_STRIDE), :] for r in row_refs]
                else:
                    pieces = [r[0, :, c0:c0 + LANES] for r in row_refs]
                cols.append(pieces[0] if n_in == 1 else jnp.concatenate(pieces, axis=0))
            xcat = jnp.concatenate(cols, axis=1)
            a = _dot((xcat + posv_ref[i, 0]).astype(BF16), w1_ref[i, 0])
            b = _dot((xcat + posv_ref[i, 1]).astype(BF16), w1_ref[i, 1])
            prev = jnp.where(t == 0, jnp.zeros((1, 512), F32), carry_ref[2 * i + p, 0:1, :])
            a_prev = jnp.where(first, prev, pltpu.roll(a, 1, 0))
            carry_ref[2 * i + p, 0:1, :] = a[m_rows - 1:m_rows, :]
            hid = _gelu_tanh(a_prev + b)
            out = _dot(hid.astype(BF16), w2_ref[i])
            out = jnp.where(first & (t == 0), 0.0, out)
            out_ref[0, :, i * 256 + p * LANES:i * 256 + (p + 1) * LANES] = out


def _compress_weights(cmp_pos, w1, w2):
    w1r = w1.reshape(2, 2, CMP_STRIDE, HEAD_DIM, w1.shape[-1])
    hid = w1.shape[-1]
    z = jnp.zeros_like(w1r)
    top = jnp.concatenate([w1r, z], axis=-1)
    bot = jnp.concatenate([z, w1r], axis=-1)
    w1bd = jnp.concatenate([top, bot], axis=3).reshape(2, 2, CMP_STRIDE * LANES, 2 * hid).astype(BF16)
    z2 = jnp.zeros_like(w2)
    w2bd = jnp.concatenate([jnp.concatenate([w2, z2], axis=-1), jnp.concatenate([z2, w2], axis=-1)],
                           axis=1).astype(BF16)
    pv = cmp_pos.reshape(2, CMP_STRIDE, 2, HEAD_DIM)
    pv = jnp.transpose(pv, (2, 0, 1, 3))
    posv = jnp.concatenate([pv, pv], axis=-1).reshape(2, 2, 1, CMP_STRIDE * LANES)
    return posv, w1bd, w2bd


def _compress_call(row_arrays, row_specs, grid, out_map, nb, n_chunks, rows_per_step, weights, prefetch=None):
    posv, w1bd, w2bd = weights
    n_in = len(row_arrays)
    nidx = 2 + (1 if prefetch is not None else 0)
    fix = lambda nd: (lambda *a: (0,) * nd)
    in_specs = list(row_specs) + [pl.BlockSpec(posv.shape, fix(4)), pl.BlockSpec(w1bd.shape, fix(4)),
                                  pl.BlockSpec(w2bd.shape, fix(3))]
    out_spec = pl.BlockSpec((1, rows_per_step, 512), out_map)
    scratch = [pltpu.VMEM((4, 8, 512), F32)]
    kern = functools.partial(_compress_kernel, n_in=n_in, n_prefetch=0 if prefetch is None else 1)
    out_shape = jax.ShapeDtypeStruct((nb, n_chunks, 512), F32)
    if prefetch is None:
        return pl.pallas_call(kern, grid=grid, in_specs=in_specs, out_specs=out_spec, out_shape=out_shape,
                              scratch_shapes=scratch, compiler_params=_cparams(("arbitrary", "arbitrary")),
                              name="compress_prompt")(*row_arrays, posv, w1bd, w2bd)
    gs = pltpu.PrefetchScalarGridSpec(num_scalar_prefetch=1, grid=grid, in_specs=in_specs, out_specs=out_spec,
                                      scratch_shapes=scratch)
    return pl.pallas_call(kern, grid_spec=gs, out_shape=out_shape,
                          compiler_params=_cparams(("arbitrary", "arbitrary")),
                          name="compress_decode")(prefetch, *row_arrays, posv, w1bd, w2bd)


def _compress_prompt(cmp_rows, b, s, weights):
    n_chunks = s // CMP_STRIDE
    y = cmp_rows.reshape(b, n_chunks, CMP_STRIDE * 512)
    spec = pl.BlockSpec((1, n_chunks, CMP_STRIDE * 512), lambda bi, t: (bi, 0, 0))
    return _compress_call([y], [spec], (b, 1), lambda bi, t: (bi, 0, 0), b, n_chunks, n_chunks, weights)


def _compress_decode(cache_cmp, page_table, weights, pages_per_step):
    nb, n_pages = page_table.shape
    n_pool, page = cache_cmp.shape[:2]
    cpp = page // CMP_STRIDE
    pp = pages_per_step
    y = jnp.swapaxes(jnp.transpose(cache_cmp, (0, 2, 3, 4, 1)).reshape(n_pool, 2, 2, LANES, page), -1, -2)
    specs = [pl.BlockSpec((1, 2, 2, page, LANES), functools.partial(
        lambda bi, t, pt, i: (pt[bi, t * pp + i], 0, 0, 0, 0), i=i)) for i in range(pp)]
    return _compress_call([y] * pp, specs, (nb, n_pages // pp), lambda bi, t, pt: (bi, t, 0), nb, n_pages * cpp,
                          pp * cpp, weights, prefetch=page_table)


def _overlap_table(n_entries, n_sel, width):
    start = (np.arange(n_entries)[:, None] - 1) * CMP_STRIDE
    j = np.arange(width)[None, :]
    ov = (start < (j + 1) * L_SEL) & (start + L_CMP > j * L_SEL) & (np.arange(n_entries)[:, None] >= 1) & (j < n_sel)
    return jnp.asarray(ov.astype(np.float32), dtype=BF16)


def _masked_softmax(s, mask):
    s = jnp.where(mask, s, NEG_INF)
    m = jnp.max(s, axis=-1, keepdims=True)
    m = jnp.where(m == NEG_INF, 0.0, m)
    e = jnp.where(mask, jnp.exp(s - m), 0.0)
    return e / jnp.maximum(jnp.sum(e, axis=-1, keepdims=True), 1e-30)


def _select_blocks(imp, qpos, n_sel):
    blk = _iota(imp.shape, 1)
    cur = qpos // L_SEL
    valid = blk * L_SEL <= qpos
    forced = (blk == 0) | (blk == cur) | (blk == cur - 1)
    score = jnp.where(valid, imp + jnp.where(forced, FORCE_BONUS, 0.0), NEG_INF)
    rank = jnp.zeros(imp.shape, F32)
    for k in range(n_sel):
        sk = score[:, k:k + 1]
        ahead = (sk > score) | ((sk == score) & (blk > k))
        rank = rank + jnp.where(ahead, 1.0, 0.0)
    return jnp.where(valid & (rank < N_SEL), 1.0, 0.0)


def _online_update(state, s, mask, v, v_keys_on_lanes=False):
    m_run, l_run, acc = state
    s = jnp.where(mask, s, NEG_INF)
    m_new = jnp.maximum(m_run, jnp.max(s, axis=-1, keepdims=True))
    m_safe = jnp.where(m_new == NEG_INF, 0.0, m_new)
    p = jnp.exp(s - m_safe)
    alpha = jnp.exp(m_run - m_safe)
    pv =_dot_nt(p.astype(BF16), v) if v_keys_on_lanes else _dot(p.astype(BF16), v)
    return (m_new, alpha * l_run + jnp.sum(p, axis=-1, keepdims=True), alpha * acc + pv)


def _online_init(rows, width):
    return (jnp.full((rows, 1), NEG_INF, F32), jnp.zeros((rows, 1), F32), jnp.zeros((rows, width), F32))


def _online_finish(state):
    _, l_run, acc = state
    return acc / jnp.maximum(l_run, 1e-30)


def _nsa_prompt_kernel(q_ref, gate_ref, kcvc_ref, selkv_ref, winkv_ref, ovl_ref, expand_ref, o_ref, *, tq, n_sel):
    qi = pl.program_id(1)
    q0 = qi * tq
    nc = kcvc_ref.shape[1]
    lo_mask, hi_mask = _lane_half_masks((tq, LANES))
    qpos1 = q0 + _iota((tq, 1), 0)
    qpos4 = jnp.concatenate([qpos1] * 4, axis=0)
    col4 = _iota((4 * tq, tq), 1)
    gate = gate_ref[0]
    lane_g = _iota((tq, LANES), 1)
    tile4 = lambda a: jnp.concatenate([a] * 4, axis=0)

    for g in range(4):
        p, gh = g // 2, g % 2
        keep = hi_mask if gh else lo_mask
        ksl = slice(p * LANES, (p + 1) * LANES)
        vsl = slice(256 + p * LANES, 256 + (p + 1) * LANES)
        qs = []
        for r in range(4):
            h = g * 4 + r
            qh = q_ref[0, :, (h // 2) * LANES:(h // 2 + 1) * LANES].astype(F32)
            if h % 2 != gh:
                qh = pltpu.roll(qh, HEAD_DIM, 1)
            qs.append(jnp.where(keep, qh, 0.0).astype(BF16))
        qst = jnp.concatenate(qs, axis=0)

        kc = kcvc_ref[0, :, ksl].astype(BF16)
        vc = kcvc_ref[0, :, vsl].astype(BF16)
        ment = _iota((4 * tq, nc), 1)
        cmask = (ment >= 1) & (ment * CMP_STRIDE + (L_CMP - CMP_STRIDE - 1) <= qpos4)
        p_c = _masked_softmax(_dot_nt(qst, kc), cmask)
        o_c = _dot(p_c.astype(BF16), vc)
        psum = p_c[0:tq] + p_c[tq:2 * tq] + p_c[2 * tq:3 * tq] + p_c[3 * tq:4 * tq]
        sel = lax.cond(q0 + tq <= N_SEL * L_SEL,
                       lambda ps: jnp.where(_iota((tq, LANES), 1) * L_SEL <= qpos1, 1.0, 0.0),
                       lambda ps: _select_blocks(_dot(ps.astype(BF16), ovl_ref[...]), qpos1, n_sel),
                       psum).astype(BF16)

        def sel_step(kb, state, diag):
            rows = pl.ds(pl.multiple_of(kb * tq, tq), tq)
            chosen = tile4(_dot(sel, expand_ref[kb])) > 0.5
            if diag:
                chosen = chosen & (q0 + col4 <= qpos4)
            return _online_update(state, _dot_nt(qst, selkv_ref[0, rows, ksl]), chosen, selkv_ref[0, rows, vsl])

        st = lax.fori_loop(0, qi, lambda kb, c: sel_step(kb, c, False), _online_init(4 * tq, LANES))
        o_s = _online_finish(sel_step(qi, st, True))

        def win_step(kb, state):
            rows = pl.ds(pl.multiple_of(kb * tq, tq), tq)
            dist = qpos4 - (kb * tq + col4)
            inside = (dist >= 0) & (dist < WINDOW)
            return _online_update(state, _dot_nt(qst, winkv_ref[0, rows, ksl]), inside, winkv_ref[0, rows, vsl])

        kb_lo = jnp.maximum(qi - (WINDOW + tq - 1) // tq, 0)
        o_w = _online_finish(lax.fori_loop(kb_lo, qi + 1, win_step, _online_init(4 * tq, LANES)))

        placed = []
        for r in range(4):
            h = g * 4 + r
            rs = slice(r * tq, (r + 1) * tq)
            gsel = lambda c: jnp.sum(jnp.where(lane_g == c, gate, 0.0), axis=-1, keepdims=True)
            oh = o_c[rs] * gsel(3 * h) + o_s[rs] * gsel(3 * h + 1) + o_w[rs] * gsel(3 * h + 2)
            if h % 2 != gh:
                oh = pltpu.roll(oh, HEAD_DIM, 1)
            placed.append(oh)
        for c in range(2):
            o_ref[0, :, (g * 2 + c) * LANES:(g * 2 + c + 1) * LANES] = jnp.where(lo_mask, placed[2 * c], placed[2 * c + 1])


def _nsa_prompt(q, gate, kcvc, selkv, winkv, b, s, tq):
    n_sel = -(-s // L_SEL)
    nc = kcvc.shape[1]
    ovl = _overlap_table(nc, n_sel, LANES)
    kpos = np.arange(s).reshape(s // tq, 1, tq)
    expand = jnp.asarray((kpos // L_SEL == np.arange(LANES).reshape(1, LANES, 1)).astype(np.float32), dtype=BF16)
    r3 = lambda a: a.reshape(b, s, a.shape[-1])
    tile = lambda w: pl.BlockSpec((1, tq, w), lambda bi, qi: (bi, qi, 0))
    full = lambda n, w: pl.BlockSpec((1, n, w), lambda bi, qi: (bi, 0, 0))
    out = pl.pallas_call(
        functools.partial(_nsa_prompt_kernel, tq=tq, n_sel=n_sel),
        grid=(b, s // tq),
        in_specs=[tile(1024), tile(LANES), full(nc, 512), full(s, 512), full(s, 512),
                  pl.BlockSpec(ovl.shape, lambda bi, qi: (0, 0)), pl.BlockSpec(expand.shape, lambda bi, qi: (0, 0, 0))],
        out_specs=tile(1024),
        out_shape=jax.ShapeDtypeStruct((b, s, 1024), F32),
        compiler_params=_cparams(("arbitrary", "arbitrary")),
        name="nsa_prompt",
    )(r3(q), r3(gate), kcvc, r3(selkv), r3(winkv), ovl, expand)
    return out.reshape(b * s, 1024)


def _nsa_decode_kernel(pt_ref, q_ref, gate_ref, kcvc_ref, ovl_ref, selnew_ref, winnew_ref, winnewf_ref, state_ref,
                       *rest, pages_per_step, n_sel, past_len):
    pp = pages_per_step
    kpages = rest[:pp]
    vpages = rest[pp:2 * pp]
    o_ref, winout_ref = rest[2 * pp:2 * pp + 2]
    s_ref, qbig_ref, sel_ref, oc_ref, m_ref, l_ref, acc_ref = rest[2 * pp + 2:]
    sweep = pl.program_id(1)
    j = pl.program_id(2)
    page = kpages[0].shape[-1]
    head_of_lane = _iota((16, 256), 1) // HEAD_DIM
    row16 = _iota((16, 256), 0)
    own = head_of_lane == row16 // 4
    spread = jnp.where((_iota((16, 16), 0) // 4) * 4 == _iota((16, 16), 1), 1.0, 0.0).astype(BF16)

    @pl.when((sweep == 0) & (j == 0))
    def _():
        tile_lanes = jnp.where(_iota((HEAD_DIM, 256), 0) == _iota((HEAD_DIM, 256), 1) % HEAD_DIM, 1.0, 0.0)
        qb = _dot(q_ref[0], tile_lanes.astype(BF16))
        qbig = jnp.where(own, qb, 0.0).astype(BF16)
        qbig_ref[...] = qbig
        nc = kcvc_ref.shape[1]
        ment = _iota((16, nc), 1)
        cmask = (ment >= 1) & (ment * CMP_STRIDE + (L_CMP - CMP_STRIDE - 1) <= past_len)
        p_c = _masked_softmax(_dot_nt(qbig, kcvc_ref[0, :, 0:256].astype(BF16)), cmask)
        oc_ref[...] = _dot(p_c.astype(BF16), kcvc_ref[0, :, 256:512].astype(BF16))
        pair = p_c + pltpu.roll(p_c, 15, 0)
        psum = pair + pltpu.roll(pair, 14, 0)
        imp = _dot(psum.astype(BF16), ovl_ref[...])
        sel_ref[...] = _select_blocks(imp, jnp.full((16, 1), past_len, jnp.int32), n_sel)
        m_ref[...] = jnp.full(m_ref.shape, NEG_INF, F32)

    qbig = qbig_ref[...]
    qf = qbig.astype(F32)
    sel = sel_ref[...].astype(BF16)
    blocks_per_page = page // L_SEL
    knew = selnew_ref[0]
    s_new = jnp.sum(qf * knew[:, 0:256].astype(F32), axis=-1, keepdims=True)

    def new_token_chosen():
        last = jnp.where(_iota((256, LANES), 0) == past_len // L_SEL, 1.0, 0.0).astype(BF16)
        return _dot(spread, _dot(sel, last).astype(BF16))[:, 0:1] > 0.5

    @pl.when(sweep == 0)
    def _():
        sel_heads = _dot(spread, sel)
        blk_lane = _iota((16, 256), 1)
        blk_in_page = _iota((16, page), 1) // L_SEL
        tops = []
        for i in range(pp):
            pg = j * pp + i
            chosen = jnp.zeros((16, page), F32)
            for c in range(blocks_per_page):
                col = jnp.sum(jnp.where(blk_lane == pg * blocks_per_page + c, sel_heads, 0.0), axis=-1, keepdims=True)
                chosen = jnp.where(blk_in_page == c, col, chosen)
            s = jnp.where(chosen > 0.5, _dot(qbig, kpages[i][0, 0].astype(BF16)), NEG_INF)
            s_ref[pg] = s
            tops.append(jnp.max(s, axis=-1, keepdims=True))
        m_ref[...] = jnp.maximum(m_ref[...], _tree(jnp.maximum, tops))

    @pl.when((sweep == 1) & (j == 0))
    def _():
        has_new = new_token_chosen()
        m = jnp.maximum(m_ref[...], jnp.where(has_new, s_new, NEG_INF))
        m = jnp.where(m == NEG_INF, 0.0, m)
        m_ref[...] = m
        e_new = jnp.where(has_new, jnp.exp(s_new - m), 0.0)
        total = lax.fori_loop(0, s_ref.shape[0],
                              lambda g, t: t + jnp.sum(jnp.exp(s_ref[g] - m), axis=-1, keepdims=True), e_new)
        l_ref[...] = jnp.maximum(total, 1e-30)
        acc_ref[...] = (e_new / l_ref[...]).astype(BF16).astype(F32) * knew[:, 256:512].astype(F32)

    @pl.when(sweep == 1)
    def _():
        m, l = m_ref[...], l_ref[...]
        acc_ref[...] += _tree(jnp.add, [_dot_nt((jnp.exp(s_ref[j * pp + i] - m) / l).astype(BF16),
                                                vpages[i][0, 0].astype(BF16)) for i in range(pp)])

    @pl.when((sweep == 1) & (j == pl.num_programs(2) - 1))
    def _():
        o_s = acc_ref[...]

        w_buf = state_ref.shape[-1]
        s_w = _dot(qbig, state_ref[0, 0].astype(BF16))
        wpos = past_len - w_buf + _iota(s_w.shape, 1)
        wmask = (past_len - wpos < WINDOW) & (wpos >= 0)
        wnew = winnew_ref[0]
        s_wn = jnp.sum(qf * wnew[:, 0:256].astype(F32), axis=-1, keepdims=True)
        mw = jnp.maximum(jnp.max(jnp.where(wmask, s_w, NEG_INF), axis=-1, keepdims=True), s_wn)
        e = jnp.where(wmask, jnp.exp(s_w - mw), 0.0)
        en = jnp.exp(s_wn - mw)
        denom = jnp.maximum(jnp.sum(e, axis=-1, keepdims=True) + en, 1e-30)
        o_w = (_dot_nt((e / denom).astype(BF16), state_ref[0, 1].astype(BF16))
               + (en / denom).astype(BF16).astype(F32) * wnew[:, 256:512].astype(F32))
        newest = _iota((256, w_buf), 1) == w_buf - 1
        for i in range(2):
            winout_ref[0, i] = jnp.where(newest, winnewf_ref[0, i], pltpu.roll(state_ref[0, i], w_buf - 1, 1))

        gate = jnp.broadcast_to(gate_ref[0], (16, LANES))
        lane = _iota((16, LANES), 1)
        hrow = _iota((16, LANES), 0)
        gsel = lambda br: jnp.sum(jnp.where(lane == 3 * hrow + br, gate, 0.0), axis=-1, keepdims=True)
        o = oc_ref[...] * gsel(0) + o_s * gsel(1) + o_w * gsel(2)
        fold = jnp.where(_iota((256, HEAD_DIM), 0) % HEAD_DIM == _iota((256, HEAD_DIM), 1), 1.0, 0.0).astype(BF16)
        o_ref[0] = _split3_dot(jnp.where(own, o, 0.0), fold)


def _nsa_decode(page_table, q, gate, kcvc, selkv_new, winkv_new, win_new, state, cache_sel, pages_per_step):
    nb, n_pages = page_table.shape
    n_pool, page = cache_sel.shape[:2]
    past_len = n_pages * page
    n_sel = -(-(past_len + 1) // L_SEL)
    assert n_sel <= 256
    pp = pages_per_step
    nc = kcvc.shape[1]
    w_buf = state.shape[1]
    ovl = _overlap_table(nc, n_sel, 256)
    keys_last = lambda a: jnp.transpose(a, (0, 2, 3, 4, 1)).reshape(a.shape[0], 2, 256, a.shape[1])
    cs = keys_last(cache_sel)
    n_steps = n_pages // pp
    per_b = lambda b, sw, j, pt: (b, 0, 0)
    per_b4 = lambda b, sw, j, pt: (b, 0, 0, 0)
    kspecs = [pl.BlockSpec((1, 1, 256, page), functools.partial(
        lambda b, sw, j, pt, i: (pt[b, jnp.where(sw == 0, j, n_steps - 1) * pp + i], 0, 0, 0), i=i)) for i in range(pp)]
    vspecs = [pl.BlockSpec((1, 1, 256, page), functools.partial(
        lambda b, sw, j, pt, i: (pt[b, jnp.where(sw == 1, j, 0) * pp + i], 1, 0, 0), i=i)) for i in range(pp)]
    grid_spec = pltpu.PrefetchScalarGridSpec(
        num_scalar_prefetch=1,
        grid=(nb, 2, n_steps),
        in_specs=[pl.BlockSpec((1, 16, HEAD_DIM), per_b), pl.BlockSpec((1, 1, LANES), per_b),
                  pl.BlockSpec((1, nc, 512), per_b), pl.BlockSpec(ovl.shape, lambda b, sw, j, pt: (0, 0)),
                  pl.BlockSpec((1, 1, 512), per_b), pl.BlockSpec((1, 1, 512), per_b), pl.BlockSpec((1, 2, 256, 1), per_b4),
                  pl.BlockSpec((1, 2, 256, w_buf), per_b4)] + kspecs + vspecs,
        out_specs=[pl.BlockSpec((1, 16, HEAD_DIM), per_b), pl.BlockSpec((1, 2, 256, w_buf), per_b4)],
        scratch_shapes=[pltpu.VMEM((n_pages, 16, page), F32),
                        pltpu.VMEM((16, 256), BF16), pltpu.VMEM((16, 256), F32), pltpu.VMEM((16, 256), F32),
                        pltpu.VMEM((16, 1), F32), pltpu.VMEM((16, 1), F32), pltpu.VMEM((16, 256), F32)],
    )
    o, win_out = pl.pallas_call(
        functools.partial(_nsa_decode_kernel, pages_per_step=pp, n_sel=n_sel, past_len=past_len),
        grid_spec=grid_spec,
        out_shape=[jax.ShapeDtypeStruct((nb, 16, HEAD_DIM), F32), jax.ShapeDtypeStruct((nb, 2, 256, w_buf), F32)],
        compiler_params=_cparams(("arbitrary", "arbitrary", "arbitrary")),
        name="nsa_decode",
    )(page_table, q.reshape(nb, 16, HEAD_DIM), gate.reshape(nb, 1, LANES), kcvc, ovl, selkv_new.reshape(nb, 1, 512),
      winkv_new.reshape(nb, 1, 512), win_new.reshape(nb, 2, 256, 1), keys_last(state), *([cs] * (2 * pp)))
    win_rows = jnp.transpose(win_out.reshape(nb, 2, 4, HEAD_DIM, w_buf), (0, 4, 1, 2, 3))
    return o.reshape(nb, 1024), win_rows


def _outproj_router_kernel(x_ref, o_ref, wo_ref, ng_ref, wr_ref, x1_ref, h_ref, comb_ref):
    x1 = x_ref[...] + _dot(o_ref[...].astype(BF16), wo_ref[...])
    x1_ref[...] = x1
    hf = _rms(x1, ng_ref[...])
    h = hf.astype(BF16)
    h_ref[...] = hf.astype(h_ref.dtype)
    logits = _dot(h, wr_ref[...])
    lane = _iota(logits.shape, 1)
    logits = jnp.where(lane < N_EXPERTS, logits, NEG_INF)
    m1 = jnp.max(logits, axis=-1, keepdims=True)
    i1 = jnp.min(jnp.where(logits == m1, lane, LANES), axis=-1, keepdims=True)
    rest = jnp.where(lane == i1, NEG_INF, logits)
    m2 = jnp.max(rest, axis=-1, keepdims=True)
    i2 = jnp.min(jnp.where(rest == m2, lane, LANES), axis=-1, keepdims=True)
    e2 = jnp.exp(m2 - m1)
    comb_ref[...] = jnp.where(lane == i1, 1.0 / (1.0 + e2), 0.0) + jnp.where(lane == i2, e2 / (1.0 + e2), 0.0)


def _outproj_router(x, o, w_out, norm_g, w_router, tm, h_dtype=BF16):
    t, d = x.shape
    row = lambda i: (i, 0)
    fixed = lambda i: (0, 0)
    wr = jnp.pad(w_router, ((0, 0), (0, LANES - w_router.shape[1]))).astype(BF16)
    outs = [jax.ShapeDtypeStruct((t, d), F32), jax.ShapeDtypeStruct((t, d), h_dtype), jax.ShapeDtypeStruct((t, LANES), F32)]
    return pl.pallas_call(
        _outproj_router_kernel,
        grid=(t // tm,),
        in_specs=[pl.BlockSpec((tm, d), row), pl.BlockSpec((tm, o.shape[1]), row), pl.BlockSpec(w_out.shape, fixed),
                  pl.BlockSpec((1, d), fixed), pl.BlockSpec(wr.shape, fixed)],
        out_specs=[pl.BlockSpec((tm, a.shape[1]), row) for a in outs],
        out_shape=outs,
        compiler_params=_cparams(("arbitrary",)),
        name="outproj_router",
    )(x, o, w_out.astype(BF16), norm_g.reshape(1, d), wr)


def _moe_kernel(x1_ref, h_ref, comb_ref, wg_ref, wu_ref, wd_ref, y_ref, acc_ref):
    e = pl.program_id(1)
    j = pl.program_id(2)

    @pl.when((e == 0) & (j == 0))
    def _():
        acc_ref[...] = jnp.zeros_like(acc_ref)

    h = h_ref[...]
    g = _dot(h, wg_ref[0])
    u = _dot(h, wu_ref[0])
    a = (g * jax.nn.sigmoid(g) * u).astype(BF16)
    comb = comb_ref[...]
    w_e = jnp.sum(jnp.where(_iota(comb.shape, 1) == e, comb, 0.0), axis=-1, keepdims=True)
    acc_ref[...] += w_e * _dot(a, wd_ref[0])

    @pl.when((e == pl.num_programs(1) - 1) & (j == pl.num_programs(2) - 1))
    def _():
        y_ref[...] = x1_ref[...] + acc_ref[...]


def _moe(x1, h, comb, w_gate_up, w_down, tm, tf):
    t, d = x1.shape
    n_e, ff = w_down.shape[:2]
    nj = ff // tf
    row = lambda i, e, j: (i, 0)
    wgu = w_gate_up.astype(BF16)
    return pl.pallas_call(
        _moe_kernel,
        grid=(t // tm, n_e, nj),
        in_specs=[pl.BlockSpec((tm, d), row), pl.BlockSpec((tm, d), row), pl.BlockSpec((tm, LANES), row),
                  pl.BlockSpec((1, d, tf), lambda i, e, j: (e, 0, j)),
                  pl.BlockSpec((1, d, tf), lambda i, e, j: (e, 0, nj + j)),
                  pl.BlockSpec((1, tf, d), lambda i, e, j: (e, j, 0))],
        out_specs=pl.BlockSpec((tm, d), row),
        out_shape=jax.ShapeDtypeStruct((t, d), F32),
        scratch_shapes=[pltpu.VMEM((tm, d), F32)],
        compiler_params=_cparams(("arbitrary", "arbitrary", "arbitrary")),
        name="moe",
    )(x1, h, comb, wgu, wgu, w_down.astype(BF16))


def _moe_plan(comb, tm):
    t = comb.shape[0]
    routed = comb[:, :N_EXPERTS] > 0.0
    r32 = routed.astype(jnp.int32)
    counts = jnp.sum(r32, axis=0)
    padded = ((counts + tm - 1) // tm) * tm
    ends = jnp.cumsum(padded)
    starts = ends - padded
    dest = starts[None, :] + jnp.cumsum(r32, axis=0) - r32
    n_tiles = (2 * t) // tm + N_EXPERTS
    tile_start = jnp.arange(n_tiles, dtype=jnp.int32) * tm
    tile_expert = jnp.minimum(jnp.sum((tile_start[:, None] >= ends[None, :]).astype(jnp.int32), axis=1),
                              N_EXPERTS - 1)
    tile_used = (tile_start < ends[-1]).astype(jnp.int32)
    first = jnp.argmax(routed, axis=1)
    last = N_EXPERTS - 1 - jnp.argmax(routed[:, ::-1], axis=1)
    row_a = jnp.take_along_axis(dest, first[:, None], axis=1)[:, 0]
    row_b = jnp.take_along_axis(dest, last[:, None], axis=1)[:, 0]
    return n_tiles, tile_expert, tile_used, row_a.astype(jnp.int32), row_b.astype(jnp.int32)


def _moe_scatter_kernel(ra_ref, rb_ref, h_ref, grouped_in, grouped_hbm, sem):
    tc = h_ref.shape[0]

    def start(r, c):
        pltpu.make_async_copy(h_ref.at[pl.ds(r, 1), :], grouped_hbm.at[pl.ds(ra_ref[0, 0, r], 1), :], sem).start()
        pltpu.make_async_copy(h_ref.at[pl.ds(r, 1), :], grouped_hbm.at[pl.ds(rb_ref[0, 0, r], 1), :], sem).start()
        return c

    lax.fori_loop(0, tc, start, 0, unroll=8)
    for _ in range(2):
        pltpu.make_async_copy(h_ref, grouped_hbm.at[pl.ds(0, tc), :], sem).wait()


def _moe_scatter(h, row_a, row_b, n_rows, tc):
    t, d = h.shape
    smem_row = pl.BlockSpec((1, 1, tc), lambda i: (i, 0, 0), memory_space=pltpu.SMEM)
    return pl.pallas_call(
        _moe_scatter_kernel,
        grid=(t // tc,),
        in_specs=[smem_row, smem_row, pl.BlockSpec((tc, d), lambda i: (i, 0)), pl.BlockSpec(memory_space=pl.ANY)],
        out_specs=pl.BlockSpec(memory_space=pl.ANY),
        out_shape=jax.ShapeDtypeStruct((n_rows, d), h.dtype),
        input_output_aliases={3: 0},
        scratch_shapes=[pltpu.SemaphoreType.DMA(())],
        compiler_params=_cparams(("arbitrary",)),
        name="moe_scatter",
    )(row_a.reshape(t // tc, 1, tc), row_b.reshape(t // tc, 1, tc), h, jnp.zeros((n_rows, d), h.dtype))


def _moe_rows_kernel(te_ref, tu_ref, x_ref, wg_ref, wu_ref, wd_ref, y_ref, acc_ref):
    i = pl.program_id(0)
    j = pl.program_id(1)
    used = tu_ref[i] > 0

    @pl.when(used & (j == 0))
    def _():
        acc_ref[...] = jnp.zeros_like(acc_ref)

    @pl.when(used)
    def _():
        h = x_ref[...].astype(BF16)
        g = _dot(h, wg_ref[0])
        u = _dot(h, wu_ref[0])
        acc_ref[...] += _dot((g * jax.nn.sigmoid(g) * u).astype(BF16), wd_ref[0])

    last = j == pl.num_programs(1) - 1

    @pl.when(used & last)
    def _():
        y_ref[...] = acc_ref[...]

    @pl.when(jnp.logical_not(used) & last)
    def _():
        y_ref[...] = jnp.zeros_like(y_ref)


def _moe_rows(grouped, tile_expert, tile_used, w_gate_up, w_down, tm, tf):
    d = grouped.shape[1]
    n_tiles = grouped.shape[0] // tm
    n_e, ff = w_down.shape[:2]
    nj = ff // tf
    wgu = w_gate_up.astype(BF16)
    grid_spec = pltpu.PrefetchScalarGridSpec(
        num_scalar_prefetch=2,
        grid=(n_tiles, nj),
        in_specs=[pl.BlockSpec((tm, d), lambda i, j, te, tu: (i, 0)),
                  pl.BlockSpec((1, d, tf), lambda i, j, te, tu: (te[i], 0, j)),
                  pl.BlockSpec((1, d, tf), lambda i, j, te, tu: (te[i], 0, nj + j)),
                  pl.BlockSpec((1, tf, d), lambda i, j, te, tu: (te[i], j, 0))],
        out_specs=pl.BlockSpec((tm, d), lambda i, j, te, tu: (i, 0)),
        scratch_shapes=[pltpu.VMEM((tm, d), F32)],
    )
    return pl.pallas_call(
        _moe_rows_kernel,
        grid_spec=grid_spec,
        out_shape=jax.ShapeDtypeStruct((n_tiles * tm, d), F32),
        compiler_params=_cparams(("arbitrary", "arbitrary")),
        name="moe_rows",
    )(tile_expert, tile_used, grouped, wgu, wgu, w_down.astype(BF16))


def _moe_combine_kernel(ra_ref, rb_ref, x1_ref, comb_ref, ys_hbm, out_ref, buf_ref, sem):
    tc = x1_ref.shape[0]

    def row_copy(slot, r, src):
        return pltpu.make_async_copy(ys_hbm.at[pl.ds(src, 1), :], buf_ref.at[slot, pl.ds(r, 1), :], sem)

    def start(r, c):
        row_copy(0, r, ra_ref[0, 0, r]).start()
        row_copy(1, r, rb_ref[0, 0, r]).start()
        return c

    lax.fori_loop(0, tc, start, 0, unroll=8)
    for slot in range(2):
        pltpu.make_async_copy(ys_hbm.at[pl.ds(0, tc), :], buf_ref.at[slot], sem).wait()
    comb = comb_ref[...]
    lane = _iota(comb.shape, 1)
    on = comb > 0.0
    ia = jnp.min(jnp.where(on, lane, LANES), axis=-1, keepdims=True)
    ib = jnp.max(jnp.where(on, lane, -1), axis=-1, keepdims=True)
    w_a = jnp.sum(jnp.where(lane == ia, comb, 0.0), axis=-1, keepdims=True)
    w_b = jnp.sum(jnp.where((lane == ib) & (ib != ia), comb, 0.0), axis=-1, keepdims=True)
    out_ref[...] = x1_ref[...] + (w_a * buf_ref[0] + w_b * buf_ref[1])


def _moe_combine(x1, comb, ys, row_a, row_b, tc):
    t, d = x1.shape
    row = lambda i: (i, 0)
    smem_row = pl.BlockSpec((1, 1, tc), lambda i: (i, 0, 0), memory_space=pltpu.SMEM)
    return pl.pallas_call(
        _moe_combine_kernel,
        grid=(t // tc,),
        in_specs=[smem_row, smem_row, pl.BlockSpec((tc, d), row), pl.BlockSpec((tc, LANES), row),
                  pl.BlockSpec(memory_space=pl.ANY)],
        out_specs=pl.BlockSpec((tc, d), row),
        out_shape=jax.ShapeDtypeStruct((t, d), F32),
        scratch_shapes=[pltpu.VMEM((2, tc, d), F32), pltpu.SemaphoreType.DMA(())],
        compiler_params=_cparams(("arbitrary",)),
        name="moe_combine",
    )(row_a.reshape(t // tc, 1, tc), row_b.reshape(t // tc, 1, tc), x1, comb, ys)


def _moe_grouped(x1, h, comb, w_gate_up, w_down, tm, tf, tc):
    n_tiles, tile_expert, tile_used, row_a, row_b = _moe_plan(comb, tm)
    grouped = _moe_scatter(h, row_a, row_b, n_tiles * tm, tc)
    ys = _moe_rows(grouped, tile_expert, tile_used, w_gate_up, w_down, tm, tf)
    return _moe_combine(x1, comb, ys, row_a, row_b, tc)


def kernel(x_prompt, x_sample, cache_l0_diff, cache_l0_sb, cache_l1_cmp, cache_l1_sel, state_l1_win, page_table, l0_norm_mix, l0_w_in, l0_diff_q_norm, l0_diff_k_norm, l0_diff_lambda, l0_diff_subln, l0_w_out, l0_norm_ffn, l0_ffn_w_gate_up, l0_ffn_w_down, l1_norm_mix, l1_w_in, l1_q_norm, l1_k_norm, l1_cmp_pos, l1_cmp_w1, l1_cmp_w2, l1_w_out, l1_norm_ffn, l1_router, l1_moe_w_gate_up, l1_moe_w_down):
    b, s, d = x_prompt.shape
    nb = x_sample.shape[0]
    n_pages, page = page_table.shape[1], cache_l0_diff.shape[1]
    past_len = n_pages * page
    tm = min(512, s)
    tq = min(256, s)
    dec_pages = min(16, n_pages)
    cmp_pages = min(32, n_pages)
    ff_tile = l0_ffn_w_down.shape[0] // 2
    moe_tile = l1_moe_w_down.shape[1] // 7
    tabs_p = _rope_tables(jnp.arange(s, dtype=jnp.int32))
    tabs_s = _rope_tables(jnp.full((nb,), past_len, jnp.int32))
    xp = x_prompt.reshape(b * s, d)
    xs = x_sample.reshape(nb, d)

    dq, diff_p, sq, sb_p, dkv, skv = _proj0(xp, l0_norm_mix, l0_w_in, l0_diff_q_norm, l0_diff_k_norm, tabs_p, tm)
    o = _attn0_prompt(l0_diff_lambda, l0_diff_subln, dq, sq, dkv, skv, b, s, tq)
    xp = _outproj_ffn(xp, o, l0_w_out, l0_norm_ffn, l0_ffn_w_gate_up, l0_ffn_w_down, tm, ff_tile)
    dq, diff_s, sq, sb_s, dkv, skv = _proj0(xs, l0_norm_mix, l0_w_in, l0_diff_q_norm, l0_diff_k_norm, tabs_s, nb)
    o = _attn0_decode(page_table, l0_diff_lambda, l0_diff_subln, dq, sq, dkv, cache_l0_diff, cache_l0_sb, dec_pages)
    xs = _outproj_ffn(xs, o, l0_w_out, l0_norm_ffn, l0_ffn_w_gate_up, l0_ffn_w_down, nb, ff_tile)

    cw = _compress_weights(l1_cmp_pos, l1_cmp_w1, l1_cmp_w2)
    q, cmp_p, sel_p, win_p, gate, selkv, winkv = _proj1(xp, l1_norm_mix, l1_w_in, l1_q_norm, l1_k_norm, tabs_p, tm)
    kcvc = _compress_prompt(cmp_p, b, s, cw)
    o = _nsa_prompt(q, gate, kcvc, selkv, winkv, b, s, tq)
    x1, h, comb = _outproj_router(xp, o, l1_w_out, l1_norm_ffn, l1_router, tm, h_dtype=F32)
    xp = _moe_grouped(x1, h, comb, l1_moe_w_gate_up, l1_moe_w_down, tm, l1_moe_w_down.shape[1] // 2, min(256, s))

    q, cmp_s, sel_s, win_s_new, gate, selkv, winkv = _proj1(xs, l1_norm_mix, l1_w_in, l1_q_norm, l1_k_norm, tabs_s, nb)
    kcvc = _compress_decode(cache_l1_cmp, page_table, cw, cmp_pages)
    o, win_s = _nsa_decode(page_table, q, gate, kcvc, selkv, winkv, win_s_new, state_l1_win, cache_l1_sel,
                           min(32, n_pages))
    x1, h, comb = _outproj_router(xs, o, l1_w_out, l1_norm_ffn, l1_router, nb)
    xs = _moe(x1, h, comb, l1_moe_w_gate_up, l1_moe_w_down, nb, moe_tile)

    w_keep = min(WINDOW, s)
    hk = cmp_p.shape[-1] // (2 * HEAD_DIM)
    rows5 = lambda a, n: a.reshape(n, -1, 2, a.shape[-1] // (2 * HEAD_DIM), HEAD_DIM)
    return (xp.reshape(b, s, d), xs.reshape(nb, 1, d),
            diff_p.reshape(b, s, 2, 4, 2 * HEAD_DIM), sb_p.reshape(b, s, 2, 8, HEAD_DIM),
            rows5(cmp_p, b), rows5(sel_p, b), rows5(win_p, b)[:, s - w_keep:],
            diff_s.reshape(nb, 1, 2, 4, 2 * HEAD_DIM), sb_s.reshape(nb, 1, 2, 8, HEAD_DIM),
            rows5(cmp_s, nb), rows5(sel_s, nb), win_s)
```
